```python
import jax, jax.numpy as jnp
from jax import lax
import numpy as np

D_MODEL = 1024
BATCH = 4
SEQ = 4096
DEPTH = 2

N_A_LAYERS = DEPTH // 2
N_B_LAYERS = DEPTH - N_A_LAYERS
D_FF = 2816
EPS = 1e-6
NEG = -1e30

M_HEADS = 8
M_DV = D_MODEL // M_HEADS
M_DQK = M_DV // 2
M_CHUNK = 64

N_QH = 16
N_KVH = 4
HEAD_DIM = 64
CMP_BLOCK = 32
CMP_STRIDE = 16
CMP_HIDDEN = 256
SEL_BLOCK = 64
SEL_TOPK = 16
WINDOW = 512
Q_BLOCK = 64
FORCE_SCORE = 1e4

kernel_name = "yoco_mlstm_nsa_macaron"


def rmsnorm(x, g):
    xf = x.astype(jnp.float32)
    y = xf * lax.rsqrt(jnp.mean(xf * xf, axis=-1, keepdims=True) + EPS)
    return (y * g.astype(jnp.float32)).astype(x.dtype)


def swiglu(h, w_in, w_out):
    a, b = jnp.split(h @ w_in, 2, axis=-1)
    return (jax.nn.silu(a) * b) @ w_out


def masked_softmax(s, mask):
    s = jnp.where(mask, s.astype(jnp.float32), NEG)
    e = jnp.where(mask, jnp.exp(s - jnp.max(s, axis=-1, keepdims=True)), 0.0)
    z = jnp.sum(e, axis=-1, keepdims=True)
    return e / jnp.where(z > 0, z, 1.0)


def mlstm_mixer(h, w_in, b_if, g_head, w_out):
    B, T, _ = h.shape
    H, L = M_HEADS, M_CHUNK
    NC = T // L
    qk_w, v_w = H * M_DQK, H * M_DV
    proj = h @ w_in
    q, k, v, ig, fg, og = jnp.split(
        proj, [qk_w, 2 * qk_w, 2 * qk_w + v_w, 2 * qk_w + v_w + H, 2 * qk_w + v_w + 2 * H], axis=-1)

    def to_chunks(a, d):
        return a.astype(jnp.float32).reshape(B, NC, L, H, d).transpose(0, 3, 1, 2, 4)

    q = to_chunks(q, M_DQK) * (M_DQK ** -0.5)
    k = to_chunks(k, M_DQK)
    v = to_chunks(v, M_DV)
    bif = b_if.astype(jnp.float32)
    log_i = to_chunks(ig, 1)[..., 0] + bif[0][:, None, None]
    log_f = jax.nn.log_sigmoid(to_chunks(fg, 1)[..., 0] + bif[1][:, None, None])

    b = jnp.cumsum(log_f, axis=-1)
    g_tot = b[..., -1]
    a = g_tot[..., None] - b + log_i
    m_loc = jnp.max(a, axis=-1)
    w_loc = jnp.exp(a - m_loc[..., None])
    C_loc = jnp.einsum('bhcl,bhclk,bhclv->bhckv', w_loc, k, v)
    n_loc = jnp.einsum('bhcl,bhclk->bhck', w_loc, k)

    def step(carry, inp):
        C, n, m = carry
        g, Cl, nl, ml = inp
        m_new = jnp.maximum(g + m, ml)
        s_old = jnp.exp(g + m - m_new)
        s_new = jnp.exp(ml - m_new)
        C_new = s_old[..., None, None] * C + s_new[..., None, None] * Cl
        n_new = s_old[..., None] * n + s_new[..., None] * nl
        return (C_new, n_new, m_new), (C, n, m)

    init = (jnp.zeros((B, H, M_DQK, M_DV), jnp.float32),
            jnp.zeros((B, H, M_DQK), jnp.float32),
            jnp.zeros((B, H), jnp.float32))
    xs = (jnp.moveaxis(g_tot, 2, 0), jnp.moveaxis(C_loc, 2, 0),
          jnp.moveaxis(n_loc, 2, 0), jnp.moveaxis(m_loc, 2, 0))
    _, (C_prev, n_prev, m_prev) = lax.scan(step, init, xs)
    C_prev = jnp.moveaxis(C_prev, 0, 2)
    n_prev = jnp.moveaxis(n_prev, 0, 2)
    m_prev = jnp.moveaxis(m_prev, 0, 2)

    causal = jnp.tril(jnp.ones((L, L), bool))
    D = jnp.where(causal, b[..., :, None] - b[..., None, :] + log_i[..., None, :], -jnp.inf)
    inter = b + m_prev[..., None]
    m_t = jnp.maximum(inter, jnp.max(D, axis=-1))
    W = jnp.exp(D - m_t[..., None]) * jnp.einsum('bhctk,bhcsk->bhcts', q, k)
    s_inter = jnp.exp(inter - m_t)
    num = (s_inter[..., None] * jnp.einsum('bhctk,bhckv->bhctv', q, C_prev)
           + jnp.einsum('bhcts,bhcsv->bhctv', W, v))
    den = s_inter * jnp.einsum('bhctk,bhck->bhct', q, n_prev) + jnp.sum(W, axis=-1)
    h_out = num / jnp.maximum(jnp.abs(den), jnp.exp(-m_t))[..., None]
    h_out = h_out.astype(h.dtype).transpose(0, 2, 3, 1, 4).reshape(B, T, H, M_DV)
    h_out = rmsnorm(h_out, g_head).reshape(B, T, H * M_DV)
    return (h_out * jax.nn.sigmoid(og)) @ w_out


def nsa_shared_kv(s, kv_norm, kv_w, cmp_pe, cmp_w1, cmp_w2, k_norm):
    B, T, _ = s.shape
    n_cmp = (T - CMP_BLOCK) // CMP_STRIDE + 1
    kv = rmsnorm(s, kv_norm) @ kv_w
    kc, vc, ks, vs, kw, vw = [a.reshape(B, T, N_KVH, HEAD_DIM) for a in jnp.split(kv, 6, axis=-1)]
    idx = jnp.arange(n_cmp)[:, None] * CMP_STRIDE + jnp.arange(CMP_BLOCK)[None, :]

    def compress(a, pe, w1, w2):
        blk = a[:, idx] + pe[:, None, :]
        flat = blk.transpose(0, 1, 3, 2, 4).reshape(B, n_cmp, N_KVH, CMP_BLOCK * HEAD_DIM)
        return jax.nn.silu(flat @ w1) @ w2

    k_cmp = rmsnorm(compress(kc, cmp_pe[0], cmp_w1[0], cmp_w2[0]), k_norm[0])
    v_cmp = compress(vc, cmp_pe[1], cmp_w1[1], cmp_w2[1])
    ks = rmsnorm(ks, k_norm[1])
    kw = rmsnorm(kw, k_norm[2])
    return (k_cmp, v_cmp, ks, vs, kw, vw)


def cmp_sel_overlap(n_cmp, n_sel):
    c0 = jnp.arange(n_cmp) * CMP_STRIDE
    s0 = jnp.arange(n_sel) * SEL_BLOCK
    ov = (c0[:, None] < s0[None, :] + SEL_BLOCK) & (c0[:, None] + CMP_BLOCK > s0[None, :])
    return ov.astype(jnp.float32)


def nsa_mixer(h, kv, w_in, q_norm, w_out):
    k_cmp, v_cmp, k_sel, v_sel, k_win, v_win = kv
    B, T, _ = h.shape
    G = N_QH // N_KVH
    n_cmp = k_cmp.shape[1]
    n_sel = T // SEL_BLOCK
    top = min(SEL_TOPK, n_sel)
    proj = h @ w_in
    q = proj[..., :N_QH * HEAD_DIM].reshape(B, T, N_KVH, G, HEAD_DIM)
    q = rmsnorm(q, q_norm) * (HEAD_DIM ** -0.5)
    gates = jax.nn.sigmoid(proj[..., N_QH * HEAD_DIM:].astype(jnp.float32)).reshape(B, T, 3, N_KVH, G)

    overlap = cmp_sel_overlap(n_cmp, n_sel)
    cmp_end = jnp.arange(n_cmp) * CMP_STRIDE + CMP_BLOCK - 1
    sel_id = jnp.arange(n_sel)
    ks_blk = k_sel.reshape(B, n_sel, SEL_BLOCK, N_KVH, HEAD_DIM).transpose(0, 3, 1, 2, 4)
    vs_blk = v_sel.reshape(B, n_sel, SEL_BLOCK, N_KVH, HEAD_DIM).transpose(0, 3, 1, 2, 4)
    kw_pad = jnp.pad(k_win, ((0, 0), (WINDOW, 0), (0, 0), (0, 0)))
    vw_pad = jnp.pad(v_win, ((0, 0), (WINDOW, 0), (0, 0), (0, 0)))
    b_ix = jnp.arange(B)[:, None, None, None]
    h_ix = jnp.arange(N_KVH)[None, None, :, None]
    dt = v_sel.dtype

    def block(qb):
        s0 = qb * Q_BLOCK
        t = s0 + jnp.arange(Q_BLOCK)
        qq = lax.dynamic_slice_in_dim(q, s0, Q_BLOCK, axis=1)
        gb = lax.dynamic_slice_in_dim(gates, s0, Q_BLOCK, axis=1).astype(dt)
        s_c = jnp.einsum('bqhgd,bchd->bqhgc', qq, k_cmp)
        p_c = masked_softmax(s_c, (cmp_end[None, :] <= t[:, None])[None, :, None, None, :])
        o_c = jnp.einsum('bqhgc,bchd->bqhgd', p_c.astype(dt), v_cmp)
        imp = jnp.einsum('bqhgc,cj->bqhj', p_c, overlap)
        valid = (sel_id[None, :] * SEL_BLOCK <= t[:, None])
        cur = (t // SEL_BLOCK)[:, None]
        forced = (sel_id[None, :] == 0) | (sel_id[None, :] == cur) | (sel_id[None, :] == cur - 1)
        score = jnp.where((forced & valid)[None, :, None, :], FORCE_SCORE,
                          jnp.where(valid[None, :, None, :], imp, -1.0))
        val, idx = lax.top_k(score, top)
        k_g = ks_blk[b_ix, h_ix, idx]
        v_g = vs_blk[b_ix, h_ix, idx]
        pos = idx[..., None] * SEL_BLOCK + jnp.arange(SEL_BLOCK)
        m_s = (pos <= t[None, :, None, None, None]) & (val >= 0)[..., None]
        s_s = jnp.einsum('bqhgd,bqhksd->bqhgks', qq, k_g).reshape(B, Q_BLOCK, N_KVH, G, top * SEL_BLOCK)
        p_s = masked_softmax(s_s, m_s.reshape(B, Q_BLOCK, N_KVH, 1, top * SEL_BLOCK))
        o_s = jnp.einsum('bqhgn,bqhnd->bqhgd', p_s.astype(dt),
                         v_g.reshape(B, Q_BLOCK, N_KVH, top * SEL_BLOCK, HEAD_DIM))
        k_w = lax.dynamic_slice_in_dim(kw_pad, s0, WINDOW + Q_BLOCK, axis=1)
        v_w = lax.dynamic_slice_in_dim(vw_pad, s0, WINDOW + Q_BLOCK, axis=1)
        kpos = s0 - WINDOW + jnp.arange(WINDOW + Q_BLOCK)
        m_w = (kpos[None, :] <= t[:, None]) & (kpos[None, :] > t[:, None] - WINDOW) & (kpos[None, :] >= 0)
        s_w = jnp.einsum('bqhgd,bkhd->bqhgk', qq, k_w)
        p_w = masked_softmax(s_w, m_w[None, :, None, None, :])
        o_w = jnp.einsum('bqhgk,bkhd->bqhgd', p_w.astype(dt), v_w)
        return (gb[:, :, 0, :, :, None] * o_c + gb[:, :, 1, :, :, None] * o_s
                + gb[:, :, 2, :, :, None] * o_w)

    o = lax.map(block, jnp.arange(T // Q_BLOCK))
    o = jnp.moveaxis(o, 0, 1).reshape(B, T, N_QH * HEAD_DIM)
    return o @ w_out


def setup_inputs(seed: int = 0) -> dict:
    key = jax.random.key(seed)
    k = jax.random.split(key, 20)
    f32 = jnp.float32

    def w(kk, shape, fan_in):
        return jax.random.normal(kk, shape, f32) * fan_in ** -0.5

    def gain(kk, shape):
        return 1.0 + 0.02 * jax.random.normal(kk, shape, f32)

    a_in = 2 * M_HEADS * M_DQK + M_HEADS * M_DV + 2 * M_HEADS + D_MODEL
    b_in = N_QH * HEAD_DIM + 3 * N_QH
    nb = jax.random.normal(k[6], (N_A_LAYERS, 2, M_HEADS), f32)
    return {
        "x": jax.random.normal(k[0], (BATCH, SEQ, D_MODEL), f32),
        "ffn_norm": gain(k[1], (DEPTH, 2, D_MODEL)),
        "ffn_w_in": w(k[2], (DEPTH, 2, D_MODEL, 2 * D_FF), D_MODEL),
        "ffn_w_out": w(k[3], (DEPTH, 2, D_FF, D_MODEL), D_FF),
        "mix_norm": gain(k[4], (DEPTH, D_MODEL)),
        "a_w_in": w(k[5], (N_A_LAYERS, D_MODEL, a_in), D_MODEL),
        "a_b_if": jnp.stack([-1.0 + 0.1 * nb[:, 0], 3.0 + 0.5 * nb[:, 1]], axis=1),
        "a_g_head": gain(k[7], (N_A_LAYERS, M_HEADS, M_DV)),
        "a_w_out": w(k[8], (N_A_LAYERS, M_HEADS * M_DV, D_MODEL), M_HEADS * M_DV),
        "kv_norm": gain(k[9], (D_MODEL,)),
        "kv_w": w(k[10], (D_MODEL, 6 * N_KVH * HEAD_DIM), D_MODEL),
        "cmp_pe": 0.02 * jax.random.normal(k[11], (2, CMP_BLOCK, HEAD_DIM), f32),
        "cmp_w1": w(k[12], (2, CMP_BLOCK * HEAD_DIM, CMP_HIDDEN), CMP_BLOCK * HEAD_DIM),
        "cmp_w2": w(k[13], (2, CMP_HIDDEN, HEAD_DIM), CMP_HIDDEN),
        "k_norm": gain(k[14], (3, HEAD_DIM)),
        "b_w_in": w(k[15], (N_B_LAYERS, D_MODEL, b_in), D_MODEL),
        "b_q_norm": gain(k[16], (N_B_LAYERS, HEAD_DIM)),
        "b_w_out": w(k[17], (N_B_LAYERS, N_QH * HEAD_DIM, D_MODEL), N_QH * HEAD_DIM),
    }


def reference(x, ffn_norm, ffn_w_in, ffn_w_out, mix_norm, a_w_in, a_b_if, a_g_head, a_w_out,
              kv_norm, kv_w, cmp_pe, cmp_w1, cmp_w2, k_norm, b_w_in, b_q_norm, b_w_out):
    h = x
    kv = None
    for layer in range(DEPTH):
        h = h + 0.5 * swiglu(rmsnorm(h, ffn_norm[layer, 0]), ffn_w_in[layer, 0], ffn_w_out[layer, 0])
        hn = rmsnorm(h, mix_norm[layer])
        if layer < N_A_LAYERS:
            h = h + mlstm_mixer(hn, a_w_in[layer], a_b_if[layer], a_g_head[layer], a_w_out[layer])
        else:
            j = layer - N_A_LAYERS
            h = h + nsa_mixer(hn, kv, b_w_in[j], b_q_norm[j], b_w_out[j])
        h = h + 0.5 * swiglu(rmsnorm(h, ffn_norm[layer, 1]), ffn_w_in[layer, 1], ffn_w_out[layer, 1])
        if layer == N_A_LAYERS - 1:
            kv = nsa_shared_kv(h, kv_norm, kv_w, cmp_pe, cmp_w1, cmp_w2, k_norm)
    return h
```

```python
import functools

import jax
import jax.numpy as jnp
import numpy as np
from jax import lax
from jax.experimental import pallas as pl
from jax.experimental.pallas import tpu as pltpu

F32 = jnp.float32
BF16 = jnp.bfloat16

D_MODEL = 1024
D_FF = 2816
EPS = 1e-6
NEG = -1e30

M_HEADS = 8
M_DV = 128
M_DQK = 64

N_QH = 16
N_KVH = 4
GROUP = N_QH // N_KVH
HEAD_DIM = 64
CMP_BLOCK = 32
CMP_STRIDE = 16
CMP_HIDDEN = 256
SEL_BLOCK = 64
SEL_TOPK = 16
WINDOW = 512
FORCE_SCORE = 1e4

LANES = 128
VMEM_LIMIT = 56 * 1024 * 1024

TOKEN_TILE = 512
FFN_CHUNK = 256
MLSTM_CHUNK = 256
Q_TILE = 128
SEL_KEY_CHUNK = 512

_NT = (((1,), (1,)), ((), ()))
_TN = (((0,), (0,)), ((), ()))


def _params(*sem):
    return pltpu.CompilerParams(dimension_semantics=sem, vmem_limit_bytes=VMEM_LIMIT)


def _rms(x, g):
    ms = jnp.mean(x * x, axis=-1, keepdims=True)
    return (x * lax.rsqrt(ms + EPS)) * g


def _dot(a, b):
    return jnp.dot(a, b, preferred_element_type=F32)


def _dot_nt(a, b, precision=None):
    return lax.dot_general(a, b, _NT, precision=precision, preferred_element_type=F32)


def _resident(shape):
    nd = len(shape)
    return pl.BlockSpec(shape, lambda *_: (0,) * nd)


def _ffn_kernel(x_ref, g_ref, win_ref, wout_ref, o_ref, xn_ref, acc_ref):
    x = x_ref[...]
    xn_ref[...] = _rms(x, g_ref[...]).astype(BF16)
    acc_ref[...] = jnp.zeros_like(acc_ref)

    def body(c, carry):
        xn = xn_ref[...]
        a = _dot(xn, win_ref[0, c])
        b = _dot(xn, win_ref[1, c])
        hid = ((a * jax.nn.sigmoid(a)) * b).astype(BF16)
        acc_ref[...] += _dot(hid, wout_ref[c])
        return carry

    lax.fori_loop(0, win_ref.shape[1], body, 0)
    o_ref[...] = x_ref[...] + 0.5 * acc_ref[...]


def ffn(x2, g, w_in, w_out):
    n, d = x2.shape
    f = w_out.shape[0]
    nf = f // FFN_CHUNK
    win = w_in.astype(BF16).reshape(d, 2, nf, FFN_CHUNK).transpose(1, 2, 0, 3)
    wout = w_out.astype(BF16).reshape(nf, FFN_CHUNK, d)
    tm = TOKEN_TILE
    return pl.pallas_call(
        _ffn_kernel,
        grid=(n // tm,),
        in_specs=[
            pl.BlockSpec((tm, d), lambda i: (i, 0)),
            _resident((1, d)),
            _resident(win.shape),
            _resident(wout.shape),
        ],
        out_specs=pl.BlockSpec((tm, d), lambda i: (i, 0)),
        out_shape=jax.ShapeDtypeStruct((n, d), F32),
        scratch_shapes=[pltpu.VMEM((tm, d), BF16), pltpu.VMEM((tm, d), F32)],
        compiler_params=_params("parallel"),
        name="ffn",
    )(x2, g.reshape(1, d), win, wout)


def _mlstm_proj_kernel(x_ref, g_ref, wqk_ref, wv_ref, wog_ref, wg_ref, bg_ref,
                       qk_ref, v_ref, og_ref, gate_ref):
    xn = _rms(x_ref[...], g_ref[...]).astype(BF16)
    qk_ref[...] = _dot(xn, wqk_ref[...]).astype(BF16)
    v_ref[...] = _dot(xn, wv_ref[...]).astype(BF16)
    og_ref[...] = _dot(xn, wog_ref[...])
    gate_ref[...] = _dot(xn, wg_ref[...]) + bg_ref[...]


def mlstm_proj(h2, g, w_in, b_if):
    n, d = h2.shape
    qk_w, v_w = M_HEADS * M_DQK, M_HEADS * M_DV
    wb = w_in.astype(BF16)
    wqk = wb[:, :2 * qk_w]
    wv = wb[:, 2 * qk_w:2 * qk_w + v_w]
    wgate = jnp.pad(wb[:, 2 * qk_w + v_w:2 * qk_w + v_w + 2 * M_HEADS], ((0, 0), (0, LANES - 2 * M_HEADS)))
    wog = wb[:, 2 * qk_w + v_w + 2 * M_HEADS:]
    bg = jnp.pad(b_if.astype(F32).reshape(1, 2 * M_HEADS), ((0, 0), (0, LANES - 2 * M_HEADS)))
    tm = TOKEN_TILE
    tile = lambda w: pl.BlockSpec((tm, w), lambda i: (i, 0))
    return pl.pallas_call(
        _mlstm_proj_kernel,
        grid=(n // tm,),
        in_specs=[tile(d), _resident((1, d)), _resident(wqk.shape), _resident(wv.shape),
                  _resident(wog.shape), _resident(wgate.shape), _resident(bg.shape)],
        out_specs=[tile(2 * qk_w), tile(v_w), tile(d), tile(LANES)],
        out_shape=[jax.ShapeDtypeStruct((n, 2 * qk_w), BF16), jax.ShapeDtypeStruct((n, v_w), BF16),
                   jax.ShapeDtypeStruct((n, d), F32), jax.ShapeDtypeStruct((n, LANES), F32)],
        compiler_params=_params("parallel"),
        name="mlstm_proj",
    )(h2, g.reshape(1, d), wqk, wv, wog, wgate, bg)


def _log_sigmoid(x):
    return jnp.minimum(x, 0.0) - jnp.log1p(jnp.exp(-jnp.abs(x)))


def _mlstm_chunk_kernel(q_ref, k_ref, v_ref, gate_ref, ghead_ref, o_ref, c_ref):
    pair = pl.program_id(1)
    L = MLSTM_CHUNK
    n_chunks = q_ref.shape[1] // L
    hi = lax.Precision.HIGHEST

    r_i = lax.broadcasted_iota(jnp.int32, (L, L), 0)
    c_i = lax.broadcasted_iota(jnp.int32, (L, L), 1)
    causal = c_i <= r_i
    tril = causal.astype(F32)
    eye = (c_i == r_i).astype(F32)
    lane = lax.broadcasted_iota(jnp.int32, (L, LANES), 1)
    ones_col = (lane == 0).astype(BF16)
    row8 = lax.broadcasted_iota(jnp.int32, (8, L), 0)

    c_ref[...] = jnp.zeros_like(c_ref)

    def body(c, m_prev):
        base = pl.multiple_of(c * L, L)
        q2 = q_ref[0, pl.ds(base, L), :]
        k2 = k_ref[0, pl.ds(base, L), :]
        m_next = []
        for hh in range(2):
            head = 2 * pair + hh
            in_head = (lane >= hh * M_DQK) & (lane < (hh + 1) * M_DQK)
            qh = jnp.where(in_head, q2, 0).astype(F32)
            qh = (qh * (M_DQK ** -0.5)).astype(BF16)
            kh = jnp.where(in_head, k2, 0)
            v_aug = jnp.concatenate([v_ref[0, pl.ds(base, L), hh * M_DV:(hh + 1) * M_DV], ones_col], axis=1)

            li_r = gate_ref[0, c, pl.ds(head, 1), :]
            lf_r = _log_sigmoid(gate_ref[0, c, pl.ds(M_HEADS + head, 1), :])
            rows = jnp.where(row8 == 0, li_r, jnp.where(row8 == 1, lf_r, 0.0))
            col_id = _dot_nt(eye, rows, hi)
            col_cs = _dot_nt(tril, rows, hi)
            li_c, b_c = col_id[:, 0:1], col_cs[:, 1:2]
            b_r = _dot_nt(rows, tril, hi)[1:2, :]
            g_tot = b_r[:, L - 1:L]

            a_c = g_tot - b_c + li_c
            m_loc = jnp.max(a_c, axis=0, keepdims=True)
            kw = (kh.astype(F32) * jnp.exp(a_c - m_loc)).astype(BF16)
            c_loc = lax.dot_general(kw, v_aug, _TN, preferred_element_type=F32)

            m_p = m_prev[hh]
            c_prev = c_ref[hh]
            dmat = jnp.where(causal, b_c - (b_r - li_r), -jnp.inf)
            inter = b_c + m_p
            m_t = jnp.maximum(inter, jnp.max(dmat, axis=1, keepdims=True))
            w = jnp.exp(dmat - m_t) * _dot_nt(qh, kh)
            num = jnp.exp(inter - m_t) * _dot(qh, c_prev.astype(BF16)) + _dot(w.astype(BF16), v_aug)
            den = num[:, M_DV:M_DV + 1]
            h_out = num[:, :M_DV] / jnp.maximum(jnp.abs(den), jnp.exp(-m_t))
            h_out = _rms(h_out, ghead_ref[pl.ds(head, 1), :])
            o_ref[0, pl.ds(base, L), hh * M_DV:(hh + 1) * M_DV] = h_out

            m_new = jnp.maximum(g_tot + m_p, m_loc)
            c_ref[hh] = jnp.exp(g_tot + m_p - m_new) * c_prev + jnp.exp(m_loc - m_new) * c_loc
            m_next.append(m_new)
        return tuple(m_next)

    zero = jnp.zeros((1, 1), F32)
    lax.fori_loop(0, n_chunks, body, (zero, zero))


def mlstm_chunk(qk, v, gates_t, g_head):
    b, t, _ = v.shape
    L = MLSTM_CHUNK
    pairs = M_HEADS // 2
    return pl.pallas_call(
        _mlstm_chunk_kernel,
        grid=(b, pairs),
        in_specs=[
            pl.BlockSpec((1, t, LANES), lambda i, p: (i, 0, p)),
            pl.BlockSpec((1, t, LANES), lambda i, p: (i, 0, pairs + p)),
            pl.BlockSpec((1, t, 2 * M_DV), lambda i, p: (i, 0, p)),
            pl.BlockSpec((1, t // L, 2 * M_HEADS, L), lambda i, p: (i, 0, 0, 0)),
            _resident((M_HEADS, M_DV)),
        ],
        out_specs=pl.BlockSpec((1, t, 2 * M_DV), lambda i, p: (i, 0, p)),
        out_shape=jax.ShapeDtypeStruct((b, t, M_HEADS * M_DV), F32),
        scratch_shapes=[pltpu.VMEM((2, 2 * M_DQK, 2 * M_DV), F32)],
        compiler_params=_params("parallel", "parallel"),
        name="mlstm_chunk",
    )(qk, qk, v, gates_t, g_head.astype(F32))


def _resid_matmul_kernel(h_ref, a_ref, w_ref, o_ref):
    o_ref[...] = h_ref[...] + _dot(a_ref[...].astype(BF16), w_ref[...])


def _resid_gated_matmul_kernel(h_ref, a_ref, og_ref, w_ref, o_ref):
    a = (a_ref[...] * jax.nn.sigmoid(og_ref[...])).astype(BF16)
    o_ref[...] = h_ref[...] + _dot(a, w_ref[...])


def resid_matmul(h2, a2, w, og2=None):
    n, d = h2.shape
    k = a2.shape[1]
    tm = TOKEN_TILE
    tile = lambda wd: pl.BlockSpec((tm, wd), lambda i: (i, 0))
    if og2 is None:
        kern, ins, specs = _resid_matmul_kernel, (h2, a2), [tile(d), tile(k)]
    else:
        kern, ins, specs = _resid_gated_matmul_kernel, (h2, a2, og2), [tile(d), tile(k), tile(k)]
    return pl.pallas_call(
        kern,
        grid=(n // tm,),
        in_specs=specs + [_resident((k, d))],
        out_specs=tile(d),
        out_shape=jax.ShapeDtypeStruct((n, d), F32),
        compiler_params=_params("parallel"),
        name="resid_matmul",
    )(*ins, w.astype(BF16))


def _pack_norm(y, gain):
    lane = lax.broadcasted_iota(jnp.int32, (1, LANES), 1)
    is_k = lane >= HEAD_DIM
    outs = []
    for h in range(N_KVH):
        yh = y[:, h * LANES:(h + 1) * LANES]
        ms = jnp.sum(jnp.where(is_k, yh * yh, 0.0), axis=-1, keepdims=True) * (1.0 / HEAD_DIM)
        normed = (yh * lax.rsqrt(ms + EPS)) * gain
        outs.append(jnp.where(is_k, normed, yh))
    return jnp.concatenate(outs, axis=1)


def _kv_proj_kernel(x_ref, g_ref, wc_ref, ws_ref, ww_ref, gs_ref, gw_ref, cmp_ref, sel_ref, win_ref):
    xn = _rms(x_ref[...], g_ref[...]).astype(BF16)
    cmp_ref[...] = _dot(xn, wc_ref[...])
    sel_ref[...] = _pack_norm(_dot(xn, ws_ref[...]), gs_ref[...]).astype(BF16)
    win_ref[...] = _pack_norm(_dot(xn, ww_ref[...]), gw_ref[...]).astype(BF16)


def _vk_pack_weights(wv, wk):
    d = wv.shape[0]
    wv = wv.reshape(d, N_KVH, HEAD_DIM)
    wk = wk.reshape(d, N_KVH, HEAD_DIM)
    return jnp.concatenate([wv, wk], axis=-1).reshape(d, N_KVH * LANES)


def _k_gain_row(gain):
    return jnp.concatenate([jnp.ones((HEAD_DIM,), F32), gain.astype(F32)]).reshape(1, LANES)


def kv_proj(h2, g, kv_w, k_norm):
    n, d = h2.shape
    wb = kv_w.astype(BF16)
    kc, vc, ks, vs, kw, vw = jnp.split(wb, 6, axis=1)
    wcmp = jnp.concatenate([kc, vc], axis=1)
    wsel = _vk_pack_weights(vs, ks)
    wwin = _vk_pack_weights(vw, kw)
    wd = N_KVH * LANES
    tm = TOKEN_TILE
    tile = lambda w: pl.BlockSpec((tm, w), lambda i: (i, 0))
    return pl.pallas_call(
        _kv_proj_kernel,
        grid=(n // tm,),
        in_specs=[tile(d), _resident((1, d)), _resident(wcmp.shape), _resident(wsel.shape),
                  _resident(wwin.shape), _resident((1, LANES)), _resident((1, LANES))],
        out_specs=[tile(wd), tile(wd), tile(wd)],
        out_shape=[jax.ShapeDtypeStruct((n, wd), F32), jax.ShapeDtypeStruct((n, wd), BF16),
                   jax.ShapeDtypeStruct((n, wd), BF16)],
        compiler_params=_params("parallel"),
        name="kv_proj",
    )(h2, g.reshape(1, d), wcmp, wsel, wwin, _k_gain_row(k_norm[1]), _k_gain_row(k_norm[2]))


def _cmp_kernel(xk_ref, xv_ref, pe_ref, w1_ref, w2k_ref, w2v_ref, gk_ref, o_ref):
    def compress(x, kv, w2):
        u = _dot((x + pe_ref[kv, 0]).astype(BF16), w1_ref[kv, 0])
        v = _dot((x + pe_ref[kv, 1]).astype(BF16), w1_ref[kv, 1])
        nrow = v.shape[0]
        hid = u + pltpu.roll(v, nrow - 1, 0)
        return _dot((hid * jax.nn.sigmoid(hid)).astype(BF16), w2)

    yk = compress(xk_ref[0, 0], 0, w2k_ref[...])
    yv = compress(xv_ref[0, 0], 1, w2v_ref[...])
    ms = jnp.sum(yk * yk, axis=-1, keepdims=True) * (1.0 / HEAD_DIM)
    o_ref[0, 0] = ((yk * lax.rsqrt(ms + EPS)) * gk_ref[...] + yv).astype(BF16)


def cmp_kv(cmp_flat, b, t, cmp_pe, cmp_w1, cmp_w2, k_gain):
    ng = t // CMP_STRIDE
    feat = CMP_STRIDE * HEAD_DIM
    x = cmp_flat.reshape(b, ng, CMP_STRIDE, 2, N_KVH, HEAD_DIM).transpose(3, 0, 4, 1, 2, 5)
    x = x.reshape(2, b, N_KVH, ng, feat)
    pe = cmp_pe.astype(F32).reshape(2, 2, 1, feat)
    w1 = cmp_w1.astype(BF16).reshape(2, 2, feat, CMP_HIDDEN)
    zeros = jnp.zeros((CMP_HIDDEN, HEAD_DIM), BF16)
    w2k = jnp.concatenate([zeros, cmp_w2[0].astype(BF16)], axis=1)
    w2v = jnp.concatenate([cmp_w2[1].astype(BF16), zeros], axis=1)
    xspec = pl.BlockSpec((1, 1, ng, feat), lambda i, h: (i, h, 0, 0))
    return pl.pallas_call(
        _cmp_kernel,
        grid=(b, N_KVH),
        in_specs=[xspec, xspec, _resident(pe.shape), _resident(w1.shape), _resident(w2k.shape),
                  _resident(w2v.shape), _resident((1, LANES))],
        out_specs=pl.BlockSpec((1, 1, ng, LANES), lambda i, h: (i, h, 0, 0)),
        out_shape=jax.ShapeDtypeStruct((b, N_KVH, ng, LANES), BF16),
        compiler_params=_params("parallel", "parallel"),
        name="cmp_kv",
    )(x[0], x[1], pe, w1, w2k, w2v, _k_gain_row(k_gain))


def _nsa_proj_kernel(x_ref, g_ref, wq_ref, wg_ref, qg_ref, q_ref, gate_ref):
    xn = _rms(x_ref[...], g_ref[...]).astype(BF16)
    q = _dot(xn, wq_ref[...])
    lane = lax.broadcasted_iota(jnp.int32, (1, LANES), 1)
    lo = lane < HEAD_DIM
    outs = []
    for cb in range(q.shape[1] // LANES):
        y = q[:, cb * LANES:(cb + 1) * LANES]
        y2 = y * y
        ms_lo = jnp.sum(jnp.where(lo, y2, 0.0), axis=-1, keepdims=True) * (1.0 / HEAD_DIM)
        ms_hi = jnp.sum(jnp.where(lo, 0.0, y2), axis=-1, keepdims=True) * (1.0 / HEAD_DIM)
        scale = jnp.where(lo, lax.rsqrt(ms_lo + EPS), lax.rsqrt(ms_hi + EPS))
        outs.append(((y * scale) * qg_ref[...]) * (HEAD_DIM ** -0.5))
    q_ref[...] = jnp.concatenate(outs, axis=1).astype(BF16)
    gate_ref[...] = jax.nn.sigmoid(_dot(xn, wg_ref[...]))


def nsa_proj(h2, g, w_in, q_norm):
    n, d = h2.shape
    wb = w_in.astype(BF16)
    wq = wb[:, :N_QH * HEAD_DIM]
    wg = wb[:, N_QH * HEAD_DIM:].reshape(d, 3, N_KVH, GROUP).transpose(0, 2, 1, 3).reshape(d, N_KVH, 3 * GROUP)
    wg = jnp.pad(wg, ((0, 0), (0, 0), (0, LANES - 3 * GROUP))).reshape(d, N_KVH * LANES)
    qg = jnp.tile(q_norm.astype(F32), LANES // HEAD_DIM).reshape(1, LANES)
    tm = TOKEN_TILE
    tile = lambda w: pl.BlockSpec((tm, w), lambda i: (i, 0))
    return pl.pallas_call(
        _nsa_proj_kernel,
        grid=(n // tm,),
        in_specs=[tile(d), _resident((1, d)), _resident(wq.shape), _resident(wg.shape), _resident((1, LANES))],
        out_specs=[tile(N_QH * HEAD_DIM), tile(N_KVH * LANES)],
        out_shape=[jax.ShapeDtypeStruct((n, N_QH * HEAD_DIM), BF16),
                   jax.ShapeDtypeStruct((n, N_KVH * LANES), F32)],
        compiler_params=_params("parallel"),
        name="nsa_proj",
    )(h2, g.reshape(1, d), wq, wg, qg)


def _softmax_rows(s3, mask):
    sm = jnp.where(mask[None], s3, NEG)
    e = jnp.where(mask[None], jnp.exp(sm - jnp.max(sm, axis=-1, keepdims=True)), 0.0)
    z = jnp.sum(e, axis=-1, keepdims=True)
    return e / jnp.where(z > 0, z, 1.0)


def _nsa_attn_kernel(q_ref, gate_ref, cmp_ref, sel_ref, win_ref, ov_ref, exp_ref, o_ref,
                     m_ref, l_ref, acc_ref):
    tq = q_ref.shape[1]
    tk = exp_ref.shape[2]
    rows = GROUP * tq
    s0 = pl.program_id(2) * tq

    lane = lax.broadcasted_iota(jnp.int32, (tq, LANES), 1)
    k_lanes = lane >= HEAD_DIM
    qs = []
    for g in range(GROUP):
        blk = q_ref[0, :, (g // 2) * LANES:(g // 2 + 1) * LANES].astype(F32)
        if g % 2 == 0:
            blk = pltpu.roll(blk, HEAD_DIM, 1)
        qs.append(jnp.where(k_lanes, blk, 0.0).astype(BF16))
    qst = jnp.concatenate(qs, axis=0)
    t_q = s0 + lax.broadcasted_iota(jnp.int32, (tq, 1), 0)

    kvc = cmp_ref[0, 0]
    n_cmp = kvc.shape[0]
    cmp_end = lax.broadcasted_iota(jnp.int32, (1, n_cmp), 1) * CMP_STRIDE + (CMP_BLOCK - 1)
    p_c = _softmax_rows(_dot_nt(qst, kvc).reshape(GROUP, tq, n_cmp), cmp_end <= t_q)
    o_c = _dot(p_c.reshape(rows, n_cmp).astype(BF16), kvc)

    p_sum = p_c[0] + p_c[1] + p_c[2] + p_c[3]
    p_hi = p_sum.astype(BF16)
    p_lo = (p_sum - p_hi.astype(F32)).astype(BF16)
    imp = _dot(p_hi, ov_ref[...]) + _dot(p_lo, ov_ref[...])
    n_sel = sel_ref.shape[1] // SEL_BLOCK
    valid = (lane * SEL_BLOCK <= t_q) & (lane < n_sel)
    cur = lax.shift_right_logical(t_q, 6)
    forced = (lane == 0) | (lane == cur) | (lane == cur - 1)
    score = jnp.where(forced & valid, FORCE_SCORE, jnp.where(valid, imp, -1.0))
    score = jnp.where(lane < n_sel, score, -2.0)
    s_t = score.T[:n_sel]
    blk_t = lax.broadcasted_iota(jnp.int32, (n_sel, tq), 0)
    rank = jnp.zeros((n_sel, tq), F32)
    for i in range(n_sel):
        s_i = s_t[i:i + 1, :]
        ahead = (s_i > s_t) | ((s_i == s_t) & (blk_t > i))
        rank = rank + jnp.where(ahead, 1.0, 0.0)
    t_row = s0 + lax.broadcasted_iota(jnp.int32, (1, tq), 1)
    chosen_t = jnp.where((rank < SEL_TOPK) & (blk_t * SEL_BLOCK <= t_row), 1.0, 0.0)
    chosen = jnp.concatenate([chosen_t, jnp.zeros((LANES - n_sel, tq), F32)], axis=0).T.astype(BF16)

    m_ref[...] = jnp.full(m_ref.shape, NEG, F32)
    l_ref[...] = jnp.zeros_like(l_ref)
    acc_ref[...] = jnp.zeros_like(acc_ref)

    def sel_body(c, carry):
        base = pl.multiple_of(c * tk, tk)
        kv = sel_ref[0, pl.ds(base, tk), :]
        s3 = _dot_nt(qst, kv).reshape(GROUP, tq, tk)
        kpos = base + lax.broadcasted_iota(jnp.int32, (1, tk), 1)
        mask = (_dot(chosen, exp_ref[c]) > 0.5) & (kpos <= t_q)
        s3 = jnp.where(mask[None], s3, NEG)
        m_old = m_ref[...]
        m_new = jnp.maximum(m_old, jnp.max(s3, axis=-1, keepdims=True))
        alpha = jnp.exp(m_old - m_new)
        p = jnp.exp(s3 - m_new)
        l_ref[...] = alpha * l_ref[...] + jnp.sum(p, axis=-1, keepdims=True)
        acc_ref[...] = alpha * acc_ref[...] + _dot(p.reshape(rows, tk).astype(BF16), kv).reshape(GROUP, tq, LANES)
        m_ref[...] = m_new
        return carry

    lax.fori_loop(0, s0 // tk + 1, sel_body, 0)
    o_s = (acc_ref[...] / l_ref[...]).reshape(rows, LANES)

    wk = WINDOW + tq
    w0 = pl.multiple_of(jnp.maximum(s0 - WINDOW, 0), tq)
    kvw = win_ref[0, pl.ds(w0, wk), :]
    kpos = w0 + lax.broadcasted_iota(jnp.int32, (1, wk), 1)
    p_w = _softmax_rows(_dot_nt(qst, kvw).reshape(GROUP, tq, wk), (kpos <= t_q) & (kpos > t_q - WINDOW))
    o_w = _dot(p_w.reshape(rows, wk).astype(BF16), kvw)

    gt = gate_ref[0]

    def gate(branch):
        return jnp.concatenate([gt[:, branch * GROUP + g:branch * GROUP + g + 1] for g in range(GROUP)], axis=0)

    o = (gate(0) * o_c + gate(1) * o_s + gate(2) * o_w).reshape(GROUP, tq, LANES)
    v_lanes = lane < HEAD_DIM
    out01 = jnp.where(v_lanes, o[0], pltpu.roll(o[1], HEAD_DIM, 1))
    out23 = jnp.where(v_lanes, o[2], pltpu.roll(o[3], HEAD_DIM, 1))
    o_ref[0] = jnp.concatenate([out01, out23], axis=1).astype(BF16)


def _overlap_matrix(n_cmp_rows, t):
    n_sel = t // SEL_BLOCK
    c0 = np.arange(n_cmp_rows) * CMP_STRIDE
    s_0 = np.arange(n_sel) * SEL_BLOCK
    ov = (c0[:, None] < s_0[None, :] + SEL_BLOCK) & (c0[:, None] + CMP_BLOCK > s_0[None, :])
    ov[(t - CMP_BLOCK) // CMP_STRIDE + 1:] = False
    out = np.zeros((n_cmp_rows, LANES), np.float32)
    out[:, :n_sel] = ov
    return jnp.asarray(out, BF16)


def _block_expand(t, tk):
    key = np.arange(t).reshape(t // tk, 1, tk)
    j = np.arange(LANES).reshape(1, LANES, 1)
    return jnp.asarray((key // SEL_BLOCK == j).astype(np.float32), BF16)


def nsa_attn(q, gates, cmp_pack, sel_pack, win_pack):
    b, t, _ = q.shape
    tq = Q_TILE
    tk = min(SEL_KEY_CHUNK, t)
    n_cmp_rows = cmp_pack.shape[2]
    assert t // SEL_BLOCK <= LANES and (t // SEL_BLOCK) % 8 == 0
    assert t % tk == 0 and tk % tq == 0 and t >= WINDOW + tq
    ov = _overlap_matrix(n_cmp_rows, t)
    expand = _block_expand(t, tk)
    gw = GROUP * HEAD_DIM
    kv_spec = pl.BlockSpec((1, t, LANES), lambda i, h, j: (i, 0, h))
    return pl.pallas_call(
        _nsa_attn_kernel,
        grid=(b, N_KVH, t // tq),
        in_specs=[
            pl.BlockSpec((1, tq, gw), lambda i, h, j: (i, j, h)),
            pl.BlockSpec((1, tq, LANES), lambda i, h, j: (i, j, h)),
            pl.BlockSpec((1, 1, n_cmp_rows, LANES), lambda i, h, j: (i, h, 0, 0)),
            kv_spec,
            kv_spec,
            _resident(ov.shape),
            _resident(expand.shape),
        ],
        out_specs=pl.BlockSpec((1, tq, gw), lambda i, h, j: (i, j, h)),
        out_shape=jax.ShapeDtypeStruct((b, t, N_QH * HEAD_DIM), BF16),
        scratch_shapes=[pltpu.VMEM((GROUP, tq, 1), F32), pltpu.VMEM((GROUP, tq, 1), F32),
                        pltpu.VMEM((GROUP, tq, LANES), F32)],
        compiler_params=_params("parallel", "parallel", "arbitrary"),
        name="nsa_attn",
    )(q, gates, cmp_pack, sel_pack, win_pack, ov, expand)


def mlstm_layer(h2, b, t, g, w_in, b_if, g_head, w_out):
    qk, v, og, gates = mlstm_proj(h2, g, w_in, b_if)
    L = MLSTM_CHUNK
    gates_t = gates[:, :2 * M_HEADS].reshape(b, t // L, L, 2 * M_HEADS).transpose(0, 1, 3, 2)
    hn = mlstm_chunk(qk.reshape(b, t, -1), v.reshape(b, t, -1), gates_t, g_head)
    return resid_matmul(h2, hn.reshape(b * t, -1), w_out, og)


def nsa_shared_kv(h2, b, t, kv_norm, kv_w, cmp_pe, cmp_w1, cmp_w2, k_norm):
    cmp_flat, sel_pack, win_pack = kv_proj(h2, kv_norm, kv_w, k_norm)
    cmp_pack = cmp_kv(cmp_flat, b, t, cmp_pe, cmp_w1, cmp_w2, k_norm[0])
    return cmp_pack, sel_pack.reshape(b, t, -1), win_pack.reshape(b, t, -1)


def nsa_layer(h2, b, t, kv, g, w_in, q_norm, w_out):
    q, gates = nsa_proj(h2, g, w_in, q_norm)
    o = nsa_attn(q.reshape(b, t, -1), gates.reshape(b, t, -1), *kv)
    return resid_matmul(h2, o.reshape(b * t, -1), w_out)


def kernel(x, ffn_norm, ffn_w_in, ffn_w_out, mix_norm, a_w_in, a_b_if, a_g_head, a_w_out, kv_norm, kv_w, cmp_pe, cmp_w1, cmp_w2, k_norm, b_w_in, b_q_norm, b_w_out):
    b, t, d = x.shape
    depth = ffn_norm.shape[0]
    n_a = a_w_in.shape[0]
    h = x.reshape(b * t, d)
    kv = None
    for layer in range(depth):
        h = ffn(h, ffn_norm[layer, 0], ffn_w_in[layer, 0], ffn_w_out[layer, 0])
        if layer < n_a:
            h = mlstm_layer(h, b, t, mix_norm[layer], a_w_in[layer], a_b_if[layer], a_g_head[layer], a_w_out[layer])
        else:
            j = layer - n_a
            h = nsa_layer(h, b, t, kv, mix_norm[layer], b_w_in[j], b_q_norm[j], b_w_out[j])
        h = ffn(h, ffn_norm[layer, 1], ffn_w_in[layer, 1], ffn_w_out[layer, 1])
        if layer == n_a - 1:
            kv = nsa_shared_kv(h, b, t, kv_norm, kv_w, cmp_pe, cmp_w1, cmp_w2, k_norm)
    return h.reshape(b, t, d)
```

```python
import functools

import jax
import jax.numpy as jnp
import numpy as np
from jax import lax
from jax.experimental import pallas as pl
from jax.experimental.pallas import tpu as pltpu

F32 = jnp.float32
BF16 = jnp.bfloat16

D_MODEL = 1024
D_FF = 2816
EPS = 1e-6
NEG = -1e30

M_HEADS = 8
M_DV = 128
M_DQK = 64

N_QH = 16
N_KVH = 4
GROUP = N_QH // N_KVH
HEAD_DIM = 64
CMP_BLOCK = 32
CMP_STRIDE = 16
CMP_HIDDEN = 256
SEL_BLOCK = 64
SEL_TOPK = 16
WINDOW = 512
FORCE_SCORE = 1e4
LOG2E = 1.4426950408889634
SEL_BIAS = -2.0 ** 100

LANES = 128
VMEM_LIMIT = 56 * 1024 * 1024

TOKEN_TILE = 512
FFN_CHUNK = 256
MLSTM_CHUNK = 256
Q_TILE = 128
SEL_KEY_CHUNK = 512

_NT = (((1,), (1,)), ((), ()))
_TN = (((0,), (0,)), ((), ()))


def _params(*sem):
    return pltpu.CompilerParams(dimension_semantics=sem, vmem_limit_bytes=VMEM_LIMIT)


def _rms(x, g):
    ms = jnp.mean(x * x, axis=-1, keepdims=True)
    return (x * lax.rsqrt(ms + EPS)) * g


def _dot(a, b):
    return jnp.dot(a, b, preferred_element_type=F32)


def _dot_nt(a, b, precision=None):
    return lax.dot_general(a, b, _NT, precision=precision, preferred_element_type=F32)


def _resident(shape):
    nd = len(shape)
    return pl.BlockSpec(shape, lambda *_: (0,) * nd)


def _ffn_kernel(x_ref, g_ref, win_ref, wout_ref, o_ref, xn_ref, acc_ref):
    x = x_ref[...]
    xn_ref[...] = _rms(x, g_ref[...]).astype(BF16)
    acc_ref[...] = jnp.zeros_like(acc_ref)

    def body(c, carry):
        xn = xn_ref[...]
        a = _dot(xn, win_ref[0, c])
        b = _dot(xn, win_ref[1, c])
        hid = ((a * jax.nn.sigmoid(a)) * b).astype(BF16)
        acc_ref[...] += _dot(hid, wout_ref[c])
        return carry

    lax.fori_loop(0, win_ref.shape[1], body, 0)
    o_ref[...] = x_ref[...] + 0.5 * acc_ref[...]


def ffn(x2, g, w_in, w_out):
    n, d = x2.shape
    f = w_out.shape[0]
    nf = f // FFN_CHUNK
    win = w_in.astype(BF16).reshape(d, 2, nf, FFN_CHUNK).transpose(1, 2, 0, 3)
    wout = w_out.astype(BF16).reshape(nf, FFN_CHUNK, d)
    tm = TOKEN_TILE
    return pl.pallas_call(
        _ffn_kernel,
        grid=(n // tm,),
        in_specs=[
            pl.BlockSpec((tm, d), lambda i: (i, 0)),
            _resident((1, d)),
            _resident(win.shape),
            _resident(wout.shape),
        ],
        out_specs=pl.BlockSpec((tm, d), lambda i: (i, 0)),
        out_shape=jax.ShapeDtypeStruct((n, d), F32),
        scratch_shapes=[pltpu.VMEM((tm, d), BF16), pltpu.VMEM((tm, d), F32)],
        compiler_params=_params("parallel"),
        name="ffn",
    )(x2, g.reshape(1, d), win, wout)


def _mlstm_proj_kernel(x_ref, g_ref, wqk_ref, wv_ref, wog_ref, wg_ref, bg_ref,
                       qk_ref, v_ref, og_ref, gate_ref):
    xn = _rms(x_ref[...], g_ref[...]).astype(BF16)
    qk_ref[...] = _dot(xn, wqk_ref[...]).astype(BF16)
    v_ref[...] = _dot(xn, wv_ref[...]).astype(BF16)
    og_ref[...] = _dot(xn, wog_ref[...])
    gate_ref[...] = _dot(xn, wg_ref[...]) + bg_ref[...]


def mlstm_proj(h2, g, w_in, b_if):
    n, d = h2.shape
    qk_w, v_w = M_HEADS * M_DQK, M_HEADS * M_DV
    wb = w_in.astype(BF16)
    wqk = wb[:, :2 * qk_w]
    wv = wb[:, 2 * qk_w:2 * qk_w + v_w]
    wgate = jnp.pad(wb[:, 2 * qk_w + v_w:2 * qk_w + v_w + 2 * M_HEADS], ((0, 0), (0, LANES - 2 * M_HEADS)))
    wog = wb[:, 2 * qk_w + v_w + 2 * M_HEADS:]
    bg = jnp.pad(b_if.astype(F32).reshape(1, 2 * M_HEADS), ((0, 0), (0, LANES - 2 * M_HEADS)))
    tm = TOKEN_TILE
    tile = lambda w: pl.BlockSpec((tm, w), lambda i: (i, 0))
    return pl.pallas_call(
        _mlstm_proj_kernel,
        grid=(n // tm,),
        in_specs=[tile(d), _resident((1, d)), _resident(wqk.shape), _resident(wv.shape),
                  _resident(wog.shape), _resident(wgate.shape), _resident(bg.shape)],
        out_specs=[tile(2 * qk_w), tile(v_w), tile(d), tile(LANES)],
        out_shape=[jax.ShapeDtypeStruct((n, 2 * qk_w), BF16), jax.ShapeDtypeStruct((n, v_w), BF16),
                   jax.ShapeDtypeStruct((n, d), F32), jax.ShapeDtypeStruct((n, LANES), F32)],
        compiler_params=_params("parallel"),
        name="mlstm_proj",
    )(h2, g.reshape(1, d), wqk, wv, wog, wgate, bg)


def _log_sigmoid(x):
    return jnp.minimum(x, 0.0) - jnp.log1p(jnp.exp(-jnp.abs(x)))


def _mlstm_chunk_kernel(q_ref, k_ref, v_ref, gate_ref, ghead_ref, o_ref, c_ref):
    pair = pl.program_id(1)
    L = MLSTM_CHUNK
    n_chunks = q_ref.shape[1] // L
    hi = lax.Precision.HIGHEST

    r_i = lax.broadcasted_iota(jnp.int32, (L, L), 0)
    c_i = lax.broadcasted_iota(jnp.int32, (L, L), 1)
    causal = c_i <= r_i
    tril = causal.astype(F32)
    eye = (c_i == r_i).astype(F32)
    lane = lax.broadcasted_iota(jnp.int32, (L, LANES), 1)
    ones_col = (lane == 0).astype(BF16)
    row8 = lax.broadcasted_iota(jnp.int32, (8, L), 0)

    c_ref[...] = jnp.zeros_like(c_ref)

    def body(c, m_prev):
        base = pl.multiple_of(c * L, L)
        q2 = q_ref[0, pl.ds(base, L), :]
        k2 = k_ref[0, pl.ds(base, L), :]
        m_next = []
        for hh in range(2):
            head = 2 * pair + hh
            in_head = (lane >= hh * M_DQK) & (lane < (hh + 1) * M_DQK)
            qh = jnp.where(in_head, q2, 0).astype(F32)
            qh = (qh * (M_DQK ** -0.5)).astype(BF16)
            kh = jnp.where(in_head, k2, 0)
            v_aug = jnp.concatenate([v_ref[0, pl.ds(base, L), hh * M_DV:(hh + 1) * M_DV], ones_col], axis=1)

            li_r = gate_ref[0, c, pl.ds(head, 1), :]
            lf_r = _log_sigmoid(gate_ref[0, c, pl.ds(M_HEADS + head, 1), :])
            rows = jnp.where(row8 == 0, li_r, jnp.where(row8 == 1, lf_r, 0.0))
            col_id = _dot_nt(eye, rows, hi)
            col_cs = _dot_nt(tril, rows, hi)
            li_c, b_c = col_id[:, 0:1], col_cs[:, 1:2]
            b_r = _dot_nt(rows, tril, hi)[1:2, :]
            g_tot = b_r[:, L - 1:L]

            a_c = g_tot - b_c + li_c
            m_loc = jnp.max(a_c, axis=0, keepdims=True)
            kw = (kh.astype(F32) * jnp.exp(a_c - m_loc)).astype(BF16)
            c_loc = lax.dot_general(kw, v_aug, _TN, preferred_element_type=F32)

            m_p = m_prev[hh]
            c_prev = c_ref[hh]
            dmat = jnp.where(causal, b_c - (b_r - li_r), -jnp.inf)
            inter = b_c + m_p
            m_t = jnp.maximum(inter, jnp.max(dmat, axis=1, keepdims=True))
            w = jnp.exp(dmat - m_t) * _dot_nt(qh, kh)
            num = jnp.exp(inter - m_t) * _dot(qh, c_prev.astype(BF16)) + _dot(w.astype(BF16), v_aug)
            den = num[:, M_DV:M_DV + 1]
            h_out = num[:, :M_DV] / jnp.maximum(jnp.abs(den), jnp.exp(-m_t))
            h_out = _rms(h_out, ghead_ref[pl.ds(head, 1), :])
            o_ref[0, pl.ds(base, L), hh * M_DV:(hh + 1) * M_DV] = h_out

            m_new = jnp.maximum(g_tot + m_p, m_loc)
            c_ref[hh] = jnp.exp(g_tot + m_p - m_new) * c_prev + jnp.exp(m_loc - m_new) * c_loc
            m_next.append(m_new)
        return tuple(m_next)

    zero = jnp.zeros((1, 1), F32)
    lax.fori_loop(0, n_chunks, body, (zero, zero))


def mlstm_chunk(qk, v, gates_t, g_head):
    b, t, _ = v.shape
    L = MLSTM_CHUNK
    pairs = M_HEADS // 2
    return pl.pallas_call(
        _mlstm_chunk_kernel,
        grid=(b, pairs),
        in_specs=[
            pl.BlockSpec((1, t, LANES), lambda i, p: (i, 0, p)),
            pl.BlockSpec((1, t, LANES), lambda i, p: (i, 0, pairs + p)),
            pl.BlockSpec((1, t, 2 * M_DV), lambda i, p: (i, 0, p)),
            pl.BlockSpec((1, t // L, 2 * M_HEADS, L), lambda i, p: (i, 0, 0, 0)),
            _resident((M_HEADS, M_DV)),
        ],
        out_specs=pl.BlockSpec((1, t, 2 * M_DV), lambda i, p: (i, 0, p)),
        out_shape=jax.ShapeDtypeStruct((b, t, M_HEADS * M_DV), F32),
        scratch_shapes=[pltpu.VMEM((2, 2 * M_DQK, 2 * M_DV), F32)],
        compiler_params=_params("parallel", "parallel"),
        name="mlstm_chunk",
    )(qk, qk, v, gates_t, g_head.astype(F32))


def _resid_matmul_kernel(h_ref, a_ref, w_ref, o_ref):
    o_ref[...] = h_ref[...] + _dot(a_ref[...].astype(BF16), w_ref[...])


def _resid_gated_matmul_kernel(h_ref, a_ref, og_ref, w_ref, o_ref):
    a = (a_ref[...] * jax.nn.sigmoid(og_ref[...])).astype(BF16)
    o_ref[...] = h_ref[...] + _dot(a, w_ref[...])


def resid_matmul(h2, a2, w, og2=None):
    n, d = h2.shape
    k = a2.shape[1]
    tm = TOKEN_TILE
    tile = lambda wd: pl.BlockSpec((tm, wd), lambda i: (i, 0))
    if og2 is None:
        kern, ins, specs = _resid_matmul_kernel, (h2, a2), [tile(d), tile(k)]
    else:
        kern, ins, specs = _resid_gated_matmul_kernel, (h2, a2, og2), [tile(d), tile(k), tile(k)]
    return pl.pallas_call(
        kern,
        grid=(n // tm,),
        in_specs=specs + [_resident((k, d))],
        out_specs=tile(d),
        out_shape=jax.ShapeDtypeStruct((n, d), F32),
        compiler_params=_params("parallel"),
        name="resid_matmul",
    )(*ins, w.astype(BF16))


def _kv_packs(y, gain, extra):
    lane = lax.broadcasted_iota(jnp.int32, (1, LANES), 1)
    lo = lane < HEAD_DIM
    kp, vp = [], []
    for h in range(N_KVH):
        yh = y[:, h * LANES:(h + 1) * LANES]
        ms = jnp.sum(jnp.where(lo, yh * yh, 0.0), axis=-1, keepdims=True) * (1.0 / HEAD_DIM)
        kp.append(jnp.where(lo, (yh * lax.rsqrt(ms + EPS)) * gain, extra))
        vp.append(jnp.where(lo, pltpu.roll(yh, HEAD_DIM, 1), 1.0))
    return jnp.concatenate(kp, axis=1), jnp.concatenate(vp, axis=1)


def _kv_proj_kernel(x_ref, g_ref, wc_ref, ws_ref, ww_ref, gs_ref, gw_ref,
                    cmp_ref, selk_ref, selv_ref, wink_ref, winv_ref, *, seq_len):
    tm = x_ref.shape[0]
    xn = _rms(x_ref[...], g_ref[...]).astype(BF16)
    cmp_ref[...] = _dot(xn, wc_ref[...])
    pos = (pl.program_id(0) * tm) % seq_len + lax.broadcasted_iota(jnp.int32, (tm, 1), 0)
    lane = lax.broadcasted_iota(jnp.int32, (1, LANES), 1)
    block_bias = jnp.where(lane == HEAD_DIM + lax.shift_right_logical(pos, 6), SEL_BIAS, 0.0)
    kp, vp = _kv_packs(_dot(xn, ws_ref[...]), gs_ref[...], block_bias)
    selk_ref[...] = kp.astype(BF16)
    selv_ref[...] = vp.astype(BF16)
    kp, vp = _kv_packs(_dot(xn, ww_ref[...]), gw_ref[...], 0.0)
    wink_ref[...] = kp.astype(BF16)
    winv_ref[...] = vp.astype(BF16)


def _kv_pack_weights(wk, wv):
    d = wk.shape[0]
    wk = wk.reshape(d, N_KVH, HEAD_DIM)
    wv = wv.reshape(d, N_KVH, HEAD_DIM)
    return jnp.concatenate([wk, wv], axis=-1).reshape(d, N_KVH * LANES)


def _k_gain_row(gain):
    return jnp.concatenate([gain.astype(F32), jnp.ones((HEAD_DIM,), F32)]).reshape(1, LANES)


def kv_proj(h2, t, g, kv_w, k_norm):
    n, d = h2.shape
    wb = kv_w.astype(BF16)
    kc, vc, ks, vs, kw, vw = jnp.split(wb, 6, axis=1)
    wcmp = jnp.concatenate([kc, vc], axis=1)
    wsel = _kv_pack_weights(ks, vs)
    wwin = _kv_pack_weights(kw, vw)
    wd = N_KVH * LANES
    tm = TOKEN_TILE
    assert t % tm == 0 and t // SEL_BLOCK <= LANES - HEAD_DIM
    tile = lambda w: pl.BlockSpec((tm, w), lambda i: (i, 0))
    packed = jax.ShapeDtypeStruct((n, wd), BF16)
    return pl.pallas_call(
        functools.partial(_kv_proj_kernel, seq_len=t),
        grid=(n // tm,),
        in_specs=[tile(d), _resident((1, d)), _resident(wcmp.shape), _resident(wsel.shape),
                  _resident(wwin.shape), _resident((1, LANES)), _resident((1, LANES))],
        out_specs=[tile(wd)] * 5,
        out_shape=[jax.ShapeDtypeStruct((n, wd), F32), packed, packed, packed, packed],
        compiler_params=_params("parallel"),
        name="kv_proj",
    )(h2, g.reshape(1, d), wcmp, wsel, wwin, _k_gain_row(k_norm[1]), _k_gain_row(k_norm[2]))


def _cmp_kernel(xk_ref, xv_ref, pe_ref, w1_ref, w2k_ref, w2v_ref, gk_ref, ok_ref, ov_ref):
    def compress(x, kv, w2):
        u = _dot((x + pe_ref[kv, 0]).astype(BF16), w1_ref[kv, 0])
        v = _dot((x + pe_ref[kv, 1]).astype(BF16), w1_ref[kv, 1])
        nrow = v.shape[0]
        hid = u + pltpu.roll(v, nrow - 1, 0)
        return _dot((hid * jax.nn.sigmoid(hid)).astype(BF16), w2)

    yk = compress(xk_ref[0, 0], 0, w2k_ref[...])
    yv = compress(xv_ref[0, 0], 1, w2v_ref[...])
    ms = jnp.sum(yk * yk, axis=-1, keepdims=True) * (1.0 / HEAD_DIM)
    ok_ref[0, 0] = ((yk * lax.rsqrt(ms + EPS)) * gk_ref[...]).astype(BF16)
    lane = lax.broadcasted_iota(jnp.int32, (1, LANES), 1)
    ov_ref[0, 0] = jnp.where(lane < HEAD_DIM, yv, 1.0).astype(BF16)


def cmp_kv(cmp_flat, b, t, cmp_pe, cmp_w1, cmp_w2, k_gain):
    ng = t // CMP_STRIDE
    feat = CMP_STRIDE * HEAD_DIM
    x = cmp_flat.reshape(b, ng, CMP_STRIDE, 2, N_KVH, HEAD_DIM).transpose(3, 0, 4, 1, 2, 5)
    x = x.reshape(2, b, N_KVH, ng, feat)
    pe = cmp_pe.astype(F32).reshape(2, 2, 1, feat)
    w1 = cmp_w1.astype(BF16).reshape(2, 2, feat, CMP_HIDDEN)
    zeros = jnp.zeros((CMP_HIDDEN, HEAD_DIM), BF16)
    w2k = jnp.concatenate([cmp_w2[0].astype(BF16), zeros], axis=1)
    w2v = jnp.concatenate([cmp_w2[1].astype(BF16), zeros], axis=1)
    xspec = pl.BlockSpec((1, 1, ng, feat), lambda i, h: (i, h, 0, 0))
    ospec = pl.BlockSpec((1, 1, ng, LANES), lambda i, h: (i, h, 0, 0))
    packed = jax.ShapeDtypeStruct((b, N_KVH, ng, LANES), BF16)
    return pl.pallas_call(
        _cmp_kernel,
        grid=(b, N_KVH),
        in_specs=[xspec, xspec, _resident(pe.shape), _resident(w1.shape), _resident(w2k.shape),
                  _resident(w2v.shape), _resident((1, LANES))],
        out_specs=[ospec, ospec],
        out_shape=[packed, packed],
        compiler_params=_params("parallel", "parallel"),
        name="cmp_kv",
    )(x[0], x[1], pe, w1, w2k, w2v, _k_gain_row(k_gain))


def _nsa_proj_kernel(x_ref, g_ref, wq_ref, wg_ref, qg_ref, q_ref, gate_ref):
    xn = _rms(x_ref[...], g_ref[...]).astype(BF16)
    q = _dot(xn, wq_ref[...])
    lane = lax.broadcasted_iota(jnp.int32, (1, LANES), 1)
    lo = lane < HEAD_DIM
    outs = []
    for cb in range(q.shape[1] // LANES):
        y = q[:, cb * LANES:(cb + 1) * LANES]
        y2 = y * y
        ms_lo = jnp.sum(jnp.where(lo, y2, 0.0), axis=-1, keepdims=True) * (1.0 / HEAD_DIM)
        ms_hi = jnp.sum(jnp.where(lo, 0.0, y2), axis=-1, keepdims=True) * (1.0 / HEAD_DIM)
        scale = jnp.where(lo, lax.rsqrt(ms_lo + EPS), lax.rsqrt(ms_hi + EPS))
        outs.append(((y * scale) * qg_ref[...]) * (HEAD_DIM ** -0.5 * LOG2E))
    q_ref[...] = jnp.concatenate(outs, axis=1).astype(BF16)
    gate_ref[...] = jax.nn.sigmoid(_dot(xn, wg_ref[...]))


def nsa_proj(h2, g, w_in, q_norm):
    n, d = h2.shape
    wb = w_in.astype(BF16)
    wq = wb[:, :N_QH * HEAD_DIM]
    wg = wb[:, N_QH * HEAD_DIM:].reshape(d, 3, N_KVH, GROUP).transpose(0, 2, 1, 3).reshape(d, N_KVH, 3 * GROUP)
    wg = jnp.pad(wg, ((0, 0), (0, 0), (0, LANES - 3 * GROUP))).reshape(d, N_KVH * LANES)
    qg = jnp.tile(q_norm.astype(F32), LANES // HEAD_DIM).reshape(1, LANES)
    tm = TOKEN_TILE
    tile = lambda w: pl.BlockSpec((tm, w), lambda i: (i, 0))
    return pl.pallas_call(
        _nsa_proj_kernel,
        grid=(n // tm,),
        in_specs=[tile(d), _resident((1, d)), _resident(wq.shape), _resident(wg.shape), _resident((1, LANES))],
        out_specs=[tile(N_QH * HEAD_DIM), tile(N_KVH * LANES)],
        out_shape=[jax.ShapeDtypeStruct((n, N_QH * HEAD_DIM), BF16),
                   jax.ShapeDtypeStruct((n, N_KVH * LANES), F32)],
        compiler_params=_params("parallel"),
        name="nsa_proj",
    )(h2, g.reshape(1, d), wq, wg, qg)


def _masked_exp2(s3, mask):
    sm = jnp.where(mask[None], s3, NEG)
    mx = jnp.max(sm, axis=-1, keepdims=True)
    return jnp.exp2(sm - mx), mx > 0.5 * NEG


def _normalise(acc, lo):
    return acc / jnp.where(lo, pltpu.roll(acc, HEAD_DIM, 1), 1.0)


def _nsa_attn_kernel(q_ref, gate_ref, cmpk_ref, cmpv_ref, selk_ref, selv_ref, wink_ref, winv_ref,
                     ov_ref, o_ref, s_ref, mx_ref, acc_ref, *, tk):
    tq = q_ref.shape[1]
    rows = GROUP * tq
    s0 = pl.program_id(2) * tq

    lane = lax.broadcasted_iota(jnp.int32, (tq, LANES), 1)
    lo = lane < HEAD_DIM
    lo_r = lax.broadcasted_iota(jnp.int32, (rows, LANES), 1) < HEAD_DIM
    qs = []
    for g in range(GROUP):
        blk = q_ref[0, :, (g // 2) * LANES:(g // 2 + 1) * LANES].astype(F32)
        if g % 2 == 1:
            blk = pltpu.roll(blk, HEAD_DIM, 1)
        qs.append(jnp.where(lo, blk, 0.0))
    q4 = jnp.concatenate(qs, axis=0)
    qst = q4.astype(BF16)
    t_q = s0 + lax.broadcasted_iota(jnp.int32, (tq, 1), 0)

    kc = cmpk_ref[0, 0]
    n_cmp = kc.shape[0]
    cmp_end = lax.broadcasted_iota(jnp.int32, (1, n_cmp), 1) * CMP_STRIDE + (CMP_BLOCK - 1)
    e_c, ok_c = _masked_exp2(_dot_nt(qst, kc).reshape(GROUP, tq, n_cmp), cmp_end <= t_q)
    e_c = e_c.reshape(rows, n_cmp)
    ok_c = ok_c.reshape(rows, 1)
    e_hi = e_c.astype(BF16)
    e_lo = (e_c - e_hi.astype(F32)).astype(BF16)
    o_c = jnp.where(ok_c, _normalise(_dot(e_hi, cmpv_ref[0, 0]), lo_r), 0.0)

    imp4 = jnp.where(ok_c, _normalise(_dot(e_hi, ov_ref[...]) + _dot(e_lo, ov_ref[...]), lo_r), 0.0)
    imp4 = imp4.reshape(GROUP, tq, LANES)
    imp = imp4[0] + imp4[1] + imp4[2] + imp4[3]

    n_sel = selk_ref.shape[1] // SEL_BLOCK
    valid = (lane * SEL_BLOCK <= t_q) & (lane < n_sel)
    cur = lax.shift_right_logical(t_q, 6)
    forced = (lane == 0) | (lane == cur) | (lane == cur - 1)
    score = jnp.where(forced & valid, FORCE_SCORE, jnp.where(valid, imp, -1.0))
    score = jnp.where(lane < n_sel, score, -2.0)
    s_t = score.T[:HEAD_DIM]
    blk_t = lax.broadcasted_iota(jnp.int32, (HEAD_DIM, tq), 0)
    rank = jnp.zeros((HEAD_DIM, tq), F32)
    for i in range(n_sel):
        s_i = s_t[i:i + 1, :]
        ahead = (s_i > s_t) | ((s_i == s_t) & (blk_t > i))
        rank = rank + jnp.where(ahead, 1.0, 0.0)
    t_row = s0 + lax.broadcasted_iota(jnp.int32, (1, tq), 1)
    dropped_t = jnp.where((rank < SEL_TOPK) & (blk_t * SEL_BLOCK <= t_row), 0.0, 1.0)
    dropped = jnp.concatenate([jnp.zeros((HEAD_DIM, tq), F32), dropped_t], axis=0).T
    qx = (q4 + jnp.concatenate([dropped] * GROUP, axis=0)).astype(BF16)

    wk = WINDOW + tq
    w0 = pl.multiple_of(jnp.maximum(s0 - WINDOW, 0), tq)
    kpos_w = w0 + lax.broadcasted_iota(jnp.int32, (1, wk), 1)
    e_w, _ = _masked_exp2(_dot_nt(qst, wink_ref[0, pl.ds(w0, wk), :]).reshape(GROUP, tq, wk),
                          (kpos_w <= t_q) & (kpos_w > t_q - WINDOW))
    o_w = _normalise(_dot(e_w.reshape(rows, wk).astype(BF16), winv_ref[0, pl.ds(w0, wk), :]), lo_r)

    nblk = tk // LANES
    last = s0 // tk
    mx_ref[...] = jnp.full(mx_ref.shape, NEG, F32)

    def scores(c, causal):
        base = pl.multiple_of(c * tk, tk)
        s = _dot_nt(qx, selk_ref[0, pl.ds(base, tk), :])
        if causal:
            kpos = base + lax.broadcasted_iota(jnp.int32, (1, tk), 1)
            s = jnp.where((kpos <= t_q)[None], s.reshape(GROUP, tq, tk), NEG).reshape(rows, tk)
        s_ref[c] = s
        m = mx_ref[...]
        for j in range(nblk):
            m = jnp.maximum(m, s[:, j * LANES:(j + 1) * LANES])
        mx_ref[...] = m

    def pass1(c, carry):
        scores(c, causal=False)
        return carry

    lax.fori_loop(0, last, pass1, 0)
    scores(last, causal=True)
    mx_ref[...] = jnp.broadcast_to(jnp.max(mx_ref[...], axis=-1, keepdims=True), mx_ref.shape)

    acc_ref[...] = jnp.zeros_like(acc_ref)

    def pass2(c, carry):
        base = pl.multiple_of(c * tk, tk)
        s = s_ref[c]
        m = mx_ref[...]
        p = jnp.concatenate([jnp.exp2(s[:, j * LANES:(j + 1) * LANES] - m) for j in range(nblk)], axis=1)
        acc_ref[...] += _dot(p.astype(BF16), selv_ref[0, pl.ds(base, tk), :])
        return carry

    lax.fori_loop(0, last + 1, pass2, 0)
    o_s = _normalise(acc_ref[...], lo_r)

    gt = gate_ref[0]

    def gate(branch):
        return jnp.concatenate([gt[:, branch * GROUP + g:branch * GROUP + g + 1] for g in range(GROUP)], axis=0)

    o = (gate(0) * o_c + gate(1) * o_s + gate(2) * o_w).reshape(GROUP, tq, LANES)
    out01 = jnp.where(lo, o[0], pltpu.roll(o[1], HEAD_DIM, 1))
    out23 = jnp.where(lo, o[2], pltpu.roll(o[3], HEAD_DIM, 1))
    o_ref[0] = jnp.concatenate([out01, out23], axis=1).astype(BF16)


def _overlap_matrix(n_cmp_rows, t):
    n_sel = t // SEL_BLOCK
    c0 = np.arange(n_cmp_rows) * CMP_STRIDE
    s_0 = np.arange(n_sel) * SEL_BLOCK
    ov = (c0[:, None] < s_0[None, :] + SEL_BLOCK) & (c0[:, None] + CMP_BLOCK > s_0[None, :])
    ov[(t - CMP_BLOCK) // CMP_STRIDE + 1:] = False
    out = np.zeros((n_cmp_rows, LANES), np.float32)
    out[:, :n_sel] = ov
    out[:, HEAD_DIM:] = 1.0
    return jnp.asarray(out, BF16)


def nsa_attn(q, gates, kv):
    cmpk, cmpv, selk, selv, wink, winv = kv
    b, t, _ = q.shape
    tq = Q_TILE
    tk = min(SEL_KEY_CHUNK, t)
    n_cmp_rows = cmpk.shape[2]
    assert t // SEL_BLOCK <= LANES - HEAD_DIM and (t // SEL_BLOCK) % 8 == 0
    assert t % tk == 0 and tk % tq == 0 and t >= WINDOW + tq
    ov = _overlap_matrix(n_cmp_rows, t)
    gw = GROUP * HEAD_DIM
    rows = GROUP * tq
    kv_spec = pl.BlockSpec((1, t, LANES), lambda i, h, j: (i, 0, h))
    cmp_spec = pl.BlockSpec((1, 1, n_cmp_rows, LANES), lambda i, h, j: (i, h, 0, 0))
    return pl.pallas_call(
        functools.partial(_nsa_attn_kernel, tk=tk),
        grid=(b, N_KVH, t // tq),
        in_specs=[
            pl.BlockSpec((1, tq, gw), lambda i, h, j: (i, j, h)),
            pl.BlockSpec((1, tq, LANES), lambda i, h, j: (i, j, h)),
            cmp_spec, cmp_spec, kv_spec, kv_spec, kv_spec, kv_spec,
            _resident(ov.shape),
        ],
        out_specs=pl.BlockSpec((1, tq, gw), lambda i, h, j: (i, j, h)),
        out_shape=jax.ShapeDtypeStruct((b, t, N_QH * HEAD_DIM), BF16),
        scratch_shapes=[pltpu.VMEM((t // tk, rows, tk), F32), pltpu.VMEM((rows, LANES), F32),
                        pltpu.VMEM((rows, LANES), F32)],
        compiler_params=_params("parallel", "parallel", "arbitrary"),
        name="nsa_attn",
    )(q, gates, cmpk, cmpv, selk, selv, wink, winv, ov)


def mlstm_layer(h2, b, t, g, w_in, b_if, g_head, w_out):
    qk, v, og, gates = mlstm_proj(h2, g, w_in, b_if)
    L = MLSTM_CHUNK
    gates_t = gates[:, :2 * M_HEADS].reshape(b, t // L, L, 2 * M_HEADS).transpose(0, 1, 3, 2)
    hn = mlstm_chunk(qk.reshape(b, t, -1), v.reshape(b, t, -1), gates_t, g_head)
    return resid_matmul(h2, hn.reshape(b * t, -1), w_out, og)


def nsa_shared_kv(h2, b, t, kv_norm, kv_w, cmp_pe, cmp_w1, cmp_w2, k_norm):
    cmp_flat, *packs = kv_proj(h2, t, kv_norm, kv_w, k_norm)
    cmpk, cmpv = cmp_kv(cmp_flat, b, t, cmp_pe, cmp_w1, cmp_w2, k_norm[0])
    return (cmpk, cmpv) + tuple(p.reshape(b, t, -1) for p in packs)


def nsa_layer(h2, b, t, kv, g, w_in, q_norm, w_out):
    q, gates = nsa_proj(h2, g, w_in, q_norm)
    o = nsa_attn(q.reshape(b, t, -1), gates.reshape(b, t, -1), kv)
    return resid_matmul(h2, o.reshape(b * t, -1), w_out)


def kernel(x, ffn_norm, ffn_w_in, ffn_w_out, mix_norm, a_w_in, a_b_if, a_g_head, a_w_out, kv_norm, kv_w, cmp_pe, cmp_w1, cmp_w2, k_norm, b_w_in, b_q_norm, b_w_out):
    b, t, d = x.shape
    depth = ffn_norm.shape[0]
    n_a = a_w_in.shape[0]
    h = x.reshape(b * t, d)
    kv = None
    for layer in range(depth):
        h = ffn(h, ffn_norm[layer, 0], ffn_w_in[layer, 0], ffn_w_out[layer, 0])
        if layer < n_a:
            h = mlstm_layer(h, b, t, mix_norm[layer], a_w_in[layer], a_b_if[layer], a_g_head[layer], a_w_out[layer])
        else:
            j = layer - n_a
            h = nsa_layer(h, b, t, kv, mix_norm[layer], b_w_in[j], b_q_norm[j], b_w_out[j])
        h = ffn(h, ffn_norm[layer, 1], ffn_w_in[layer, 1], ffn_w_out[layer, 1])
        if layer == n_a - 1:
            kv = nsa_shared_kv(h, b, t, kv_norm, kv_w, cmp_pe, cmp_w1, cmp_w2, k_norm)
    return h.reshape(b, t, d)
```

```python
import functools

import jax
import jax.numpy as jnp
import numpy as np
from jax import lax
from jax.experimental import pallas as pl
from jax.experimental.pallas import tpu as pltpu

F32 = jnp.float32
BF16 = jnp.bfloat16

D_MODEL = 1024
D_FF = 2816
EPS = 1e-6
NEG = -1e30

M_HEADS = 8
M_DV = 128
M_DQK = 64

N_QH = 16
N_KVH = 4
GROUP = N_QH // N_KVH
HEAD_DIM = 64
CMP_BLOCK = 32
CMP_STRIDE = 16
CMP_HIDDEN = 256
SEL_BLOCK = 64
SEL_TOPK = 16
WINDOW = 512
FORCE_SCORE = 1e4
LOG2E = 1.4426950408889634
SEL_BIAS = -2.0 ** 100

LANES = 128
VMEM_LIMIT = 56 * 1024 * 1024

TOKEN_TILE = 512
FFN_CHUNK = 256
MLSTM_CHUNK = 256
Q_TILE = 256
SEL_KEY_CHUNK = 512

_NT = (((1,), (1,)), ((), ()))
_TN = (((0,), (0,)), ((), ()))


def _params(*sem):
    return pltpu.CompilerParams(dimension_semantics=sem, vmem_limit_bytes=VMEM_LIMIT)


def _rms(x, g):
    ms = jnp.mean(x * x, axis=-1, keepdims=True)
    return (x * lax.rsqrt(ms + EPS)) * g


def _dot(a, b):
    return jnp.dot(a, b, preferred_element_type=F32)


def _dot_nt(a, b, precision=None):
    return lax.dot_general(a, b, _NT, precision=precision, preferred_element_type=F32)


def _resident(shape):
    nd = len(shape)
    return pl.BlockSpec(shape, lambda *_: (0,) * nd)


def _ffn_kernel(x_ref, g_ref, win_ref, wout_ref, o_ref, xn_ref, acc_ref):
    f = wout_ref.shape[0]
    x = x_ref[...]
    xn_ref[...] = _rms(x, g_ref[...]).astype(BF16)
    for c in range(f // FFN_CHUNK):
        cols = slice(c * FFN_CHUNK, (c + 1) * FFN_CHUNK)
        up_cols = slice(f + c * FFN_CHUNK, f + (c + 1) * FFN_CHUNK)
        xn = xn_ref[...]
        a = _dot(xn, win_ref[:, cols])
        b = _dot(xn, win_ref[:, up_cols])
        hid = ((a * jax.nn.sigmoid(a)) * b).astype(BF16)
        part = _dot(hid, wout_ref[cols, :])
        if c == 0:
            acc_ref[...] = part
        else:
            acc_ref[...] += part
    o_ref[...] = x_ref[...] + 0.5 * acc_ref[...]


def ffn(x2, g, w_in_all, w_out_all, layer, j):
    n, d = x2.shape
    f = w_out_all.shape[2]
    assert f % FFN_CHUNK == 0
    tm = TOKEN_TILE
    return pl.pallas_call(
        _ffn_kernel,
        grid=(n // tm,),
        in_specs=[
            pl.BlockSpec((tm, d), lambda i: (i, 0)),
            _resident((1, d)),
            pl.BlockSpec((None, None, d, 2 * f), lambda i: (layer, j, 0, 0)),
            pl.BlockSpec((None, None, f, d), lambda i: (layer, j, 0, 0)),
        ],
        out_specs=pl.BlockSpec((tm, d), lambda i: (i, 0)),
        out_shape=jax.ShapeDtypeStruct((n, d), F32),
        scratch_shapes=[pltpu.VMEM((tm, d), BF16), pltpu.VMEM((tm, d), F32)],
        compiler_params=_params("parallel"),
        name="ffn",
    )(x2, g.reshape(1, d), w_in_all, w_out_all)


def _mlstm_proj_kernel(x_ref, g_ref, wqk_ref, wv_ref, wog_ref, wg_ref, bg_ref,
                       qk_ref, v_ref, og_ref, gate_ref):
    xn = _rms(x_ref[...], g_ref[...]).astype(BF16)
    qk_ref[...] = _dot(xn, wqk_ref[...]).astype(BF16)
    v_ref[...] = _dot(xn, wv_ref[...]).astype(BF16)
    og_ref[...] = _dot(xn, wog_ref[...])
    gate_ref[...] = _dot(xn, wg_ref[...]) + bg_ref[...]


def mlstm_proj(h2, g, w_in, b_if):
    n, d = h2.shape
    qk_w, v_w = M_HEADS * M_DQK, M_HEADS * M_DV
    wb = w_in.astype(BF16)
    wqk = wb[:, :2 * qk_w]
    wv = wb[:, 2 * qk_w:2 * qk_w + v_w]
    wgate = jnp.pad(wb[:, 2 * qk_w + v_w:2 * qk_w + v_w + 2 * M_HEADS], ((0, 0), (0, LANES - 2 * M_HEADS)))
    wog = wb[:, 2 * qk_w + v_w + 2 * M_HEADS:]
    bg = jnp.pad(b_if.astype(F32).reshape(1, 2 * M_HEADS), ((0, 0), (0, LANES - 2 * M_HEADS)))
    tm = TOKEN_TILE
    tile = lambda w: pl.BlockSpec((tm, w), lambda i: (i, 0))
    return pl.pallas_call(
        _mlstm_proj_kernel,
        grid=(n // tm,),
        in_specs=[tile(d), _resident((1, d)), _resident(wqk.shape), _resident(wv.shape),
                  _resident(wog.shape), _resident(wgate.shape), _resident(bg.shape)],
        out_specs=[tile(2 * qk_w), tile(v_w), tile(d), tile(LANES)],
        out_shape=[jax.ShapeDtypeStruct((n, 2 * qk_w), BF16), jax.ShapeDtypeStruct((n, v_w), BF16),
                   jax.ShapeDtypeStruct((n, d), F32), jax.ShapeDtypeStruct((n, LANES), F32)],
        compiler_params=_params("parallel"),
        name="mlstm_proj",
    )(h2, g.reshape(1, d), wqk, wv, wog, wgate, bg)


def _log_sigmoid(x):
    return jnp.minimum(x, 0.0) - jnp.log1p(jnp.exp(-jnp.abs(x)))


def _mlstm_chunk_kernel(q_ref, k_ref, v_ref, gate_ref, ghead_ref, o_ref, c_ref, li_r_ref, b_r_ref):
    pair = pl.program_id(1)
    L = MLSTM_CHUNK
    n_chunks = q_ref.shape[1] // L
    hi = lax.Precision.HIGHEST

    r_i = lax.broadcasted_iota(jnp.int32, (L, L), 0)
    c_i = lax.broadcasted_iota(jnp.int32, (L, L), 1)
    causal = c_i <= r_i
    tril = causal.astype(F32)
    lane = lax.broadcasted_iota(jnp.int32, (L, LANES), 1)
    ones_col = (lane == 0).astype(BF16)

    c_ref[...] = jnp.zeros_like(c_ref)

    for hh in range(2):
        li_r_ref[hh] = gate_ref[0, 0, 0, hh]
        b_r_ref[hh] = _dot_nt(_log_sigmoid(gate_ref[0, 0, 1, hh]), tril, hi)
    row_id = lax.broadcasted_iota(jnp.int32, (8, L), 0)
    pad_rows = jnp.zeros((LANES - 8, L), F32)

    def body(c, m_prev):
        base = pl.multiple_of(c * L, L)
        q2 = q_ref[0, pl.ds(base, L), :]
        k2 = k_ref[0, pl.ds(base, L), :]
        rows4 = [li_r_ref[0, pl.ds(c, 1), :], b_r_ref[0, pl.ds(c, 1), :],
                 li_r_ref[1, pl.ds(c, 1), :], b_r_ref[1, pl.ds(c, 1), :]]
        stacked = jnp.zeros((8, L), F32)
        for r, row in enumerate(rows4):
            stacked = jnp.where(row_id == r, row, stacked)
        cols = jnp.concatenate([stacked, pad_rows], axis=0).T
        m_next = []
        for hh in range(2):
            head = 2 * pair + hh
            in_head = (lane >= hh * M_DQK) & (lane < (hh + 1) * M_DQK)
            qh = jnp.where(in_head, q2, 0).astype(F32)
            qh = (qh * (M_DQK ** -0.5)).astype(BF16)
            kh = jnp.where(in_head, k2, 0)
            v_aug = jnp.concatenate([v_ref[0, pl.ds(base, L), hh * M_DV:(hh + 1) * M_DV], ones_col], axis=1)

            li_r, b_r = rows4[2 * hh], rows4[2 * hh + 1]
            li_c, b_c = cols[:, 2 * hh:2 * hh + 1], cols[:, 2 * hh + 1:2 * hh + 2]
            g_tot = b_r[:, L - 1:L]

            a_c = g_tot - b_c + li_c
            m_loc = jnp.max(a_c, axis=0, keepdims=True)
            kw = (kh.astype(F32) * jnp.exp(a_c - m_loc)).astype(BF16)
            c_loc = lax.dot_general(kw, v_aug, _TN, preferred_element_type=F32)

            m_p = m_prev[hh]
            c_prev = c_ref[hh]
            dmat = jnp.where(causal, b_c - (b_r - li_r), -jnp.inf)
            inter = b_c + m_p
            m_t = jnp.maximum(inter, jnp.max(dmat, axis=1, keepdims=True))
            w = jnp.exp(dmat - m_t) * _dot_nt(qh, kh)
            num = jnp.exp(inter - m_t) * _dot(qh, c_prev.astype(BF16)) + _dot(w.astype(BF16), v_aug)
            den = num[:, M_DV:M_DV + 1]
            h_out = num[:, :M_DV] / jnp.maximum(jnp.abs(den), jnp.exp(-m_t))
            h_out = _rms(h_out, ghead_ref[pl.ds(head, 1), :])
            o_ref[0, pl.ds(base, L), hh * M_DV:(hh + 1) * M_DV] = h_out

            m_new = jnp.maximum(g_tot + m_p, m_loc)
            c_ref[hh] = jnp.exp(g_tot + m_p - m_new) * c_prev + jnp.exp(m_loc - m_new) * c_loc
            m_next.append(m_new)
        return tuple(m_next)

    zero = jnp.zeros((1, 1), F32)
    lax.fori_loop(0, n_chunks, body, (zero, zero))


def mlstm_chunk(qk, v, gates_t, g_head):
    b, t, _ = v.shape
    L = MLSTM_CHUNK
    nc = t // L
    pairs = M_HEADS // 2
    return pl.pallas_call(
        _mlstm_chunk_kernel,
        grid=(b, pairs),
        in_specs=[
            pl.BlockSpec((1, t, LANES), lambda i, p: (i, 0, p)),
            pl.BlockSpec((1, t, LANES), lambda i, p: (i, 0, pairs + p)),
            pl.BlockSpec((1, t, 2 * M_DV), lambda i, p: (i, 0, p)),
            pl.BlockSpec((1, 1, 2, 2, nc, L), lambda i, p: (i, p, 0, 0, 0, 0)),
            _resident((M_HEADS, M_DV)),
        ],
        out_specs=pl.BlockSpec((1, t, 2 * M_DV), lambda i, p: (i, 0, p)),
        out_shape=jax.ShapeDtypeStruct((b, t, M_HEADS * M_DV), F32),
        scratch_shapes=[pltpu.VMEM((2, 2 * M_DQK, 2 * M_DV), F32),
                        pltpu.VMEM((2, nc, L), F32), pltpu.VMEM((2, nc, L), F32)],
        compiler_params=_params("parallel", "parallel"),
        name="mlstm_chunk",
    )(qk, qk, v, gates_t, g_head.astype(F32))


def _resid_matmul_kernel(h_ref, a_ref, w_ref, o_ref):
    o_ref[...] = h_ref[...] + _dot(a_ref[...].astype(BF16), w_ref[...])


def _resid_gated_matmul_kernel(h_ref, a_ref, og_ref, w_ref, o_ref):
    a = (a_ref[...] * jax.nn.sigmoid(og_ref[...])).astype(BF16)
    o_ref[...] = h_ref[...] + _dot(a, w_ref[...])


def resid_matmul(h2, a2, w, og2=None):
    n, d = h2.shape
    k = a2.shape[1]
    tm = TOKEN_TILE
    tile = lambda wd: pl.BlockSpec((tm, wd), lambda i: (i, 0))
    if og2 is None:
        kern, ins, specs = _resid_matmul_kernel, (h2, a2), [tile(d), tile(k)]
    else:
        kern, ins, specs = _resid_gated_matmul_kernel, (h2, a2, og2), [tile(d), tile(k), tile(k)]
    return pl.pallas_call(
        kern,
        grid=(n // tm,),
        in_specs=specs + [_resident((k, d))],
        out_specs=tile(d),
        out_shape=jax.ShapeDtypeStruct((n, d), F32),
        compiler_params=_params("parallel"),
        name="resid_matmul",
    )(*ins, w.astype(BF16))


def _kv_packs(y, gain, extra):
    lane = lax.broadcasted_iota(jnp.int32, (1, LANES), 1)
    lo = lane < HEAD_DIM
    kp, vp = [], []
    for h in range(N_KVH):
        yh = y[:, h * LANES:(h + 1) * LANES]
        ms = jnp.sum(jnp.where(lo, yh * yh, 0.0), axis=-1, keepdims=True) * (1.0 / HEAD_DIM)
        kp.append(jnp.where(lo, (yh * lax.rsqrt(ms + EPS)) * gain, extra))
        vp.append(jnp.where(lo, 1.0, yh))
    return jnp.concatenate(kp, axis=1), jnp.concatenate(vp, axis=1)


def _kv_proj_kernel(x_ref, g_ref, wc_ref, ws_ref, ww_ref, gs_ref, gw_ref,
                    cmp_ref, selk_ref, selv_ref, wink_ref, winv_ref, y_ref, *, seq_len):
    tm = x_ref.shape[0]
    xn = _rms(x_ref[...], g_ref[...]).astype(BF16)

    y = _dot(xn, wc_ref[...])
    n_slab = y.shape[1] // LANES
    for s in range(n_slab):
        y_ref[s] = y[:, s * LANES:(s + 1) * LANES]
    groups = tm // CMP_STRIDE
    lo = lax.broadcasted_iota(jnp.int32, (1, LANES), 1) < HEAD_DIM
    for s in range(n_slab):
        for rp in range(CMP_STRIDE // 2):
            even = y_ref[s, pl.ds(2 * rp, groups, stride=CMP_STRIDE), :]
            odd = y_ref[s, pl.ds(2 * rp + 1, groups, stride=CMP_STRIDE), :]
            cols = slice(rp * LANES, (rp + 1) * LANES)
            cmp_ref[2 * s, :, cols] = jnp.where(lo, even, pltpu.roll(odd, HEAD_DIM, 1))
            cmp_ref[2 * s + 1, :, cols] = jnp.where(lo, pltpu.roll(even, HEAD_DIM, 1), odd)

    pos = (pl.program_id(0) * tm) % seq_len + lax.broadcasted_iota(jnp.int32, (tm, 1), 0)
    lane = lax.broadcasted_iota(jnp.int32, (1, LANES), 1)
    block_bias = jnp.where(lane == HEAD_DIM + lax.shift_right_logical(pos, 6), SEL_BIAS, 0.0)
    kp, vp = _kv_packs(_dot(xn, ws_ref[...]), gs_ref[...], block_bias)
    selk_ref[...] = kp.astype(BF16)
    selv_ref[...] = vp.astype(BF16)
    kp, vp = _kv_packs(_dot(xn, ww_ref[...]), gw_ref[...], 0.0)
    wink_ref[...] = kp.astype(BF16)
    winv_ref[...] = vp.astype(BF16)


def _kv_pack_weights(wk, wv):
    d = wk.shape[0]
    wk = wk.reshape(d, N_KVH, HEAD_DIM)
    wv = wv.reshape(d, N_KVH, HEAD_DIM)
    return jnp.concatenate([wk, wv], axis=-1).reshape(d, N_KVH * LANES)


def _k_gain_row(gain):
    return jnp.concatenate([gain.astype(F32), jnp.ones((HEAD_DIM,), F32)]).reshape(1, LANES)


def kv_proj(h2, t, g, kv_w, k_norm):
    n, d = h2.shape
    wb = kv_w.astype(BF16)
    kc, vc, ks, vs, kw, vw = jnp.split(wb, 6, axis=1)
    wcmp = jnp.concatenate([kc, vc], axis=1)
    wsel = _kv_pack_weights(ks, vs)
    wwin = _kv_pack_weights(kw, vw)
    wd = N_KVH * LANES
    tm = TOKEN_TILE
    assert t % tm == 0 and t // SEL_BLOCK <= LANES - HEAD_DIM
    tile = lambda w: pl.BlockSpec((tm, w), lambda i: (i, 0))
    packed = jax.ShapeDtypeStruct((n, wd), BF16)
    feat = CMP_STRIDE * HEAD_DIM
    groups = tm // CMP_STRIDE
    return pl.pallas_call(
        functools.partial(_kv_proj_kernel, seq_len=t),
        grid=(n // tm,),
        in_specs=[tile(d), _resident((1, d)), _resident(wcmp.shape), _resident(wsel.shape),
                  _resident(wwin.shape), _resident((1, LANES)), _resident((1, LANES))],
        out_specs=[pl.BlockSpec((2 * N_KVH, groups, feat), lambda i: (0, i, 0))] + [tile(wd)] * 4,
        out_shape=[jax.ShapeDtypeStruct((2 * N_KVH, n // CMP_STRIDE, feat), F32), packed, packed, packed, packed],
        scratch_shapes=[pltpu.VMEM((2 * N_KVH * HEAD_DIM // LANES, tm, LANES), F32)],
        compiler_params=_params("parallel"),
        name="kv_proj",
    )(h2, g.reshape(1, d), wcmp, wsel, wwin, _k_gain_row(k_norm[1]), _k_gain_row(k_norm[2]))


def _cmp_kernel(xk_ref, xv_ref, pe_ref, w1_ref, w2k_ref, w2v_ref, gk_ref, ok_ref, ov_ref):
    def compress(x, kv, w2):
        u = _dot((x + pe_ref[kv, 0]).astype(BF16), w1_ref[kv, 0])
        v = _dot((x + pe_ref[kv, 1]).astype(BF16), w1_ref[kv, 1])
        nrow = v.shape[0]
        hid = u + pltpu.roll(v, nrow - 1, 0)
        return _dot((hid * jax.nn.sigmoid(hid)).astype(BF16), w2)

    yk = compress(xk_ref[0], 0, w2k_ref[...])
    yv = compress(xv_ref[0], 1, w2v_ref[...])
    ms = jnp.sum(yk * yk, axis=-1, keepdims=True) * (1.0 / HEAD_DIM)
    ok_ref[0, 0] = ((yk * lax.rsqrt(ms + EPS)) * gk_ref[...]).astype(BF16)
    lane = lax.broadcasted_iota(jnp.int32, (1, LANES), 1)
    ov_ref[0, 0] = jnp.where(lane < HEAD_DIM, 1.0, yv).astype(BF16)


def cmp_kv(cmp_x, b, t, cmp_pe, cmp_w1, cmp_w2, k_gain):
    ng = t // CMP_STRIDE
    feat = CMP_STRIDE * HEAD_DIM
    pe = cmp_pe.astype(F32).reshape(2, 2, 1, feat)
    w1 = cmp_w1.astype(BF16).reshape(2, 2, feat, CMP_HIDDEN)
    zeros = jnp.zeros((CMP_HIDDEN, HEAD_DIM), BF16)
    w2k = jnp.concatenate([cmp_w2[0].astype(BF16), zeros], axis=1)
    w2v = jnp.concatenate([zeros, cmp_w2[1].astype(BF16)], axis=1)
    ospec = pl.BlockSpec((1, 1, ng, LANES), lambda i, h: (i, h, 0, 0))
    packed = jax.ShapeDtypeStruct((b, N_KVH, ng, LANES), BF16)
    return pl.pallas_call(
        _cmp_kernel,
        grid=(b, N_KVH),
        in_specs=[pl.BlockSpec((1, ng, feat), lambda i, h: (h, i, 0)),
                  pl.BlockSpec((1, ng, feat), lambda i, h: (N_KVH + h, i, 0)),
                  _resident(pe.shape), _resident(w1.shape), _resident(w2k.shape),
                  _resident(w2v.shape), _resident((1, LANES))],
        out_specs=[ospec, ospec],
        out_shape=[packed, packed],
        compiler_params=_params("parallel", "parallel"),
        name="cmp_kv",
    )(cmp_x, cmp_x, pe, w1, w2k, w2v, _k_gain_row(k_gain))


def _nsa_proj_kernel(x_ref, g_ref, wq_ref, wg_ref, qg_ref, q_ref, gate_ref):
    xn = _rms(x_ref[...], g_ref[...]).astype(BF16)
    q = _dot(xn, wq_ref[...])
    lane = lax.broadcasted_iota(jnp.int32, (1, LANES), 1)
    lo = lane < HEAD_DIM
    outs = []
    for cb in range(q.shape[1] // LANES):
        y = q[:, cb * LANES:(cb + 1) * LANES]
        y2 = y * y
        ms_lo = jnp.sum(jnp.where(lo, y2, 0.0), axis=-1, keepdims=True) * (1.0 / HEAD_DIM)
        ms_hi = jnp.sum(jnp.where(lo, 0.0, y2), axis=-1, keepdims=True) * (1.0 / HEAD_DIM)
        scale = jnp.where(lo, lax.rsqrt(ms_lo + EPS), lax.rsqrt(ms_hi + EPS))
        outs.append(((y * scale) * qg_ref[...]) * (HEAD_DIM ** -0.5 * LOG2E))
    q_ref[...] = jnp.concatenate(outs, axis=1).astype(BF16)
    gate_ref[...] = jax.nn.sigmoid(_dot(xn, wg_ref[...]))


def nsa_proj(h2, g, w_in, q_norm):
    n, d = h2.shape
    wb = w_in.astype(BF16)
    wq = wb[:, :N_QH * HEAD_DIM]
    wg = wb[:, N_QH * HEAD_DIM:].reshape(d, 3, N_KVH, GROUP).transpose(0, 2, 1, 3).reshape(d, N_KVH, 3 * GROUP)
    wg = jnp.pad(wg, ((0, 0), (0, 0), (0, LANES - 3 * GROUP))).reshape(d, N_KVH * LANES)
    qg = jnp.tile(q_norm.astype(F32), LANES // HEAD_DIM).reshape(1, LANES)
    tm = TOKEN_TILE
    tile = lambda w: pl.BlockSpec((tm, w), lambda i: (i, 0))
    return pl.pallas_call(
        _nsa_proj_kernel,
        grid=(n // tm,),
        in_specs=[tile(d), _resident((1, d)), _resident(wq.shape), _resident(wg.shape), _resident((1, LANES))],
        out_specs=[tile(N_QH * HEAD_DIM), tile(N_KVH * LANES)],
        out_shape=[jax.ShapeDtypeStruct((n, N_QH * HEAD_DIM), BF16),
                   jax.ShapeDtypeStruct((n, N_KVH * LANES), F32)],
        compiler_params=_params("parallel"),
        name="nsa_proj",
    )(h2, g.reshape(1, d), wq, wg, qg)


def _masked_exp2(s3, mask):
    sm = jnp.where(mask[None], s3, NEG)
    mx = jnp.max(sm, axis=-1, keepdims=True)
    return jnp.exp2(sm - mx), mx > 0.5 * NEG


def _normalise(acc, values_lo):
    lo = lax.broadcasted_iota(jnp.int32, acc.shape, 1) < HEAD_DIM
    return acc / jnp.where(lo == values_lo, pltpu.roll(acc, HEAD_DIM, 1), 1.0)


def _nsa_attn_kernel(q_ref, gate_ref, cmpk_ref, cmpv_ref, selk_ref, selv_ref, wink_ref, winv_ref,
                     ov_ref, o_ref, s_ref, mx_ref, acc_ref, *, tk):
    tq = q_ref.shape[1]
    rows = GROUP * tq
    s0 = pl.program_id(2) * tq

    lane = lax.broadcasted_iota(jnp.int32, (tq, LANES), 1)
    lo = lane < HEAD_DIM
    qs = []
    for g in range(GROUP):
        blk = q_ref[0, :, (g // 2) * LANES:(g // 2 + 1) * LANES].astype(F32)
        if g % 2 == 1:
            blk = pltpu.roll(blk, HEAD_DIM, 1)
        qs.append(jnp.where(lo, blk, 0.0))
    q4 = jnp.concatenate(qs, axis=0)
    qst = q4.astype(BF16)
    t_q = s0 + lax.broadcasted_iota(jnp.int32, (tq, 1), 0)

    kc = cmpk_ref[0, 0]
    n_cmp = kc.shape[0]
    cmp_end = lax.broadcasted_iota(jnp.int32, (1, n_cmp), 1) * CMP_STRIDE + (CMP_BLOCK - 1)
    e_c, ok_c = _masked_exp2(_dot_nt(qst, kc).reshape(GROUP, tq, n_cmp), cmp_end <= t_q)
    e_c = e_c.reshape(rows, n_cmp)
    ok_c = ok_c.reshape(rows, 1)
    e_hi = e_c.astype(BF16)
    e_lo = (e_c - e_hi.astype(F32)).astype(BF16)
    o_c = jnp.where(ok_c, _normalise(_dot(e_hi, cmpv_ref[0, 0]), False), 0.0)

    imp4 = jnp.where(ok_c, _normalise(_dot(e_hi, ov_ref[...]) + _dot(e_lo, ov_ref[...]), True), 0.0)
    imp4 = imp4.reshape(GROUP, tq, LANES)
    imp = imp4[0] + imp4[1] + imp4[2] + imp4[3]

    n_sel = selk_ref.shape[1] // SEL_BLOCK
    valid = (lane * SEL_BLOCK <= t_q) & (lane < n_sel)
    cur = lax.shift_right_logical(t_q, 6)
    forced = (lane == 0) | (lane == cur) | (lane == cur - 1)
    score = jnp.where(forced & valid, FORCE_SCORE, jnp.where(valid, imp, -1.0))
    score = jnp.where(lane < n_sel, score, -2.0)
    s_t = score.T[:HEAD_DIM]
    sub = lax.broadcasted_iota(jnp.int32, (8, tq), 0)
    groups = [s_t[8 * v:8 * v + 8] for v in range(HEAD_DIM // 8)]
    ranks = [jnp.zeros((8, tq), F32) for _ in groups]
    for i in range(n_sel):
        s_i = s_t[i:i + 1, :]
        for v, s_v in enumerate(groups):
            if 8 * v + 7 < i:
                ahead = s_i > s_v
            elif 8 * v > i:
                ahead = s_i >= s_v
            else:
                ahead = (s_i > s_v) | ((s_i == s_v) & (sub > i - 8 * v))
            ranks[v] = ranks[v] + jnp.where(ahead, 1.0, 0.0)
    rank = jnp.concatenate(ranks, axis=0)
    blk_t = lax.broadcasted_iota(jnp.int32, (HEAD_DIM, tq), 0)
    t_row = s0 + lax.broadcasted_iota(jnp.int32, (1, tq), 1)
    dropped_t = jnp.where((rank < SEL_TOPK) & (blk_t * SEL_BLOCK <= t_row), 0.0, 1.0)
    dropped = jnp.concatenate([jnp.zeros((HEAD_DIM, tq), F32), dropped_t], axis=0).T
    qx = (q4 + jnp.concatenate([dropped] * GROUP, axis=0)).astype(BF16)

    wk = WINDOW + tq
    w0 = pl.multiple_of(jnp.maximum(s0 - WINDOW, 0), tq)
    kpos_w = w0 + lax.broadcasted_iota(jnp.int32, (1, wk), 1)
    e_w, _ = _masked_exp2(_dot_nt(qst, wink_ref[0, pl.ds(w0, wk), :]).reshape(GROUP, tq, wk),
                          (kpos_w <= t_q) & (kpos_w > t_q - WINDOW))
    o_w = _normalise(_dot(e_w.reshape(rows, wk).astype(BF16), winv_ref[0, pl.ds(w0, wk), :]), False)

    nblk = tk // LANES
    last = s0 // tk
    mx_ref[...] = jnp.full(mx_ref.shape, NEG, F32)

    def scores(c, causal):
        base = pl.multiple_of(c * tk, tk)
        s = _dot_nt(qx, selk_ref[0, pl.ds(base, tk), :])
        if causal:
            kpos = base + lax.broadcasted_iota(jnp.int32, (1, tk), 1)
            s = jnp.where((kpos <= t_q)[None], s.reshape(GROUP, tq, tk), NEG).reshape(rows, tk)
        s_ref[c] = s
        m = mx_ref[...]
        for j in range(nblk):
            m = jnp.maximum(m, s[:, j * LANES:(j + 1) * LANES])
        mx_ref[...] = m

    def pass1(c, carry):
        scores(c, causal=False)
        return carry

    lax.fori_loop(0, last, pass1, 0)
    scores(last, causal=True)
    mx_ref[...] = jnp.broadcast_to(jnp.max(mx_ref[...], axis=-1, keepdims=True), mx_ref.shape)

    acc_ref[...] = jnp.zeros_like(acc_ref)

    def pass2(c, carry):
        base = pl.multiple_of(c * tk, tk)
        s = s_ref[c]
        m = mx_ref[...]
        p = jnp.concatenate([jnp.exp2(s[:, j * LANES:(j + 1) * LANES] - m) for j in range(nblk)], axis=1)
        acc_ref[...] += _dot(p.astype(BF16), selv_ref[0, pl.ds(base, tk), :])
        return carry

    lax.fori_loop(0, last + 1, pass2, 0)
    o_s = _normalise(acc_ref[...], False)

    gt = gate_ref[0]

    def gate(branch):
        return jnp.concatenate([gt[:, branch * GROUP + g:branch * GROUP + g + 1] for g in range(GROUP)], axis=0)

    o = (gate(0) * o_c + gate(1) * o_s + gate(2) * o_w).reshape(GROUP, tq, LANES)
    out01 = jnp.where(lo, pltpu.roll(o[0], HEAD_DIM, 1), o[1])
    out23 = jnp.where(lo, pltpu.roll(o[2], HEAD_DIM, 1), o[3])
    o_ref[0] = jnp.concatenate([out01, out23], axis=1).astype(BF16)


def _overlap_matrix(n_cmp_rows, t):
    n_sel = t // SEL_BLOCK
    c0 = np.arange(n_cmp_rows) * CMP_STRIDE
    s_0 = np.arange(n_sel) * SEL_BLOCK
    ov = (c0[:, None] < s_0[None, :] + SEL_BLOCK) & (c0[:, None] + CMP_BLOCK > s_0[None, :])
    ov[(t - CMP_BLOCK) // CMP_STRIDE + 1:] = False
    out = np.zeros((n_cmp_rows, LANES), np.float32)
    out[:, :n_sel] = ov
    out[:, HEAD_DIM:] = 1.0
    return jnp.asarray(out, BF16)


def nsa_attn(q, gates, kv):
    cmpk, cmpv, selk, selv, wink, winv = kv
    b, t, _ = q.shape
    tq = Q_TILE
    tk = min(SEL_KEY_CHUNK, t)
    n_cmp_rows = cmpk.shape[2]
    assert t // SEL_BLOCK <= LANES - HEAD_DIM and (t // SEL_BLOCK) % 8 == 0
    assert t % tk == 0 and tk % tq == 0 and t >= WINDOW + tq
    ov = _overlap_matrix(n_cmp_rows, t)
    gw = GROUP * HEAD_DIM
    rows = GROUP * tq
    kv_spec = pl.BlockSpec((1, t, LANES), lambda i, h, j: (i, 0, h))
    cmp_spec = pl.BlockSpec((1, 1, n_cmp_rows, LANES), lambda i, h, j: (i, h, 0, 0))
    return pl.pallas_call(
        functools.partial(_nsa_attn_kernel, tk=tk),
        grid=(b, N_KVH, t // tq),
        in_specs=[
            pl.BlockSpec((1, tq, gw), lambda i, h, j: (i, j, h)),
            pl.BlockSpec((1, tq, LANES), lambda i, h, j: (i, j, h)),
            cmp_spec, cmp_spec, kv_spec, kv_spec, kv_spec, kv_spec,
            _resident(ov.shape),
        ],
        out_specs=pl.BlockSpec((1, tq, gw), lambda i, h, j: (i, j, h)),
        out_shape=jax.ShapeDtypeStruct((b, t, N_QH * HEAD_DIM), BF16),
        scratch_shapes=[pltpu.VMEM((t // tk, rows, tk), F32), pltpu.VMEM((rows, LANES), F32),
                        pltpu.VMEM((rows, LANES), F32)],
        compiler_params=_params("parallel", "parallel", "arbitrary"),
        name="nsa_attn",
    )(q, gates, cmpk, cmpv, selk, selv, wink, winv, ov)


def mlstm_layer(h2, b, t, g, w_in, b_if, g_head, w_out):
    qk, v, og, gates = mlstm_proj(h2, g, w_in, b_if)
    L = MLSTM_CHUNK
    gates_t = gates[:, :2 * M_HEADS].reshape(b, t // L, L, 2, M_HEADS // 2, 2).transpose(0, 4, 3, 5, 1, 2)
    hn = mlstm_chunk(qk.reshape(b, t, -1), v.reshape(b, t, -1), gates_t, g_head)
    return resid_matmul(h2, hn.reshape(b * t, -1), w_out, og)


def nsa_shared_kv(h2, b, t, kv_norm, kv_w, cmp_pe, cmp_w1, cmp_w2, k_norm):
    cmp_x, *packs = kv_proj(h2, t, kv_norm, kv_w, k_norm)
    cmpk, cmpv = cmp_kv(cmp_x, b, t, cmp_pe, cmp_w1, cmp_w2, k_norm[0])
    return (cmpk, cmpv) + tuple(p.reshape(b, t, -1) for p in packs)


def nsa_layer(h2, b, t, kv, g, w_in, q_norm, w_out):
    q, gates = nsa_proj(h2, g, w_in, q_norm)
    o = nsa_attn(q.reshape(b, t, -1), gates.reshape(b, t, -1), kv)
    return resid_matmul(h2, o.reshape(b * t, -1), w_out)


def kernel(x, ffn_norm, ffn_w_in, ffn_w_out, mix_norm, a_w_in, a_b_if, a_g_head, a_w_out, kv_norm, kv_w, cmp_pe, cmp_w1, cmp_w2, k_norm, b_w_in, b_q_norm, b_w_out):
    b, t, d = x.shape
    depth = ffn_norm.shape[0]
    n_a = a_w_in.shape[0]
    h = x.reshape(b * t, d)
    w_in_bf, w_out_bf = ffn_w_in.astype(BF16), ffn_w_out.astype(BF16)
    kv = None
    for layer in range(depth):
        h = ffn(h, ffn_norm[layer, 0], w_in_bf, w_out_bf, layer, 0)
        if layer < n_a:
            h = mlstm_layer(h, b, t, mix_norm[layer], a_w_in[layer], a_b_if[layer], a_g_head[layer], a_w_out[layer])
        else:
            j = layer - n_a
            h = nsa_layer(h, b, t, kv, mix_norm[layer], b_w_in[j], b_q_norm[j], b_w_out[j])
        h = ffn(h, ffn_norm[layer, 1], w_in_bf, w_out_bf, layer, 1)
        if layer == n_a - 1:
            kv = nsa_shared_kv(h, b, t, kv_norm, kv_w, cmp_pe, cmp_w1, cmp_w2, k_norm)
    return h.reshape(b, t, d)
```

```python
import functools

import jax
import jax.numpy as jnp
import numpy as np
from jax import lax
from jax.experimental import pallas as pl
from jax.experimental.pallas import tpu as pltpu

F32 = jnp.float32
BF16 = jnp.bfloat16

D_MODEL = 1024
D_FF = 2816
EPS = 1e-6
NEG = -1e30

M_HEADS = 8
M_DV = 128
M_DQK = 64

N_QH = 16
N_KVH = 4
GROUP = N_QH // N_KVH
HEAD_DIM = 64
CMP_BLOCK = 32
CMP_STRIDE = 16
CMP_HIDDEN = 256
SEL_BLOCK = 64
SEL_TOPK = 16
WINDOW = 512
FORCE_SCORE = 1e4
LOG2E = 1.4426950408889634
SEL_BIAS = -2.0 ** 100

LANES = 128
VMEM_LIMIT = 56 * 1024 * 1024

TOKEN_TILE = 512
FFN_CHUNK = 256
MLSTM_CHUNK = 256
Q_TILE = 256
SEL_KEY_CHUNK = 512

_NT = (((1,), (1,)), ((), ()))


def _params(*sem):
    return pltpu.CompilerParams(dimension_semantics=sem, vmem_limit_bytes=VMEM_LIMIT)


def _rms(x, g):
    ms = jnp.mean(x * x, axis=-1, keepdims=True)
    return (x * lax.rsqrt(ms + EPS)) * g


def _dot(a, b):
    return jnp.dot(a, b, preferred_element_type=F32)


def _dot_nt(a, b, precision=None):
    return lax.dot_general(a, b, _NT, precision=precision, preferred_element_type=F32)


def _resident(shape):
    nd = len(shape)
    return pl.BlockSpec(shape, lambda *_: (0,) * nd)


def _ffn_kernel(x_ref, g_ref, win_ref, wout_ref, o_ref, xn_ref, acc_ref):
    f = wout_ref.shape[0]
    x = x_ref[...]
    xn_ref[...] = _rms(x, g_ref[...]).astype(BF16)
    for c in range(f // FFN_CHUNK):
        cols = slice(c * FFN_CHUNK, (c + 1) * FFN_CHUNK)
        up_cols = slice(f + c * FFN_CHUNK, f + (c + 1) * FFN_CHUNK)
        xn = xn_ref[...]
        a = _dot(xn, win_ref[:, cols])
        b = _dot(xn, win_ref[:, up_cols])
        hid = ((a * jax.nn.sigmoid(a)) * b).astype(BF16)
        part = _dot(hid, wout_ref[cols, :])
        if c == 0:
            acc_ref[...] = part
        else:
            acc_ref[...] += part
    o_ref[...] = x_ref[...] + 0.5 * acc_ref[...]


def ffn(x2, g, w_in_all, w_out_all, layer, j):
    n, d = x2.shape
    f = w_out_all.shape[2]
    assert f % FFN_CHUNK == 0
    tm = TOKEN_TILE
    return pl.pallas_call(
        _ffn_kernel,
        grid=(n // tm,),
        in_specs=[
            pl.BlockSpec((tm, d), lambda i: (i, 0)),
            _resident((1, d)),
            pl.BlockSpec((None, None, d, 2 * f), lambda i: (layer, j, 0, 0)),
            pl.BlockSpec((None, None, f, d), lambda i: (layer, j, 0, 0)),
        ],
        out_specs=pl.BlockSpec((tm, d), lambda i: (i, 0)),
        out_shape=jax.ShapeDtypeStruct((n, d), F32),
        scratch_shapes=[pltpu.VMEM((tm, d), BF16), pltpu.VMEM((tm, d), F32)],
        compiler_params=_params("parallel"),
        name="ffn",
    )(x2, g.reshape(1, d), w_in_all, w_out_all)


def _mlstm_proj_kernel(x_ref, g_ref, wqk_ref, wv_ref, wog_ref, wg_ref, bg_ref,
                       qk_ref, v_ref, og_ref, gate_ref):
    xn = _rms(x_ref[...], g_ref[...]).astype(BF16)
    qk_ref[...] = _dot(xn, wqk_ref[...]).astype(BF16)
    v_ref[...] = _dot(xn, wv_ref[...]).astype(BF16)
    og_ref[...] = _dot(xn, wog_ref[...])
    gate_ref[...] = _dot(xn, wg_ref[...]) + bg_ref[...]


def mlstm_proj(h2, g, w_in, b_if):
    n, d = h2.shape
    qk_w, v_w = M_HEADS * M_DQK, M_HEADS * M_DV
    wb = w_in.astype(BF16)
    wqk = wb[:, :2 * qk_w]
    wv = wb[:, 2 * qk_w:2 * qk_w + v_w]
    wgate = jnp.pad(wb[:, 2 * qk_w + v_w:2 * qk_w + v_w + 2 * M_HEADS], ((0, 0), (0, LANES - 2 * M_HEADS)))
    wog = wb[:, 2 * qk_w + v_w + 2 * M_HEADS:]
    bg = jnp.pad(b_if.astype(F32).reshape(1, 2 * M_HEADS), ((0, 0), (0, LANES - 2 * M_HEADS)))
    tm = TOKEN_TILE
    tile = lambda w: pl.BlockSpec((tm, w), lambda i: (i, 0))
    return pl.pallas_call(
        _mlstm_proj_kernel,
        grid=(n // tm,),
        in_specs=[tile(d), _resident((1, d)), _resident(wqk.shape), _resident(wv.shape),
                  _resident(wog.shape), _resident(wgate.shape), _resident(bg.shape)],
        out_specs=[tile(2 * qk_w), tile(v_w), tile(d), tile(LANES)],
        out_shape=[jax.ShapeDtypeStruct((n, 2 * qk_w), BF16), jax.ShapeDtypeStruct((n, v_w), BF16),
                   jax.ShapeDtypeStruct((n, d), F32), jax.ShapeDtypeStruct((n, LANES), F32)],
        compiler_params=_params("parallel"),
        name="mlstm_proj",
    )(h2, g.reshape(1, d), wqk, wv, wog, wgate, bg)


def _log_sigmoid(x):
    return jnp.minimum(x, 0.0) - jnp.log1p(jnp.exp(-jnp.abs(x)))


def _mlstm_chunk_kernel(q_ref, k_ref, v_ref, og_ref, gate_ref, ghead_ref, o_ref, c_ref, g_r_ref, b_r_ref, cm_r_ref):
    pair = pl.program_id(1)
    L = MLSTM_CHUNK
    n_chunks = q_ref.shape[1] // L

    r_i = lax.broadcasted_iota(jnp.int32, (L, L), 0)
    c_i = lax.broadcasted_iota(jnp.int32, (L, L), 1)
    causal = c_i <= r_i
    tril = causal.astype(F32)
    lane = lax.broadcasted_iota(jnp.int32, (L, LANES), 1)
    ones_blk = jnp.ones((L, LANES), BF16)

    c_ref[...] = jnp.zeros_like(c_ref)

    t_i = lax.broadcasted_iota(jnp.int32, (n_chunks, L), 1)
    for hh in range(2):
        b = _dot_nt(_log_sigmoid(gate_ref[0, 0, 1, hh]), tril, lax.Precision.HIGHEST)
        g = gate_ref[0, 0, 0, hh] - b
        cm = g
        shift = 1
        while shift < L:
            cm = jnp.maximum(cm, jnp.where(t_i >= shift, pltpu.roll(cm, shift, 1), -jnp.inf))
            shift *= 2
        b_r_ref[hh] = b
        g_r_ref[hh] = g
        cm_r_ref[hh] = cm

    def column(row):
        return jnp.broadcast_to(row, (LANES, L)).T

    def body(c, m_prev):
        base = pl.multiple_of(c * L, L)
        q2 = q_ref[0, pl.ds(base, L), :]
        k2 = k_ref[0, pl.ds(base, L), :]
        m_next = []
        for hh in range(2):
            head = 2 * pair + hh
            in_head = (lane >= hh * M_DQK) & (lane < (hh + 1) * M_DQK)
            qh = jnp.where(in_head, q2, 0).astype(F32)
            qh = (qh * (M_DQK ** -0.5)).astype(BF16)
            kh = jnp.where(in_head, k2, 0)
            v_aug = jnp.concatenate([v_ref[0, pl.ds(base, L), hh * M_DV:(hh + 1) * M_DV], ones_blk], axis=1)

            g_r = g_r_ref[hh, pl.ds(c, 1), :]
            b_r = b_r_ref[hh, pl.ds(c, 1), :]
            g_tot = b_r[:, L - 1:L]
            m_p = m_prev[hh]
            c_prev = c_ref[hh]

            a_r = g_tot + g_r
            m_loc = jnp.max(a_r, axis=1, keepdims=True)
            kw_t = (kh.astype(F32).T * jnp.exp(a_r - m_loc)).astype(BF16)
            c_loc = _dot(kw_t, v_aug)

            mm = jnp.maximum(m_p, column(cm_r_ref[hh, pl.ds(c, 1), :]))
            w = jnp.where(causal, jnp.exp(g_r - jnp.concatenate([mm, mm], axis=1)), 0.0) * _dot_nt(qh, kh)
            num = jnp.concatenate([jnp.exp(m_p - mm)] * 2, axis=1) * _dot(qh, c_prev.astype(BF16)) \
                + _dot(w.astype(BF16), v_aug)
            floor = jnp.exp(-(column(b_r) + mm))
            h_out = num[:, :M_DV] / jnp.maximum(jnp.abs(num[:, M_DV:]), floor)
            h_out = _rms(h_out, ghead_ref[pl.ds(head, 1), :])
            cols = slice(hh * M_DV, (hh + 1) * M_DV)
            o_ref[0, pl.ds(base, L), cols] = (h_out * jax.nn.sigmoid(og_ref[0, pl.ds(base, L), cols])).astype(BF16)

            m_new = jnp.maximum(g_tot + m_p, m_loc)
            c_ref[hh] = jnp.exp(g_tot + m_p - m_new) * c_prev + jnp.exp(m_loc - m_new) * c_loc
            m_next.append(m_new)
        return tuple(m_next)

    zero = jnp.zeros((1, 1), F32)
    lax.fori_loop(0, n_chunks, body, (zero, zero))


def mlstm_chunk(qk, v, og, gates_t, g_head):
    b, t, _ = v.shape
    L = MLSTM_CHUNK
    nc = t // L
    pairs = M_HEADS // 2
    return pl.pallas_call(
        _mlstm_chunk_kernel,
        grid=(b, pairs),
        in_specs=[
            pl.BlockSpec((1, t, LANES), lambda i, p: (i, 0, p)),
            pl.BlockSpec((1, t, LANES), lambda i, p: (i, 0, pairs + p)),
            pl.BlockSpec((1, t, 2 * M_DV), lambda i, p: (i, 0, p)),
            pl.BlockSpec((1, t, 2 * M_DV), lambda i, p: (i, 0, p)),
            pl.BlockSpec((1, 1, 2, 2, nc, L), lambda i, p: (i, p, 0, 0, 0, 0)),
            _resident((M_HEADS, M_DV)),
        ],
        out_specs=pl.BlockSpec((1, t, 2 * M_DV), lambda i, p: (i, 0, p)),
        out_shape=jax.ShapeDtypeStruct((b, t, M_HEADS * M_DV), BF16),
        scratch_shapes=[pltpu.VMEM((2, 2 * M_DQK, 2 * M_DV), F32),
                        pltpu.VMEM((2, nc, L), F32), pltpu.VMEM((2, nc, L), F32), pltpu.VMEM((2, nc, L), F32)],
        compiler_params=_params("parallel", "parallel"),
        name="mlstm_chunk",
    )(qk, qk, v, og, gates_t, g_head.astype(F32))


def _resid_matmul_kernel(h_ref, a_ref, w_ref, o_ref):
    o_ref[...] = h_ref[...] + _dot(a_ref[...], w_ref[...])


def resid_matmul(h2, a2, w):
    n, d = h2.shape
    k = a2.shape[1]
    tm = TOKEN_TILE
    tile = lambda wd: pl.BlockSpec((tm, wd), lambda i: (i, 0))
    return pl.pallas_call(
        _resid_matmul_kernel,
        grid=(n // tm,),
        in_specs=[tile(d), tile(k), _resident((k, d))],
        out_specs=tile(d),
        out_shape=jax.ShapeDtypeStruct((n, d), F32),
        compiler_params=_params("parallel"),
        name="resid_matmul",
    )(h2, a2, w.astype(BF16))


def _kv_packs(y, gain, extra):
    lane = lax.broadcasted_iota(jnp.int32, (1, LANES), 1)
    lo = lane < HEAD_DIM
    kp, vp = [], []
    for h in range(N_KVH):
        yh = y[:, h * LANES:(h + 1) * LANES]
        ms = jnp.sum(jnp.where(lo, yh * yh, 0.0), axis=-1, keepdims=True) * (1.0 / HEAD_DIM)
        kp.append(jnp.where(lo, (yh * lax.rsqrt(ms + EPS)) * gain, extra))
        vp.append(jnp.where(lo, 1.0, yh))
    return jnp.concatenate(kp, axis=1), jnp.concatenate(vp, axis=1)


def _kv_proj_kernel(x_ref, g_ref, wc_ref, ws_ref, ww_ref, gs_ref, gw_ref,
                    cmp_ref, selk_ref, selv_ref, wink_ref, winv_ref, y_ref, *, seq_len):
    tm = x_ref.shape[0]
    xn = _rms(x_ref[...], g_ref[...]).astype(BF16)

    y = _dot(xn, wc_ref[...])
    n_slab = y.shape[1] // LANES
    for s in range(n_slab):
        y_ref[s] = y[:, s * LANES:(s + 1) * LANES]
    groups = tm // CMP_STRIDE
    lo = lax.broadcasted_iota(jnp.int32, (1, LANES), 1) < HEAD_DIM
    for s in range(n_slab):
        for rp in range(CMP_STRIDE // 2):
            even = y_ref[s, pl.ds(2 * rp, groups, stride=CMP_STRIDE), :]
            odd = y_ref[s, pl.ds(2 * rp + 1, groups, stride=CMP_STRIDE), :]
            cols = slice(rp * LANES, (rp + 1) * LANES)
            cmp_ref[2 * s, :, cols] = jnp.where(lo, even, pltpu.roll(odd, HEAD_DIM, 1))
            cmp_ref[2 * s + 1, :, cols] = jnp.where(lo, pltpu.roll(even, HEAD_DIM, 1), odd)

    pos = (pl.program_id(0) * tm) % seq_len + lax.broadcasted_iota(jnp.int32, (tm, 1), 0)
    lane = lax.broadcasted_iota(jnp.int32, (1, LANES), 1)
    block_bias = jnp.where(lane == HEAD_DIM + lax.shift_right_logical(pos, 6), SEL_BIAS, 0.0)
    kp, vp = _kv_packs(_dot(xn, ws_ref[...]), gs_ref[...], block_bias)
    selk_ref[...] = kp.astype(BF16)
    selv_ref[...] = vp.astype(BF16)
    kp, vp = _kv_packs(_dot(xn, ww_ref[...]), gw_ref[...], 0.0)
    wink_ref[...] = kp.astype(BF16)
    winv_ref[...] = vp.astype(BF16)


def _kv_pack_weights(wk, wv):
    d = wk.shape[0]
    wk = wk.reshape(d, N_KVH, HEAD_DIM)
    wv = wv.reshape(d, N_KVH, HEAD_DIM)
    return jnp.concatenate([wk, wv], axis=-1).reshape(d, N_KVH * LANES)


def _k_gain_row(gain):
    return jnp.concatenate([gain.astype(F32), jnp.ones((HEAD_DIM,), F32)]).reshape(1, LANES)


def kv_proj(h2, t, g, kv_w, k_norm):
    n, d = h2.shape
    wb = kv_w.astype(BF16)
    kc, vc, ks, vs, kw, vw = jnp.split(wb, 6, axis=1)
    wcmp = jnp.concatenate([kc, vc], axis=1)
    wsel = _kv_pack_weights(ks, vs)
    wwin = _kv_pack_weights(kw, vw)
    wd = N_KVH * LANES
    tm = TOKEN_TILE
    assert t % tm == 0 and t // SEL_BLOCK <= LANES - HEAD_DIM
    tile = lambda w: pl.BlockSpec((tm, w), lambda i: (i, 0))
    packed = jax.ShapeDtypeStruct((n, wd), BF16)
    feat = CMP_STRIDE * HEAD_DIM
    groups = tm // CMP_STRIDE
    return pl.pallas_call(
        functools.partial(_kv_proj_kernel, seq_len=t),
        grid=(n // tm,),
        in_specs=[tile(d), _resident((1, d)), _resident(wcmp.shape), _resident(wsel.shape),
                  _resident(wwin.shape), _resident((1, LANES)), _resident((1, LANES))],
        out_specs=[pl.BlockSpec((2 * N_KVH, groups, feat), lambda i: (0, i, 0))] + [tile(wd)] * 4,
        out_shape=[jax.ShapeDtypeStruct((2 * N_KVH, n // CMP_STRIDE, feat), F32), packed, packed, packed, packed],
        scratch_shapes=[pltpu.VMEM((2 * N_KVH * HEAD_DIM // LANES, tm, LANES), F32)],
        compiler_params=_params("parallel"),
        name="kv_proj",
    )(h2, g.reshape(1, d), wcmp, wsel, wwin, _k_gain_row(k_norm[1]), _k_gain_row(k_norm[2]))


def _cmp_kernel(xk_ref, xv_ref, pe_ref, w1_ref, w2k_ref, w2v_ref, gk_ref, ok_ref, ov_ref):
    def compress(x, kv, w2):
        u = _dot((x + pe_ref[kv, 0]).astype(BF16), w1_ref[kv, 0])
        v = _dot((x + pe_ref[kv, 1]).astype(BF16), w1_ref[kv, 1])
        nrow = v.shape[0]
        hid = u + pltpu.roll(v, nrow - 1, 0)
        return _dot((hid * jax.nn.sigmoid(hid)).astype(BF16), w2)

    yk = compress(xk_ref[0], 0, w2k_ref[...])
    yv = compress(xv_ref[0], 1, w2v_ref[...])
    ms = jnp.sum(yk * yk, axis=-1, keepdims=True) * (1.0 / HEAD_DIM)
    ok_ref[0, 0] = ((yk * lax.rsqrt(ms + EPS)) * gk_ref[...]).astype(BF16)
    lane = lax.broadcasted_iota(jnp.int32, (1, LANES), 1)
    ov_ref[0, 0] = jnp.where(lane < HEAD_DIM, 1.0, yv).astype(BF16)


def cmp_kv(cmp_x, b, t, cmp_pe, cmp_w1, cmp_w2, k_gain):
    ng = t // CMP_STRIDE
    feat = CMP_STRIDE * HEAD_DIM
    pe = cmp_pe.astype(F32).reshape(2, 2, 1, feat)
    w1 = cmp_w1.astype(BF16).reshape(2, 2, feat, CMP_HIDDEN)
    zeros = jnp.zeros((CMP_HIDDEN, HEAD_DIM), BF16)
    w2k = jnp.concatenate([cmp_w2[0].astype(BF16), zeros], axis=1)
    w2v = jnp.concatenate([zeros, cmp_w2[1].astype(BF16)], axis=1)
    ospec = pl.BlockSpec((1, 1, ng, LANES), lambda i, h: (i, h, 0, 0))
    packed = jax.ShapeDtypeStruct((b, N_KVH, ng, LANES), BF16)
    return pl.pallas_call(
        _cmp_kernel,
        grid=(b, N_KVH),
        in_specs=[pl.BlockSpec((1, ng, feat), lambda i, h: (h, i, 0)),
                  pl.BlockSpec((1, ng, feat), lambda i, h: (N_KVH + h, i, 0)),
                  _resident(pe.shape), _resident(w1.shape), _resident(w2k.shape),
                  _resident(w2v.shape), _resident((1, LANES))],
        out_specs=[ospec, ospec],
        out_shape=[packed, packed],
        compiler_params=_params("parallel", "parallel"),
        name="cmp_kv",
    )(cmp_x, cmp_x, pe, w1, w2k, w2v, _k_gain_row(k_gain))


def _nsa_proj_kernel(x_ref, g_ref, wq_ref, wg_ref, qg_ref, q_ref, gate_ref):
    xn = _rms(x_ref[...], g_ref[...]).astype(BF16)
    q = _dot(xn, wq_ref[...])
    lane = lax.broadcasted_iota(jnp.int32, (1, LANES), 1)
    lo = lane < HEAD_DIM
    outs = []
    for cb in range(q.shape[1] // LANES):
        y = q[:, cb * LANES:(cb + 1) * LANES]
        y2 = y * y
        ms_lo = jnp.sum(jnp.where(lo, y2, 0.0), axis=-1, keepdims=True) * (1.0 / HEAD_DIM)
        ms_hi = jnp.sum(jnp.where(lo, 0.0, y2), axis=-1, keepdims=True) * (1.0 / HEAD_DIM)
        scale = jnp.where(lo, lax.rsqrt(ms_lo + EPS), lax.rsqrt(ms_hi + EPS))
        outs.append(((y * scale) * qg_ref[...]) * (HEAD_DIM ** -0.5 * LOG2E))
    q_ref[...] = jnp.concatenate(outs, axis=1).astype(BF16)
    gate_ref[...] = jax.nn.sigmoid(_dot(xn, wg_ref[...]))


def nsa_proj(h2, g, w_in, q_norm):
    n, d = h2.shape
    wb = w_in.astype(BF16)
    wq = wb[:, :N_QH * HEAD_DIM]
    wg = wb[:, N_QH * HEAD_DIM:].reshape(d, 3, N_KVH, GROUP).transpose(0, 2, 1, 3).reshape(d, N_KVH, 3 * GROUP)
    wg = jnp.pad(wg, ((0, 0), (0, 0), (0, LANES - 3 * GROUP))).reshape(d, N_KVH * LANES)
    qg = jnp.tile(q_norm.astype(F32), LANES // HEAD_DIM).reshape(1, LANES)
    tm = TOKEN_TILE
    tile = lambda w: pl.BlockSpec((tm, w), lambda i: (i, 0))
    return pl.pallas_call(
        _nsa_proj_kernel,
        grid=(n // tm,),
        in_specs=[tile(d), _resident((1, d)), _resident(wq.shape), _resident(wg.shape), _resident((1, LANES))],
        out_specs=[tile(N_QH * HEAD_DIM), tile(N_KVH * LANES)],
        out_shape=[jax.ShapeDtypeStruct((n, N_QH * HEAD_DIM), BF16),
                   jax.ShapeDtypeStruct((n, N_KVH * LANES), F32)],
        compiler_params=_params("parallel"),
        name="nsa_proj",
    )(h2, g.reshape(1, d), wq, wg, qg)


def _masked_exp2(s3, mask):
    sm = jnp.where(mask[None], s3, NEG)
    mx = jnp.max(sm, axis=-1, keepdims=True)
    return jnp.exp2(sm - mx), mx > 0.5 * NEG


def _normalise(acc, values_lo):
    lo = lax.broadcasted_iota(jnp.int32, acc.shape, 1) < HEAD_DIM
    return acc / jnp.where(lo == values_lo, pltpu.roll(acc, HEAD_DIM, 1), 1.0)


def _nsa_attn_kernel(q_ref, gate_ref, cmpk_ref, cmpv_ref, selk_ref, selv_ref, wink_ref, winv_ref,
                     ov_ref, wbias_ref, spread_ref, o_ref, s_ref, mx_ref, acc_ref, *, tk):
    tq = q_ref.shape[1]
    rows = GROUP * tq
    s0 = pl.program_id(2) * tq

    lane = lax.broadcasted_iota(jnp.int32, (tq, LANES), 1)
    lo = lane < HEAD_DIM
    qs = []
    for g in range(GROUP):
        blk = q_ref[0, :, (g // 2) * LANES:(g // 2 + 1) * LANES].astype(F32)
        if g % 2 == 1:
            blk = pltpu.roll(blk, HEAD_DIM, 1)
        qs.append(jnp.where(lo, blk, 0.0))
    q4 = jnp.concatenate(qs, axis=0)
    qst = q4.astype(BF16)
    t_q = s0 + lax.broadcasted_iota(jnp.int32, (tq, 1), 0)

    kc = cmpk_ref[0, 0]
    n_cmp = kc.shape[0]
    cmp_end = lax.broadcasted_iota(jnp.int32, (1, n_cmp), 1) * CMP_STRIDE + (CMP_BLOCK - 1)
    e_c, ok_c = _masked_exp2(_dot_nt(qst, kc).reshape(GROUP, tq, n_cmp), cmp_end <= t_q)
    e_c = e_c.reshape(rows, n_cmp)
    ok_c = ok_c.reshape(rows, 1)
    e_hi = e_c.astype(BF16)
    e_lo = (e_c - e_hi.astype(F32)).astype(BF16)
    both = _dot(e_hi, jnp.concatenate([cmpv_ref[0, 0], ov_ref[...]], axis=1))
    o_c = jnp.where(ok_c, _normalise(both[:, :LANES], False), 0.0)

    imp4 = jnp.where(ok_c, _normalise(both[:, LANES:] + _dot(e_lo, ov_ref[...]), True), 0.0)
    imp4 = imp4.reshape(GROUP, tq, LANES)
    imp = imp4[0] + imp4[1] + imp4[2] + imp4[3]

    n_sel = selk_ref.shape[1] // SEL_BLOCK
    valid = (lane * SEL_BLOCK <= t_q) & (lane < n_sel)
    cur = lax.shift_right_logical(t_q, 6)
    forced = (lane == 0) | (lane == cur) | (lane == cur - 1)
    score = jnp.where(forced & valid, FORCE_SCORE, jnp.where(valid, imp, -1.0))
    score = jnp.where(lane < n_sel, score, -2.0)
    s_t = score.T[:HEAD_DIM]
    sub = lax.broadcasted_iota(jnp.int32, (8, tq), 0)
    groups = [s_t[8 * v:8 * v + 8] for v in range(HEAD_DIM // 8)]
    ranks = [jnp.zeros((8, tq), F32) for _ in groups]
    for i in range(n_sel):
        s_i = s_t[i:i + 1, :]
        for v, s_v in enumerate(groups):
            if 8 * v + 7 < i:
                ahead = s_i > s_v
            elif 8 * v > i:
                ahead = s_i >= s_v
            else:
                ahead = (s_i > s_v) | ((s_i == s_v) & (sub > i - 8 * v))
            ranks[v] = ranks[v] + jnp.where(ahead, 1.0, 0.0)
    rank = jnp.concatenate(ranks, axis=0)
    blk_t = lax.broadcasted_iota(jnp.int32, (HEAD_DIM, tq), 0)
    t_row = s0 + lax.broadcasted_iota(jnp.int32, (1, tq), 1)
    dropped_t = jnp.where((rank < SEL_TOPK) & (blk_t * SEL_BLOCK <= t_row), 0.0, 1.0)
    dropped = jnp.concatenate([jnp.zeros((HEAD_DIM, tq), F32), dropped_t], axis=0).T
    qx = (q4 + jnp.concatenate([dropped] * GROUP, axis=0)).astype(BF16)

    wk = WINDOW + tq
    w0 = pl.multiple_of(jnp.maximum(s0 - WINDOW, 0), tq)
    s_w = _dot_nt(qst, wink_ref[0, pl.ds(w0, wk), :]).reshape(GROUP, tq, wk)
    s_w = s_w + wbias_ref[jnp.minimum(pl.program_id(2), WINDOW // tq)][None]
    e_w = jnp.exp2(s_w - jnp.max(s_w, axis=-1, keepdims=True))
    o_w = _normalise(_dot(e_w.reshape(rows, wk).astype(BF16), winv_ref[0, pl.ds(w0, wk), :]), False)

    nblk = tk // LANES
    last = s0 // tk
    mx_ref[...] = jnp.full(mx_ref.shape, NEG, F32)

    def scores(c, causal):
        base = pl.multiple_of(c * tk, tk)
        s = _dot_nt(qx, selk_ref[0, pl.ds(base, tk), :])
        if causal:
            kpos = base + lax.broadcasted_iota(jnp.int32, (1, tk), 1)
            s = jnp.where((kpos <= t_q)[None], s.reshape(GROUP, tq, tk), NEG).reshape(rows, tk)
        s_ref[c] = s
        m = mx_ref[...]
        for j in range(nblk):
            m = jnp.maximum(m, s[:, j * LANES:(j + 1) * LANES])
        mx_ref[...] = m

    def pass1(c, carry):
        scores(c, causal=False)
        return carry

    lax.fori_loop(0, last, pass1, 0)
    scores(last, causal=True)
    mx_ref[...] = jnp.broadcast_to(jnp.max(mx_ref[...], axis=-1, keepdims=True), mx_ref.shape)

    acc_ref[...] = jnp.zeros_like(acc_ref)

    def pass2(c, carry):
        base = pl.multiple_of(c * tk, tk)
        s = s_ref[c]
        m = mx_ref[...]
        p = jnp.concatenate([jnp.exp2(s[:, j * LANES:(j + 1) * LANES] - m) for j in range(nblk)], axis=1)
        acc_ref[...] += _dot(p.astype(BF16), selv_ref[0, pl.ds(base, tk), :])
        return carry

    lax.fori_loop(0, last + 1, pass2, 0)
    o_s = _normalise(acc_ref[...], False)

    gt = gate_ref[0]
    g_hi = gt.astype(BF16)
    g_lo = (gt - g_hi.astype(F32)).astype(BF16)
    g_rep = _dot(g_hi, spread_ref[...]) + _dot(g_lo, spread_ref[...])

    def gate(branch):
        first = branch * GROUP
        return jnp.concatenate([g_rep[:, (first + g) * LANES:(first + g + 1) * LANES] for g in range(GROUP)], axis=0)

    o = (gate(0) * o_c + gate(1) * o_s + gate(2) * o_w).reshape(GROUP, tq, LANES)
    out01 = jnp.where(lo, pltpu.roll(o[0], HEAD_DIM, 1), o[1])
    out23 = jnp.where(lo, pltpu.roll(o[2], HEAD_DIM, 1), o[3])
    o_ref[0] = jnp.concatenate([out01, out23], axis=1).astype(BF16)


def _overlap_matrix(n_cmp_rows, t):
    n_sel = t // SEL_BLOCK
    c0 = np.arange(n_cmp_rows) * CMP_STRIDE
    s_0 = np.arange(n_sel) * SEL_BLOCK
    ov = (c0[:, None] < s_0[None, :] + SEL_BLOCK) & (c0[:, None] + CMP_BLOCK > s_0[None, :])
    ov[(t - CMP_BLOCK) // CMP_STRIDE + 1:] = False
    out = np.zeros((n_cmp_rows, LANES), np.float32)
    out[:, :n_sel] = ov
    out[:, HEAD_DIM:] = 1.0
    return jnp.asarray(out, BF16)


def _window_bias(tq):
    off = np.arange(WINDOW // tq + 1).reshape(-1, 1, 1) * tq
    i = np.arange(tq).reshape(1, tq, 1)
    j = np.arange(WINDOW + tq).reshape(1, 1, -1)
    visible = (j <= off + i) & (j > off + i - WINDOW)
    return jnp.asarray(np.where(visible, 0.0, NEG).astype(np.float32))


def _gate_spread():
    n_gate = 3 * GROUP
    src = np.arange(LANES).reshape(LANES, 1)
    dst_group = np.arange(n_gate * LANES).reshape(1, -1) // LANES
    return jnp.asarray((src == dst_group).astype(np.float32), BF16)


def nsa_attn(q, gates, kv):
    cmpk, cmpv, selk, selv, wink, winv = kv
    b, t, _ = q.shape
    tq = Q_TILE
    tk = min(SEL_KEY_CHUNK, t)
    n_cmp_rows = cmpk.shape[2]
    assert t // SEL_BLOCK <= LANES - HEAD_DIM and (t // SEL_BLOCK) % 8 == 0
    assert t % tk == 0 and tk % tq == 0 and t >= WINDOW + tq
    assert WINDOW % tq == 0
    ov = _overlap_matrix(n_cmp_rows, t)
    wbias = _window_bias(tq)
    spread = _gate_spread()
    gw = GROUP * HEAD_DIM
    rows = GROUP * tq
    kv_spec = pl.BlockSpec((1, t, LANES), lambda i, h, j: (i, 0, h))
    cmp_spec = pl.BlockSpec((1, 1, n_cmp_rows, LANES), lambda i, h, j: (i, h, 0, 0))
    return pl.pallas_call(
        functools.partial(_nsa_attn_kernel, tk=tk),
        grid=(b, N_KVH, t // tq),
        in_specs=[
            pl.BlockSpec((1, tq, gw), lambda i, h, j: (i, j, h)),
            pl.BlockSpec((1, tq, LANES), lambda i, h, j: (i, j, h)),
            cmp_spec, cmp_spec, kv_spec, kv_spec, kv_spec, kv_spec,
            _resident(ov.shape), _resident(wbias.shape), _resident(spread.shape),
        ],
        out_specs=pl.BlockSpec((1, tq, gw), lambda i, h, j: (i, j, h)),
        out_shape=jax.ShapeDtypeStruct((b, t, N_QH * HEAD_DIM), BF16),
        scratch_shapes=[pltpu.VMEM((t // tk, rows, tk), F32), pltpu.VMEM((rows, LANES), F32),
                        pltpu.VMEM((rows, LANES), F32)],
        compiler_params=_params("parallel", "parallel", "arbitrary"),
        name="nsa_attn",
    )(q, gates, cmpk, cmpv, selk, selv, wink, winv, ov, wbias, spread)


def mlstm_layer(h2, b, t, g, w_in, b_if, g_head, w_out):
    qk, v, og, gates = mlstm_proj(h2, g, w_in, b_if)
    L = MLSTM_CHUNK
    gates_t = gates[:, :2 * M_HEADS].reshape(b, t // L, L, 2, M_HEADS // 2, 2).transpose(0, 4, 3, 5, 1, 2)
    hn = mlstm_chunk(qk.reshape(b, t, -1), v.reshape(b, t, -1), og.reshape(b, t, -1), gates_t, g_head)
    return resid_matmul(h2, hn.reshape(b * t, -1), w_out)


def nsa_shared_kv(h2, b, t, kv_norm, kv_w, cmp_pe, cmp_w1, cmp_w2, k_norm):
    cmp_x, *packs = kv_proj(h2, t, kv_norm, kv_w, k_norm)
    cmpk, cmpv = cmp_kv(cmp_x, b, t, cmp_pe, cmp_w1, cmp_w2, k_norm[0])
    return (cmpk, cmpv) + tuple(p.reshape(b, t, -1) for p in packs)


def nsa_layer(h2, b, t, kv, g, w_in, q_norm, w_out):
    q, gates = nsa_proj(h2, g, w_in, q_norm)
    o = nsa_attn(q.reshape(b, t, -1), gates.reshape(b, t, -1), kv)
    return resid_matmul(h2, o.reshape(b * t, -1), w_out)


def kernel(x, ffn_norm, ffn_w_in, ffn_w_out, mix_norm, a_w_in, a_b_if, a_g_head, a_w_out, kv_norm, kv_w, cmp_pe, cmp_w1, cmp_w2, k_norm, b_w_in, b_q_norm, b_w_out):
    b, t, d = x.shape
    depth = ffn_norm.shape[0]
    n_a = a_w_in.shape[0]
    h = x.reshape(b * t, d)
    w_in_bf, w_out_bf = ffn_w_in.astype(BF16), ffn_w_out.astype(BF16)
    kv = None
    for layer in range(depth):
        h = ffn(h, ffn_norm[layer, 0], w_in_bf, w_out_bf, layer, 0)
        if layer < n_a:
            h = mlstm_layer(h, b, t, mix_norm[layer], a_w_in[layer], a_b_if[layer], a_g_head[layer], a_w_out[layer])
        else:
            j = layer - n_a
            h = nsa_layer(h, b, t, kv, mix_norm[layer], b_w_in[j], b_q_norm[j], b_w_out[j])
        h = ffn(h, ffn_norm[layer, 1], w_in_bf, w_out_bf, layer, 1)
        if layer == n_a - 1:
            kv = nsa_shared_kv(h, b, t, kv_norm, kv_w, cmp_pe, cmp_w1, cmp_w2, k_norm)
    return h.reshape(b, t, d)
```

```python
import functools

import jax
import jax.numpy as jnp
import numpy as np
from jax import lax
from jax.experimental import pallas as pl
from jax.experimental.pallas import tpu as pltpu

F32 = jnp.float32
BF16 = jnp.bfloat16

D_MODEL = 1024
D_FF = 2816
EPS = 1e-6
NEG = -1e30

M_HEADS = 8
M_DV = 128
M_DQK = 64

N_QH = 16
N_KVH = 4
GROUP = N_QH // N_KVH
HEAD_DIM = 64
CMP_BLOCK = 32
CMP_STRIDE = 16
CMP_HIDDEN = 256
SEL_BLOCK = 64
SEL_TOPK = 16
WINDOW = 512
FORCE_SCORE = 1e4
LOG2E = 1.4426950408889634
SEL_BIAS = -2.0 ** 100

LANES = 128
VMEM_LIMIT = 56 * 1024 * 1024

TOKEN_TILE = 512
FFN_CHUNK = 256
MLSTM_CHUNK = 256
Q_TILE = 256
SEL_KEY_CHUNK = 512

_NT = (((1,), (1,)), ((), ()))


def _params(*sem):
    return pltpu.CompilerParams(dimension_semantics=sem, vmem_limit_bytes=VMEM_LIMIT)


def _rms(x, g):
    ms = jnp.mean(x * x, axis=-1, keepdims=True)
    return (x * lax.rsqrt(ms + EPS)) * g


def _dot(a, b):
    return jnp.dot(a, b, preferred_element_type=F32)


def _dot_nt(a, b, precision=None):
    return lax.dot_general(a, b, _NT, precision=precision, preferred_element_type=F32)


def _resident(shape):
    nd = len(shape)
    return pl.BlockSpec(shape, lambda *_: (0,) * nd)


def _ffn_kernel(*refs, mixer_proj):
    if mixer_proj:
        x_ref, a_ref, wmix_ref, g_ref, win_ref, wout_ref, o_ref, xn_ref, acc_ref, x_scr = refs
        x_scr[...] = x_ref[...] + _dot(a_ref[...], wmix_ref[...])
        x_ref = x_scr
    else:
        x_ref, g_ref, win_ref, wout_ref, o_ref, xn_ref, acc_ref = refs
    f = wout_ref.shape[0]
    x = x_ref[...]
    xn_ref[...] = _rms(x, g_ref[...]).astype(BF16)
    for c in range(f // FFN_CHUNK):
        cols = slice(c * FFN_CHUNK, (c + 1) * FFN_CHUNK)
        up_cols = slice(f + c * FFN_CHUNK, f + (c + 1) * FFN_CHUNK)
        xn = xn_ref[...]
        a = _dot(xn, win_ref[:, cols])
        b = _dot(xn, win_ref[:, up_cols])
        hid = ((a * jax.nn.sigmoid(a)) * b).astype(BF16)
        part = _dot(hid, wout_ref[cols, :])
        if c == 0:
            acc_ref[...] = part
        else:
            acc_ref[...] += part
    o_ref[...] = x_ref[...] + 0.5 * acc_ref[...]


def ffn(x2, g, w_in_all, w_out_all, layer, j, mixer=None):
    n, d = x2.shape
    f = w_out_all.shape[2]
    assert f % FFN_CHUNK == 0
    tm = TOKEN_TILE
    tile = lambda w: pl.BlockSpec((tm, w), lambda i: (i, 0))
    ins, specs, scratch = [x2], [tile(d)], [pltpu.VMEM((tm, d), BF16), pltpu.VMEM((tm, d), F32)]
    if mixer is not None:
        a2, w_mix = mixer
        ins += [a2, w_mix.astype(BF16)]
        specs += [tile(a2.shape[1]), _resident(w_mix.shape)]
        scratch.append(pltpu.VMEM((tm, d), F32))
    return pl.pallas_call(
        functools.partial(_ffn_kernel, mixer_proj=mixer is not None),
        grid=(n // tm,),
        in_specs=specs + [
            _resident((1, d)),
            pl.BlockSpec((None, None, d, 2 * f), lambda i: (layer, j, 0, 0)),
            pl.BlockSpec((None, None, f, d), lambda i: (layer, j, 0, 0)),
        ],
        out_specs=tile(d),
        out_shape=jax.ShapeDtypeStruct((n, d), F32),
        scratch_shapes=scratch,
        compiler_params=_params("parallel"),
        name="ffn",
    )(*ins, g.reshape(1, d), w_in_all, w_out_all)


def _mlstm_proj_kernel(x_ref, g_ref, wqk_ref, wv_ref, wog_ref, wg_ref, bg_ref,
                       qk_ref, v_ref, og_ref, gate_ref):
    xn = _rms(x_ref[...], g_ref[...]).astype(BF16)
    qk_ref[...] = _dot(xn, wqk_ref[...]).astype(BF16)
    v_ref[...] = _dot(xn, wv_ref[...]).astype(BF16)
    og_ref[...] = _dot(xn, wog_ref[...])
    gate_ref[...] = _dot(xn, wg_ref[...]) + bg_ref[...]


def mlstm_proj(h2, g, w_in, b_if):
    n, d = h2.shape
    qk_w, v_w = M_HEADS * M_DQK, M_HEADS * M_DV
    wb = w_in.astype(BF16)
    wqk = wb[:, :2 * qk_w]
    wv = wb[:, 2 * qk_w:2 * qk_w + v_w]
    wgate = jnp.pad(wb[:, 2 * qk_w + v_w:2 * qk_w + v_w + 2 * M_HEADS], ((0, 0), (0, LANES - 2 * M_HEADS)))
    wog = wb[:, 2 * qk_w + v_w + 2 * M_HEADS:]
    bg = jnp.pad(b_if.astype(F32).reshape(1, 2 * M_HEADS), ((0, 0), (0, LANES - 2 * M_HEADS)))
    tm = TOKEN_TILE
    tile = lambda w: pl.BlockSpec((tm, w), lambda i: (i, 0))
    return pl.pallas_call(
        _mlstm_proj_kernel,
        grid=(n // tm,),
        in_specs=[tile(d), _resident((1, d)), _resident(wqk.shape), _resident(wv.shape),
                  _resident(wog.shape), _resident(wgate.shape), _resident(bg.shape)],
        out_specs=[tile(2 * qk_w), tile(v_w), tile(d), tile(LANES)],
        out_shape=[jax.ShapeDtypeStruct((n, 2 * qk_w), BF16), jax.ShapeDtypeStruct((n, v_w), BF16),
                   jax.ShapeDtypeStruct((n, d), F32), jax.ShapeDtypeStruct((n, LANES), F32)],
        compiler_params=_params("parallel"),
        name="mlstm_proj",
    )(h2, g.reshape(1, d), wqk, wv, wog, wgate, bg)


def _log_sigmoid(x):
    return jnp.minimum(x, 0.0) - jnp.log1p(jnp.exp(-jnp.abs(x)))


def _mlstm_chunk_kernel(q_ref, k_ref, v_ref, og_ref, gate_ref, ghead_ref, o_ref, c_ref, g_r_ref, b_r_ref, cm_r_ref):
    pair = pl.program_id(1)
    L = MLSTM_CHUNK
    n_chunks = q_ref.shape[1] // L

    r_i = lax.broadcasted_iota(jnp.int32, (L, L), 0)
    c_i = lax.broadcasted_iota(jnp.int32, (L, L), 1)
    causal = c_i <= r_i
    tril = causal.astype(F32)
    lane = lax.broadcasted_iota(jnp.int32, (L, LANES), 1)
    ones_blk = jnp.ones((L, LANES), BF16)

    c_ref[...] = jnp.zeros_like(c_ref)

    t_i = lax.broadcasted_iota(jnp.int32, (n_chunks, L), 1)
    for hh in range(2):
        b = _dot_nt(_log_sigmoid(gate_ref[0, 0, 1, hh]), tril, lax.Precision.HIGHEST)
        g = gate_ref[0, 0, 0, hh] - b
        cm = g
        shift = 1
        while shift < L:
            cm = jnp.maximum(cm, jnp.where(t_i >= shift, pltpu.roll(cm, shift, 1), -jnp.inf))
            shift *= 2
        b_r_ref[hh] = b
        g_r_ref[hh] = g
        cm_r_ref[hh] = cm

    def column(row):
        return jnp.broadcast_to(row, (LANES, L)).T

    def body(c, m_prev):
        base = pl.multiple_of(c * L, L)
        q2 = q_ref[0, pl.ds(base, L), :]
        k2 = k_ref[0, pl.ds(base, L), :]
        m_next = []
        for hh in range(2):
            head = 2 * pair + hh
            in_head = (lane >= hh * M_DQK) & (lane < (hh + 1) * M_DQK)
            qh = jnp.where(in_head, q2, 0).astype(F32)
            qh = (qh * (M_DQK ** -0.5)).astype(BF16)
            kh = jnp.where(in_head, k2, 0)
            v_aug = jnp.concatenate([v_ref[0, pl.ds(base, L), hh * M_DV:(hh + 1) * M_DV], ones_blk], axis=1)

            g_r = g_r_ref[hh, pl.ds(c, 1), :]
            b_r = b_r_ref[hh, pl.ds(c, 1), :]
            g_tot = b_r[:, L - 1:L]
            m_p = m_prev[hh]
            c_prev = c_ref[hh]

            a_r = g_tot + g_r
            m_loc = jnp.max(a_r, axis=1, keepdims=True)
            kw_t = (kh.astype(F32).T * jnp.exp(a_r - m_loc)).astype(BF16)
            c_loc = _dot(kw_t, v_aug)

            mm = jnp.maximum(m_p, column(cm_r_ref[hh, pl.ds(c, 1), :]))
            w = jnp.where(causal, jnp.exp(g_r - jnp.concatenate([mm, mm], axis=1)), 0.0) * _dot_nt(qh, kh)
            num = jnp.concatenate([jnp.exp(m_p - mm)] * 2, axis=1) * _dot(qh, c_prev.astype(BF16)) \
                + _dot(w.astype(BF16), v_aug)
            floor = jnp.exp(-(column(b_r) + mm))
            h_out = num[:, :M_DV] / jnp.maximum(jnp.abs(num[:, M_DV:]), floor)
            h_out = _rms(h_out, ghead_ref[pl.ds(head, 1), :])
            cols = slice(hh * M_DV, (hh + 1) * M_DV)
            o_ref[0, pl.ds(base, L), cols] = (h_out * jax.nn.sigmoid(og_ref[0, pl.ds(base, L), cols])).astype(BF16)

            m_new = jnp.maximum(g_tot + m_p, m_loc)
            c_ref[hh] = jnp.exp(g_tot + m_p - m_new) * c_prev + jnp.exp(m_loc - m_new) * c_loc
            m_next.append(m_new)
        return tuple(m_next)

    zero = jnp.zeros((1, 1), F32)
    lax.fori_loop(0, n_chunks, body, (zero, zero))


def mlstm_chunk(qk, v, og, gates_t, g_head):
    b, t, _ = v.shape
    L = MLSTM_CHUNK
    nc = t // L
    pairs = M_HEADS // 2
    return pl.pallas_call(
        _mlstm_chunk_kernel,
        grid=(b, pairs),
        in_specs=[
            pl.BlockSpec((1, t, LANES), lambda i, p: (i, 0, p)),
            pl.BlockSpec((1, t, LANES), lambda i, p: (i, 0, pairs + p)),
            pl.BlockSpec((1, t, 2 * M_DV), lambda i, p: (i, 0, p)),
            pl.BlockSpec((1, t, 2 * M_DV), lambda i, p: (i, 0, p)),
            pl.BlockSpec((1, 1, 2, 2, nc, L), lambda i, p: (i, p, 0, 0, 0, 0)),
            _resident((M_HEADS, M_DV)),
        ],
        out_specs=pl.BlockSpec((1, t, 2 * M_DV), lambda i, p: (i, 0, p)),
        out_shape=jax.ShapeDtypeStruct((b, t, M_HEADS * M_DV), BF16),
        scratch_shapes=[pltpu.VMEM((2, 2 * M_DQK, 2 * M_DV), F32),
                        pltpu.VMEM((2, nc, L), F32), pltpu.VMEM((2, nc, L), F32), pltpu.VMEM((2, nc, L), F32)],
        compiler_params=_params("parallel", "parallel"),
        name="mlstm_chunk",
    )(qk, qk, v, og, gates_t, g_head.astype(F32))


def _kv_packs(y, gain, extra):
    lane = lax.broadcasted_iota(jnp.int32, (1, LANES), 1)
    lo = lane < HEAD_DIM
    kp, vp = [], []
    for h in range(N_KVH):
        yh = y[:, h * LANES:(h + 1) * LANES]
        ms = jnp.sum(jnp.where(lo, yh * yh, 0.0), axis=-1, keepdims=True) * (1.0 / HEAD_DIM)
        kp.append(jnp.where(lo, (yh * lax.rsqrt(ms + EPS)) * gain, extra))
        vp.append(jnp.where(lo, 1.0, yh))
    return jnp.concatenate(kp, axis=1), jnp.concatenate(vp, axis=1)


def _kv_proj_kernel(x_ref, g_ref, wc_ref, ws_ref, ww_ref, gs_ref, gw_ref,
                    cmp_ref, selk_ref, selv_ref, wink_ref, winv_ref, y_ref, *, seq_len):
    tm = x_ref.shape[0]
    xn = _rms(x_ref[...], g_ref[...]).astype(BF16)

    y = _dot(xn, wc_ref[...])
    n_slab = y.shape[1] // LANES
    for s in range(n_slab):
        y_ref[s] = y[:, s * LANES:(s + 1) * LANES]
    groups = tm // CMP_STRIDE
    lo = lax.broadcasted_iota(jnp.int32, (1, LANES), 1) < HEAD_DIM
    for s in range(n_slab):
        for rp in range(CMP_STRIDE // 2):
            even = y_ref[s, pl.ds(2 * rp, groups, stride=CMP_STRIDE), :]
            odd = y_ref[s, pl.ds(2 * rp + 1, groups, stride=CMP_STRIDE), :]
            cols = slice(rp * LANES, (rp + 1) * LANES)
            cmp_ref[2 * s, :, cols] = jnp.where(lo, even, pltpu.roll(odd, HEAD_DIM, 1))
            cmp_ref[2 * s + 1, :, cols] = jnp.where(lo, pltpu.roll(even, HEAD_DIM, 1), odd)

    pos = (pl.program_id(0) * tm) % seq_len + lax.broadcasted_iota(jnp.int32, (tm, 1), 0)
    lane = lax.broadcasted_iota(jnp.int32, (1, LANES), 1)
    block_bias = jnp.where(lane == HEAD_DIM + lax.shift_right_logical(pos, 6), SEL_BIAS, 0.0)
    kp, vp = _kv_packs(_dot(xn, ws_ref[...]), gs_ref[...], block_bias)
    selk_ref[...] = kp.astype(BF16)
    selv_ref[...] = vp.astype(BF16)
    kp, vp = _kv_packs(_dot(xn, ww_ref[...]), gw_ref[...], 0.0)
    wink_ref[...] = kp.astype(BF16)
    winv_ref[...] = vp.astype(BF16)


def _kv_pack_weights(wk, wv):
    d = wk.shape[0]
    wk = wk.reshape(d, N_KVH, HEAD_DIM)
    wv = wv.reshape(d, N_KVH, HEAD_DIM)
    return jnp.concatenate([wk, wv], axis=-1).reshape(d, N_KVH * LANES)


def _k_gain_row(gain):
    return jnp.concatenate([gain.astype(F32), jnp.ones((HEAD_DIM,), F32)]).reshape(1, LANES)


def kv_proj(h2, t, g, kv_w, k_norm):
    n, d = h2.shape
    wb = kv_w.astype(BF16)
    kc, vc, ks, vs, kw, vw = jnp.split(wb, 6, axis=1)
    wcmp = jnp.concatenate([kc, vc], axis=1)
    wsel = _kv_pack_weights(ks, vs)
    wwin = _kv_pack_weights(kw, vw)
    wd = N_KVH * LANES
    tm = TOKEN_TILE
    assert t % tm == 0 and t // SEL_BLOCK <= LANES - HEAD_DIM
    tile = lambda w: pl.BlockSpec((tm, w), lambda i: (i, 0))
    packed = jax.ShapeDtypeStruct((n, wd), BF16)
    feat = CMP_STRIDE * HEAD_DIM
    groups = tm // CMP_STRIDE
    return pl.pallas_call(
        functools.partial(_kv_proj_kernel, seq_len=t),
        grid=(n // tm,),
        in_specs=[tile(d), _resident((1, d)), _resident(wcmp.shape), _resident(wsel.shape),
                  _resident(wwin.shape), _resident((1, LANES)), _resident((1, LANES))],
        out_specs=[pl.BlockSpec((2 * N_KVH, groups, feat), lambda i: (0, i, 0))] + [tile(wd)] * 4,
        out_shape=[jax.ShapeDtypeStruct((2 * N_KVH, n // CMP_STRIDE, feat), F32), packed, packed, packed, packed],
        scratch_shapes=[pltpu.VMEM((2 * N_KVH * HEAD_DIM // LANES, tm, LANES), F32)],
        compiler_params=_params("parallel"),
        name="kv_proj",
    )(h2, g.reshape(1, d), wcmp, wsel, wwin, _k_gain_row(k_norm[1]), _k_gain_row(k_norm[2]))


def _cmp_kernel(xk_ref, xv_ref, pe_ref, w1_ref, w2k_ref, w2v_ref, gk_ref, ok_ref, ov_ref):
    def compress(x, kv, w2):
        u = _dot((x + pe_ref[kv, 0]).astype(BF16), w1_ref[kv, 0])
        v = _dot((x + pe_ref[kv, 1]).astype(BF16), w1_ref[kv, 1])
        nrow = v.shape[0]
        hid = u + pltpu.roll(v, nrow - 1, 0)
        return _dot((hid * jax.nn.sigmoid(hid)).astype(BF16), w2)

    yk = compress(xk_ref[0], 0, w2k_ref[...])
    yv = compress(xv_ref[0], 1, w2v_ref[...])
    ms = jnp.sum(yk * yk, axis=-1, keepdims=True) * (1.0 / HEAD_DIM)
    ok_ref[0, 0] = ((yk * lax.rsqrt(ms + EPS)) * gk_ref[...]).astype(BF16)
    lane = lax.broadcasted_iota(jnp.int32, (1, LANES), 1)
    ov_ref[0, 0] = jnp.where(lane < HEAD_DIM, 1.0, yv).astype(BF16)


def cmp_kv(cmp_x, b, t, cmp_pe, cmp_w1, cmp_w2, k_gain):
    ng = t // CMP_STRIDE
    feat = CMP_STRIDE * HEAD_DIM
    pe = cmp_pe.astype(F32).reshape(2, 2, 1, feat)
    w1 = cmp_w1.astype(BF16).reshape(2, 2, feat, CMP_HIDDEN)
    zeros = jnp.zeros((CMP_HIDDEN, HEAD_DIM), BF16)
    w2k = jnp.concatenate([cmp_w2[0].astype(BF16), zeros], axis=1)
    w2v = jnp.concatenate([zeros, cmp_w2[1].astype(BF16)], axis=1)
    ospec = pl.BlockSpec((1, 1, ng, LANES), lambda i, h: (i, h, 0, 0))
    packed = jax.ShapeDtypeStruct((b, N_KVH, ng, LANES), BF16)
    return pl.pallas_call(
        _cmp_kernel,
        grid=(b, N_KVH),
        in_specs=[pl.BlockSpec((1, ng, feat), lambda i, h: (h, i, 0)),
                  pl.BlockSpec((1, ng, feat), lambda i, h: (N_KVH + h, i, 0)),
                  _resident(pe.shape), _resident(w1.shape), _resident(w2k.shape),
                  _resident(w2v.shape), _resident((1, LANES))],
        out_specs=[ospec, ospec],
        out_shape=[packed, packed],
        compiler_params=_params("parallel", "parallel"),
        name="cmp_kv",
    )(cmp_x, cmp_x, pe, w1, w2k, w2v, _k_gain_row(k_gain))


def _nsa_proj_kernel(x_ref, g_ref, wq_ref, wg_ref, qg_ref, q_ref, gate_ref):
    xn = _rms(x_ref[...], g_ref[...]).astype(BF16)
    q = _dot(xn, wq_ref[...])
    lane = lax.broadcasted_iota(jnp.int32, (1, LANES), 1)
    lo = lane < HEAD_DIM
    outs = []
    for cb in range(q.shape[1] // LANES):
        y = q[:, cb * LANES:(cb + 1) * LANES]
        y2 = y * y
        ms_lo = jnp.sum(jnp.where(lo, y2, 0.0), axis=-1, keepdims=True) * (1.0 / HEAD_DIM)
        ms_hi = jnp.sum(jnp.where(lo, 0.0, y2), axis=-1, keepdims=True) * (1.0 / HEAD_DIM)
        scale = jnp.where(lo, lax.rsqrt(ms_lo + EPS), lax.rsqrt(ms_hi + EPS))
        outs.append(((y * scale) * qg_ref[...]) * (HEAD_DIM ** -0.5 * LOG2E))
    q_ref[...] = jnp.concatenate(outs, axis=1).astype(BF16)
    gate_ref[...] = jax.nn.sigmoid(_dot(xn, wg_ref[...])).astype(BF16)


def nsa_proj(h2, g, w_in, q_norm):
    n, d = h2.shape
    wb = w_in.astype(BF16)
    wq = wb[:, :N_QH * HEAD_DIM]
    wg = wb[:, N_QH * HEAD_DIM:].reshape(d, 3, N_KVH, GROUP).transpose(0, 2, 1, 3).reshape(d, N_KVH, 3 * GROUP)
    wg = jnp.pad(wg, ((0, 0), (0, 0), (0, LANES - 3 * GROUP))).reshape(d, N_KVH * LANES)
    qg = jnp.tile(q_norm.astype(F32), LANES // HEAD_DIM).reshape(1, LANES)
    tm = TOKEN_TILE
    tile = lambda w: pl.BlockSpec((tm, w), lambda i: (i, 0))
    return pl.pallas_call(
        _nsa_proj_kernel,
        grid=(n // tm,),
        in_specs=[tile(d), _resident((1, d)), _resident(wq.shape), _resident(wg.shape), _resident((1, LANES))],
        out_specs=[tile(N_QH * HEAD_DIM), tile(N_KVH * LANES)],
        out_shape=[jax.ShapeDtypeStruct((n, N_QH * HEAD_DIM), BF16),
                   jax.ShapeDtypeStruct((n, N_KVH * LANES), BF16)],
        compiler_params=_params("parallel"),
        name="nsa_proj",
    )(h2, g.reshape(1, d), wq, wg, qg)


def _masked_exp2(s3, mask):
    sm = jnp.where(mask[None], s3, NEG)
    mx = jnp.max(sm, axis=-1, keepdims=True)
    return jnp.exp2(sm - mx), mx > 0.5 * NEG


def _normalise(acc, values_lo):
    lo = lax.broadcasted_iota(jnp.int32, acc.shape, 1) < HEAD_DIM
    return acc / jnp.where(lo == values_lo, pltpu.roll(acc, HEAD_DIM, 1), 1.0)


def _nsa_attn_kernel(q_ref, gate_ref, cmpk_ref, cmpv_ref, selk_ref, selv_ref, wink_ref, winv_ref,
                     ov_ref, wbias_ref, spread_ref, o_ref, s_ref, mx_ref, acc_ref, *, tk):
    tq = q_ref.shape[1]
    rows = GROUP * tq
    s0 = pl.program_id(2) * tq

    lane = lax.broadcasted_iota(jnp.int32, (tq, LANES), 1)
    lo = lane < HEAD_DIM
    qs = []
    for g in range(GROUP):
        blk = q_ref[0, :, (g // 2) * LANES:(g // 2 + 1) * LANES].astype(F32)
        if g % 2 == 1:
            blk = pltpu.roll(blk, HEAD_DIM, 1)
        qs.append(jnp.where(lo, blk, 0.0))
    q4 = jnp.concatenate(qs, axis=0)
    qst = q4.astype(BF16)
    t_q = s0 + lax.broadcasted_iota(jnp.int32, (tq, 1), 0)

    kc = cmpk_ref[0, 0]
    n_cmp = kc.shape[0]
    cmp_end = lax.broadcasted_iota(jnp.int32, (1, n_cmp), 1) * CMP_STRIDE + (CMP_BLOCK - 1)
    e_c, ok_c = _masked_exp2(_dot_nt(qst, kc).reshape(GROUP, tq, n_cmp), cmp_end <= t_q)
    e_c = e_c.reshape(rows, n_cmp)
    ok_c = ok_c.reshape(rows, 1)
    e_hi = e_c.astype(BF16)
    e_lo = (e_c - e_hi.astype(F32)).astype(BF16)
    both = _dot(e_hi, jnp.concatenate([cmpv_ref[0, 0], ov_ref[...]], axis=1))
    o_c = jnp.where(ok_c, _normalise(both[:, :LANES], False), 0.0)

    imp4 = jnp.where(ok_c, _normalise(both[:, LANES:] + _dot(e_lo, ov_ref[...]), True), 0.0)
    imp4 = imp4.reshape(GROUP, tq, LANES)
    imp = imp4[0] + imp4[1] + imp4[2] + imp4[3]

    n_sel = selk_ref.shape[1] // SEL_BLOCK
    valid = (lane * SEL_BLOCK <= t_q) & (lane < n_sel)
    cur = lax.shift_right_logical(t_q, 6)
    forced = (lane == 0) | (lane == cur) | (lane == cur - 1)
    score = jnp.where(forced & valid, FORCE_SCORE, jnp.where(valid, imp, -1.0))
    score = jnp.where(lane < n_sel, score, -2.0)
    s_t = score.T[:HEAD_DIM]
    sub = lax.broadcasted_iota(jnp.int32, (8, tq), 0)
    groups = [s_t[8 * v:8 * v + 8] for v in range(HEAD_DIM // 8)]
    ranks = [jnp.zeros((8, tq), F32) for _ in groups]
    for i in range(n_sel):
        s_i = s_t[i:i + 1, :]
        for v, s_v in enumerate(groups):
            if 8 * v + 7 < i:
                ahead = s_i > s_v
            elif 8 * v > i:
                ahead = s_i >= s_v
            else:
                ahead = (s_i > s_v) | ((s_i == s_v) & (sub > i - 8 * v))
            ranks[v] = ranks[v] + jnp.where(ahead, 1.0, 0.0)
    rank = jnp.concatenate(ranks, axis=0)
    blk_t = lax.broadcasted_iota(jnp.int32, (HEAD_DIM, tq), 0)
    t_row = s0 + lax.broadcasted_iota(jnp.int32, (1, tq), 1)
    dropped_t = jnp.where((rank < SEL_TOPK) & (blk_t * SEL_BLOCK <= t_row), 0.0, 1.0)
    dropped = jnp.concatenate([jnp.zeros((HEAD_DIM, tq), F32), dropped_t], axis=0).T
    qx = (q4 + jnp.concatenate([dropped] * GROUP, axis=0)).astype(BF16)

    wk = WINDOW + tq
    w0 = pl.multiple_of(jnp.maximum(s0 - WINDOW, 0), tq)
    s_w = _dot_nt(qst, wink_ref[0, pl.ds(w0, wk), :]).reshape(GROUP, tq, wk)
    s_w = s_w + wbias_ref[jnp.minimum(pl.program_id(2), WINDOW // tq)][None]
    e_w = jnp.exp2(s_w - jnp.max(s_w, axis=-1, keepdims=True))
    o_w = _normalise(_dot(e_w.reshape(rows, wk).astype(BF16), winv_ref[0, pl.ds(w0, wk), :]), False)

    nblk = tk // LANES
    last = s0 // tk
    mx_ref[...] = jnp.full(mx_ref.shape, NEG, F32)

    def scores(c, causal):
        base = pl.multiple_of(c * tk, tk)
        s = _dot_nt(qx, selk_ref[0, pl.ds(base, tk), :])
        if causal:
            kpos = base + lax.broadcasted_iota(jnp.int32, (1, tk), 1)
            s = jnp.where((kpos <= t_q)[None], s.reshape(GROUP, tq, tk), NEG).reshape(rows, tk)
        s_ref[c] = s
        m = mx_ref[...]
        for j in range(nblk):
            m = jnp.maximum(m, s[:, j * LANES:(j + 1) * LANES])
        mx_ref[...] = m

    def pass1(i, carry):
        scores(2 * i, causal=False)
        scores(2 * i + 1, causal=False)
        return carry

    lax.fori_loop(0, last // 2, pass1, 0)

    @pl.when(last % 2 == 1)
    def _():
        scores(last - 1, causal=False)

    scores(last, causal=True)
    mx_ref[...] = jnp.broadcast_to(jnp.max(mx_ref[...], axis=-1, keepdims=True), mx_ref.shape)

    acc_ref[...] = jnp.zeros_like(acc_ref)

    def weighted(c):
        base = pl.multiple_of(c * tk, tk)
        s = s_ref[c]
        m = mx_ref[...]
        p = jnp.concatenate([jnp.exp2(s[:, j * LANES:(j + 1) * LANES] - m) for j in range(nblk)], axis=1)
        return _dot(p.astype(BF16), selv_ref[0, pl.ds(base, tk), :])

    def pass2(i, carry):
        acc_ref[...] += weighted(2 * i) + weighted(2 * i + 1)
        return carry

    lax.fori_loop(0, (last + 1) // 2, pass2, 0)

    @pl.when(last % 2 == 0)
    def _():
        acc_ref[...] += weighted(last)

    o_s = _normalise(acc_ref[...], False)

    g_rep = _dot(gate_ref[0], spread_ref[...])

    def gate(branch):
        first = branch * GROUP
        return jnp.concatenate([g_rep[:, (first + g) * LANES:(first + g + 1) * LANES] for g in range(GROUP)], axis=0)

    o = (gate(0) * o_c + gate(1) * o_s + gate(2) * o_w).reshape(GROUP, tq, LANES)
    out01 = jnp.where(lo, pltpu.roll(o[0], HEAD_DIM, 1), o[1])
    out23 = jnp.where(lo, pltpu.roll(o[2], HEAD_DIM, 1), o[3])
    o_ref[0] = jnp.concatenate([out01, out23], axis=1).astype(BF16)


def _overlap_matrix(n_cmp_rows, t):
    n_sel = t // SEL_BLOCK
    c0 = np.arange(n_cmp_rows) * CMP_STRIDE
    s_0 = np.arange(n_sel) * SEL_BLOCK
    ov = (c0[:, None] < s_0[None, :] + SEL_BLOCK) & (c0[:, None] + CMP_BLOCK > s_0[None, :])
    ov[(t - CMP_BLOCK) // CMP_STRIDE + 1:] = False
    out = np.zeros((n_cmp_rows, LANES), np.float32)
    out[:, :n_sel] = ov
    out[:, HEAD_DIM:] = 1.0
    return jnp.asarray(out, BF16)


def _window_bias(tq):
    off = np.arange(WINDOW // tq + 1).reshape(-1, 1, 1) * tq
    i = np.arange(tq).reshape(1, tq, 1)
    j = np.arange(WINDOW + tq).reshape(1, 1, -1)
    visible = (j <= off + i) & (j > off + i - WINDOW)
    return jnp.asarray(np.where(visible, 0.0, NEG).astype(np.float32))


def _gate_spread():
    n_gate = 3 * GROUP
    src = np.arange(LANES).reshape(LANES, 1)
    dst_group = np.arange(n_gate * LANES).reshape(1, -1) // LANES
    return jnp.asarray((src == dst_group).astype(np.float32), BF16)


def nsa_attn(q, gates, kv):
    cmpk, cmpv, selk, selv, wink, winv = kv
    b, t, _ = q.shape
    tq = Q_TILE
    tk = min(SEL_KEY_CHUNK, t)
    n_cmp_rows = cmpk.shape[2]
    assert t // SEL_BLOCK <= LANES - HEAD_DIM and (t // SEL_BLOCK) % 8 == 0
    assert t % tk == 0 and tk % tq == 0 and t >= WINDOW + tq
    assert WINDOW % tq == 0
    ov = _overlap_matrix(n_cmp_rows, t)
    wbias = _window_bias(tq)
    spread = _gate_spread()
    gw = GROUP * HEAD_DIM
    rows = GROUP * tq
    kv_spec = pl.BlockSpec((1, t, LANES), lambda i, h, j: (i, 0, h))
    cmp_spec = pl.BlockSpec((1, 1, n_cmp_rows, LANES), lambda i, h, j: (i, h, 0, 0))
    return pl.pallas_call(
        functools.partial(_nsa_attn_kernel, tk=tk),
        grid=(b, N_KVH, t // tq),
        in_specs=[
            pl.BlockSpec((1, tq, gw), lambda i, h, j: (i, j, h)),
            pl.BlockSpec((1, tq, LANES), lambda i, h, j: (i, j, h)),
            cmp_spec, cmp_spec, kv_spec, kv_spec, kv_spec, kv_spec,
            _resident(ov.shape), _resident(wbias.shape), _resident(spread.shape),
        ],
        out_specs=pl.BlockSpec((1, tq, gw), lambda i, h, j: (i, j, h)),
        out_shape=jax.ShapeDtypeStruct((b, t, N_QH * HEAD_DIM), BF16),
        scratch_shapes=[pltpu.VMEM((t // tk, rows, tk), F32), pltpu.VMEM((rows, LANES), F32),
                        pltpu.VMEM((rows, LANES), F32)],
        compiler_params=_params("parallel", "parallel", "arbitrary"),
        name="nsa_attn",
    )(q, gates, cmpk, cmpv, selk, selv, wink, winv, ov, wbias, spread)


def mlstm_mixer(h2, b, t, g, w_in, b_if, g_head):
    qk, v, og, gates = mlstm_proj(h2, g, w_in, b_if)
    L = MLSTM_CHUNK
    gates_t = gates[:, :2 * M_HEADS].reshape(b, t // L, L, 2, M_HEADS // 2, 2).transpose(0, 4, 3, 5, 1, 2)
    hn = mlstm_chunk(qk.reshape(b, t, -1), v.reshape(b, t, -1), og.reshape(b, t, -1), gates_t, g_head)
    return hn.reshape(b * t, -1)


def nsa_shared_kv(h2, b, t, kv_norm, kv_w, cmp_pe, cmp_w1, cmp_w2, k_norm):
    cmp_x, *packs = kv_proj(h2, t, kv_norm, kv_w, k_norm)
    cmpk, cmpv = cmp_kv(cmp_x, b, t, cmp_pe, cmp_w1, cmp_w2, k_norm[0])
    return (cmpk, cmpv) + tuple(p.reshape(b, t, -1) for p in packs)


def nsa_mixer(h2, b, t, kv, g, w_in, q_norm):
    q, gates = nsa_proj(h2, g, w_in, q_norm)
    o = nsa_attn(q.reshape(b, t, -1), gates.reshape(b, t, -1), kv)
    return o.reshape(b * t, -1)


def kernel(x, ffn_norm, ffn_w_in, ffn_w_out, mix_norm, a_w_in, a_b_if, a_g_head, a_w_out, kv_norm, kv_w, cmp_pe, cmp_w1, cmp_w2, k_norm, b_w_in, b_q_norm, b_w_out):
    b, t, d = x.shape
    depth = ffn_norm.shape[0]
    n_a = a_w_in.shape[0]
    h = x.reshape(b * t, d)
    w_in_bf, w_out_bf = ffn_w_in.astype(BF16), ffn_w_out.astype(BF16)
    kv = None
    for layer in range(depth):
        h = ffn(h, ffn_norm[layer, 0], w_in_bf, w_out_bf, layer, 0)
        if layer < n_a:
            mixed = mlstm_mixer(h, b, t, mix_norm[layer], a_w_in[layer], a_b_if[layer], a_g_head[layer])
            w_mix = a_w_out[layer]
        else:
            j = layer - n_a
            mixed = nsa_mixer(h, b, t, kv, mix_norm[layer], b_w_in[j], b_q_norm[j])
            w_mix = b_w_out[j]
        h = ffn(h, ffn_norm[layer, 1], w_in_bf, w_out_bf, layer, 1, mixer=(mixed, w_mix))
        if layer == n_a - 1:
            kv = nsa_shared_kv(h, b, t, kv_norm, kv_w, cmp_pe, cmp_w1, cmp_w2, k_norm)
    return h.reshape(b, t, d)
```

```python
import functools

import jax
import jax.numpy as jnp
import numpy as np
from jax import lax
from jax.experimental import pallas as pl
from jax.experimental.pallas import tpu as pltpu

F32 = jnp.float32
BF16 = jnp.bfloat16

D_MODEL = 1024
D_FF = 2816
EPS = 1e-6
NEG = -1e30

M_HEADS = 8
M_DV = 128
M_DQK = 64

N_QH = 16
N_KVH = 4
GROUP = N_QH // N_KVH
HEAD_DIM = 64
CMP_BLOCK = 32
CMP_STRIDE = 16
CMP_HIDDEN = 256
SEL_BLOCK = 64
SEL_TOPK = 16
WINDOW = 512
FORCE_SCORE = 1e4
LOG2E = 1.4426950408889634
SEL_BIAS = -2.0 ** 100

LANES = 128
VMEM_LIMIT = 56 * 1024 * 1024

TOKEN_TILE = 512
FFN_CHUNK = 256
MLSTM_CHUNK = 256
Q_TILE = 256
SEL_KEY_CHUNK = 512

_NT = (((1,), (1,)), ((), ()))


def _params(*sem):
    return pltpu.CompilerParams(dimension_semantics=sem, vmem_limit_bytes=VMEM_LIMIT)


def _rms(x, g):
    ms = jnp.mean(x * x, axis=-1, keepdims=True)
    return (x * lax.rsqrt(ms + EPS)) * g


def _dot(a, b):
    return jnp.dot(a, b, preferred_element_type=F32)


def _dot_nt(a, b, precision=None):
    return lax.dot_general(a, b, _NT, precision=precision, preferred_element_type=F32)


def _resident(shape):
    nd = len(shape)
    return pl.BlockSpec(shape, lambda *_: (0,) * nd)


def _ffn_kernel(*refs, mixer_proj):
    if mixer_proj:
        x_ref, a_ref, wmix_ref, g_ref, win_ref, wout_ref, o_ref, xn_ref, acc_ref, x_scr = refs
        x_scr[...] = x_ref[...] + _dot(a_ref[...], wmix_ref[...])
        x_ref = x_scr
    else:
        x_ref, g_ref, win_ref, wout_ref, o_ref, xn_ref, acc_ref = refs
    f = wout_ref.shape[0]
    x = x_ref[...]
    xn_ref[...] = _rms(x, g_ref[...]).astype(BF16)
    for c in range(f // FFN_CHUNK):
        cols = slice(c * FFN_CHUNK, (c + 1) * FFN_CHUNK)
        up_cols = slice(f + c * FFN_CHUNK, f + (c + 1) * FFN_CHUNK)
        xn = xn_ref[...]
        a = _dot(xn, win_ref[:, cols].astype(BF16))
        b = _dot(xn, win_ref[:, up_cols].astype(BF16))
        hid = ((a * jax.nn.sigmoid(a)) * b).astype(BF16)
        part = _dot(hid, wout_ref[cols, :].astype(BF16))
        if c == 0:
            acc_ref[...] = part
        else:
            acc_ref[...] += part
    o_ref[...] = x_ref[...] + 0.5 * acc_ref[...]


def ffn(x2, g, w_in_all, w_out_all, layer, j, mixer=None):
    n, d = x2.shape
    f = w_out_all.shape[2]
    assert f % FFN_CHUNK == 0
    tm = TOKEN_TILE
    tile = lambda w: pl.BlockSpec((tm, w), lambda i: (i, 0))
    ins, specs, scratch = [x2], [tile(d)], [pltpu.VMEM((tm, d), BF16), pltpu.VMEM((tm, d), F32)]
    if mixer is not None:
        a2, w_mix = mixer
        ins += [a2, w_mix.astype(BF16)]
        specs += [tile(a2.shape[1]), _resident(w_mix.shape)]
        scratch.append(pltpu.VMEM((tm, d), F32))
    return pl.pallas_call(
        functools.partial(_ffn_kernel, mixer_proj=mixer is not None),
        grid=(n // tm,),
        in_specs=specs + [
            _resident((1, d)),
            pl.BlockSpec((None, None, d, 2 * f), lambda i: (layer, j, 0, 0), pipeline_mode=pl.Buffered(1)),
            pl.BlockSpec((None, None, f, d), lambda i: (layer, j, 0, 0), pipeline_mode=pl.Buffered(1)),
        ],
        out_specs=tile(d),
        out_shape=jax.ShapeDtypeStruct((n, d), F32),
        scratch_shapes=scratch,
        compiler_params=_params("parallel"),
        name="ffn",
    )(*ins, g.reshape(1, d), w_in_all, w_out_all)


def _mlstm_proj_kernel(x_ref, g_ref, wqk_ref, wv_ref, wog_ref, wg_ref, bg_ref,
                       qk_ref, v_ref, og_ref, gate_ref):
    xn = _rms(x_ref[...], g_ref[...]).astype(BF16)
    qk_ref[...] = _dot(xn, wqk_ref[...]).astype(BF16)
    v_ref[...] = _dot(xn, wv_ref[...]).astype(BF16)
    og_ref[...] = _dot(xn, wog_ref[...])
    gate_ref[...] = _dot(xn, wg_ref[...]) + bg_ref[...]


def mlstm_proj(h2, g, w_in, b_if):
    n, d = h2.shape
    qk_w, v_w = M_HEADS * M_DQK, M_HEADS * M_DV
    wb = w_in.astype(BF16)
    wqk = wb[:, :2 * qk_w]
    wv = wb[:, 2 * qk_w:2 * qk_w + v_w]
    wgate = jnp.pad(wb[:, 2 * qk_w + v_w:2 * qk_w + v_w + 2 * M_HEADS], ((0, 0), (0, LANES - 2 * M_HEADS)))
    wog = wb[:, 2 * qk_w + v_w + 2 * M_HEADS:]
    bg = jnp.pad(b_if.astype(F32).reshape(1, 2 * M_HEADS), ((0, 0), (0, LANES - 2 * M_HEADS)))
    tm = TOKEN_TILE
    tile = lambda w: pl.BlockSpec((tm, w), lambda i: (i, 0))
    return pl.pallas_call(
        _mlstm_proj_kernel,
        grid=(n // tm,),
        in_specs=[tile(d), _resident((1, d)), _resident(wqk.shape), _resident(wv.shape),
                  _resident(wog.shape), _resident(wgate.shape), _resident(bg.shape)],
        out_specs=[tile(2 * qk_w), tile(v_w), tile(d), tile(LANES)],
        out_shape=[jax.ShapeDtypeStruct((n, 2 * qk_w), BF16), jax.ShapeDtypeStruct((n, v_w), BF16),
                   jax.ShapeDtypeStruct((n, d), F32), jax.ShapeDtypeStruct((n, LANES), F32)],
        compiler_params=_params("parallel"),
        name="mlstm_proj",
    )(h2, g.reshape(1, d), wqk, wv, wog, wgate, bg)


def _log_sigmoid(x):
    return jnp.minimum(x, 0.0) - jnp.log1p(jnp.exp(-jnp.abs(x)))


def _mlstm_chunk_kernel(q_ref, k_ref, v_ref, og_ref, gate_ref, ghead_ref, o_ref, c_ref, g_r_ref, b_r_ref, cm_r_ref):
    pair = pl.program_id(1)
    L = MLSTM_CHUNK
    n_chunks = q_ref.shape[1] // L

    r_i = lax.broadcasted_iota(jnp.int32, (L, L), 0)
    c_i = lax.broadcasted_iota(jnp.int32, (L, L), 1)
    causal = c_i <= r_i
    tril = causal.astype(F32)
    lane = lax.broadcasted_iota(jnp.int32, (L, LANES), 1)
    ones_blk = jnp.ones((L, LANES), BF16)

    c_ref[...] = jnp.zeros_like(c_ref)

    t_i = lax.broadcasted_iota(jnp.int32, (n_chunks, L), 1)
    for hh in range(2):
        b = _dot_nt(_log_sigmoid(gate_ref[0, 0, 1, hh]), tril, lax.Precision.HIGHEST)
        g = gate_ref[0, 0, 0, hh] - b
        cm = g
        shift = 1
        while shift < L:
            cm = jnp.maximum(cm, jnp.where(t_i >= shift, pltpu.roll(cm, shift, 1), -jnp.inf))
            shift *= 2
        b_r_ref[hh] = b
        g_r_ref[hh] = g
        cm_r_ref[hh] = cm

    def column(row):
        return jnp.broadcast_to(row, (LANES, L)).T

    def body(c, m_prev):
        base = pl.multiple_of(c * L, L)
        q2 = q_ref[0, pl.ds(base, L), :]
        k2 = k_ref[0, pl.ds(base, L), :]
        m_next = []
        for hh in range(2):
            head = 2 * pair + hh
            in_head = (lane >= hh * M_DQK) & (lane < (hh + 1) * M_DQK)
            qh = jnp.where(in_head, q2, 0).astype(F32)
            qh = (qh * (M_DQK ** -0.5)).astype(BF16)
            kh = jnp.where(in_head, k2, 0)
            v_aug = jnp.concatenate([v_ref[0, pl.ds(base, L), hh * M_DV:(hh + 1) * M_DV], ones_blk], axis=1)

            g_r = g_r_ref[hh, pl.ds(c, 1), :]
            b_r = b_r_ref[hh, pl.ds(c, 1), :]
            g_tot = b_r[:, L - 1:L]
            m_p = m_prev[hh]
            c_prev = c_ref[hh]

            a_r = g_tot + g_r
            m_loc = jnp.max(a_r, axis=1, keepdims=True)
            kw_t = (kh.astype(F32).T * jnp.exp(a_r - m_loc)).astype(BF16)
            c_loc = _dot(kw_t, v_aug)

            mm = jnp.maximum(m_p, column(cm_r_ref[hh, pl.ds(c, 1), :]))
            w = jnp.where(causal, jnp.exp(g_r - jnp.concatenate([mm, mm], axis=1)), 0.0) * _dot_nt(qh, kh)
            num = jnp.concatenate([jnp.exp(m_p - mm)] * 2, axis=1) * _dot(qh, c_prev.astype(BF16)) \
                + _dot(w.astype(BF16), v_aug)
            floor = jnp.exp(-(column(b_r) + mm))
            h_out = num[:, :M_DV] / jnp.maximum(jnp.abs(num[:, M_DV:]), floor)
            h_out = _rms(h_out, ghead_ref[pl.ds(head, 1), :])
            cols = slice(hh * M_DV, (hh + 1) * M_DV)
            o_ref[0, pl.ds(base, L), cols] = (h_out * jax.nn.sigmoid(og_ref[0, pl.ds(base, L), cols])).astype(BF16)

            m_new = jnp.maximum(g_tot + m_p, m_loc)
            c_ref[hh] = jnp.exp(g_tot + m_p - m_new) * c_prev + jnp.exp(m_loc - m_new) * c_loc
            m_next.append(m_new)
        return tuple(m_next)

    zero = jnp.zeros((1, 1), F32)
    lax.fori_loop(0, n_chunks, body, (zero, zero))


def mlstm_chunk(qk, v, og, gates_t, g_head):
    b, t, _ = v.shape
    L = MLSTM_CHUNK
    nc = t // L
    pairs = M_HEADS // 2
    return pl.pallas_call(
        _mlstm_chunk_kernel,
        grid=(b, pairs),
        in_specs=[
            pl.BlockSpec((1, t, LANES), lambda i, p: (i, 0, p)),
            pl.BlockSpec((1, t, LANES), lambda i, p: (i, 0, pairs + p)),
            pl.BlockSpec((1, t, 2 * M_DV), lambda i, p: (i, 0, p)),
            pl.BlockSpec((1, t, 2 * M_DV), lambda i, p: (i, 0, p)),
            pl.BlockSpec((1, 1, 2, 2, nc, L), lambda i, p: (i, p, 0, 0, 0, 0)),
            _resident((M_HEADS, M_DV)),
        ],
        out_specs=pl.BlockSpec((1, t, 2 * M_DV), lambda i, p: (i, 0, p)),
        out_shape=jax.ShapeDtypeStruct((b, t, M_HEADS * M_DV), BF16),
        scratch_shapes=[pltpu.VMEM((2, 2 * M_DQK, 2 * M_DV), F32),
                        pltpu.VMEM((2, nc, L), F32), pltpu.VMEM((2, nc, L), F32), pltpu.VMEM((2, nc, L), F32)],
        compiler_params=_params("parallel", "parallel"),
        name="mlstm_chunk",
    )(qk, qk, v, og, gates_t, g_head.astype(F32))


def _kv_packs(y, gain, extra):
    lane = lax.broadcasted_iota(jnp.int32, (1, LANES), 1)
    lo = lane < HEAD_DIM
    kp, vp = [], []
    for h in range(N_KVH):
        yh = y[:, h * LANES:(h + 1) * LANES]
        ms = jnp.sum(jnp.where(lo, yh * yh, 0.0), axis=-1, keepdims=True) * (1.0 / HEAD_DIM)
        kp.append(jnp.where(lo, (yh * lax.rsqrt(ms + EPS)) * gain, extra))
        vp.append(jnp.where(lo, 1.0, yh))
    return jnp.concatenate(kp, axis=1), jnp.concatenate(vp, axis=1)


def _kv_proj_kernel(x_ref, g_ref, wc_ref, ws_ref, ww_ref, gs_ref, gw_ref,
                    cmp_ref, selk_ref, selv_ref, wink_ref, winv_ref, y_ref, *, seq_len):
    tm = x_ref.shape[0]
    xn = _rms(x_ref[...], g_ref[...]).astype(BF16)

    y = _dot(xn, wc_ref[...])
    n_slab = y.shape[1] // LANES
    for s in range(n_slab):
        y_ref[s] = y[:, s * LANES:(s + 1) * LANES]
    groups = tm // CMP_STRIDE
    lo = lax.broadcasted_iota(jnp.int32, (1, LANES), 1) < HEAD_DIM
    for s in range(n_slab):
        for rp in range(CMP_STRIDE // 2):
            even = y_ref[s, pl.ds(2 * rp, groups, stride=CMP_STRIDE), :]
            odd = y_ref[s, pl.ds(2 * rp + 1, groups, stride=CMP_STRIDE), :]
            cols = slice(rp * LANES, (rp + 1) * LANES)
            cmp_ref[2 * s, :, cols] = jnp.where(lo, even, pltpu.roll(odd, HEAD_DIM, 1))
            cmp_ref[2 * s + 1, :, cols] = jnp.where(lo, pltpu.roll(even, HEAD_DIM, 1), odd)

    pos = (pl.program_id(0) * tm) % seq_len + lax.broadcasted_iota(jnp.int32, (tm, 1), 0)
    lane = lax.broadcasted_iota(jnp.int32, (1, LANES), 1)
    block_bias = jnp.where(lane == HEAD_DIM + lax.shift_right_logical(pos, 6), SEL_BIAS, 0.0)
    kp, vp = _kv_packs(_dot(xn, ws_ref[...]), gs_ref[...], block_bias)
    selk_ref[...] = kp.astype(BF16)
    selv_ref[...] = vp.astype(BF16)
    kp, vp = _kv_packs(_dot(xn, ww_ref[...]), gw_ref[...], 0.0)
    wink_ref[...] = kp.astype(BF16)
    winv_ref[...] = vp.astype(BF16)


def _kv_pack_weights(wk, wv):
    d = wk.shape[0]
    wk = wk.reshape(d, N_KVH, HEAD_DIM)
    wv = wv.reshape(d, N_KVH, HEAD_DIM)
    return jnp.concatenate([wk, wv], axis=-1).reshape(d, N_KVH * LANES)


def _k_gain_row(gain):
    return jnp.concatenate([gain.astype(F32), jnp.ones((HEAD_DIM,), F32)]).reshape(1, LANES)


def kv_proj(h2, t, g, kv_w, k_norm):
    n, d = h2.shape
    wb = kv_w.astype(BF16)
    kc, vc, ks, vs, kw, vw = jnp.split(wb, 6, axis=1)
    wcmp = jnp.concatenate([kc, vc], axis=1)
    wsel = _kv_pack_weights(ks, vs)
    wwin = _kv_pack_weights(kw, vw)
    wd = N_KVH * LANES
    tm = TOKEN_TILE
    assert t % tm == 0 and t // SEL_BLOCK <= LANES - HEAD_DIM
    tile = lambda w: pl.BlockSpec((tm, w), lambda i: (i, 0))
    packed = jax.ShapeDtypeStruct((n, wd), BF16)
    feat = CMP_STRIDE * HEAD_DIM
    groups = tm // CMP_STRIDE
    return pl.pallas_call(
        functools.partial(_kv_proj_kernel, seq_len=t),
        grid=(n // tm,),
        in_specs=[tile(d), _resident((1, d)), _resident(wcmp.shape), _resident(wsel.shape),
                  _resident(wwin.shape), _resident((1, LANES)), _resident((1, LANES))],
        out_specs=[pl.BlockSpec((2 * N_KVH, groups, feat), lambda i: (0, i, 0))] + [tile(wd)] * 4,
        out_shape=[jax.ShapeDtypeStruct((2 * N_KVH, n // CMP_STRIDE, feat), F32), packed, packed, packed, packed],
        scratch_shapes=[pltpu.VMEM((2 * N_KVH * HEAD_DIM // LANES, tm, LANES), F32)],
        compiler_params=_params("parallel"),
        name="kv_proj",
    )(h2, g.reshape(1, d), wcmp, wsel, wwin, _k_gain_row(k_norm[1]), _k_gain_row(k_norm[2]))


def _cmp_kernel(xk_ref, xv_ref, pe_ref, w1_ref, w2k_ref, w2v_ref, gk_ref, ok_ref, ov_ref):
    def compress(x, kv, w2):
        u = _dot((x + pe_ref[kv, 0]).astype(BF16), w1_ref[kv, 0])
        v = _dot((x + pe_ref[kv, 1]).astype(BF16), w1_ref[kv, 1])
        nrow = v.shape[0]
        hid = u + pltpu.roll(v, nrow - 1, 0)
        return _dot((hid * jax.nn.sigmoid(hid)).astype(BF16), w2)

    yk = compress(xk_ref[0], 0, w2k_ref[...])
    yv = compress(xv_ref[0], 1, w2v_ref[...])
    ms = jnp.sum(yk * yk, axis=-1, keepdims=True) * (1.0 / HEAD_DIM)
    ok_ref[0, 0] = ((yk * lax.rsqrt(ms + EPS)) * gk_ref[...]).astype(BF16)
    lane = lax.broadcasted_iota(jnp.int32, (1, LANES), 1)
    ov_ref[0, 0] = jnp.where(lane < HEAD_DIM, 1.0, yv).astype(BF16)


def cmp_kv(cmp_x, b, t, cmp_pe, cmp_w1, cmp_w2, k_gain):
    ng = t // CMP_STRIDE
    feat = CMP_STRIDE * HEAD_DIM
    pe = cmp_pe.astype(F32).reshape(2, 2, 1, feat)
    w1 = cmp_w1.astype(BF16).reshape(2, 2, feat, CMP_HIDDEN)
    zeros = jnp.zeros((CMP_HIDDEN, HEAD_DIM), BF16)
    w2k = jnp.concatenate([cmp_w2[0].astype(BF16), zeros], axis=1)
    w2v = jnp.concatenate([zeros, cmp_w2[1].astype(BF16)], axis=1)
    ospec = pl.BlockSpec((1, 1, ng, LANES), lambda i, h: (i, h, 0, 0))
    packed = jax.ShapeDtypeStruct((b, N_KVH, ng, LANES), BF16)
    return pl.pallas_call(
        _cmp_kernel,
        grid=(b, N_KVH),
        in_specs=[pl.BlockSpec((1, ng, feat), lambda i, h: (h, i, 0)),
                  pl.BlockSpec((1, ng, feat), lambda i, h: (N_KVH + h, i, 0)),
                  _resident(pe.shape), _resident(w1.shape), _resident(w2k.shape),
                  _resident(w2v.shape), _resident((1, LANES))],
        out_specs=[ospec, ospec],
        out_shape=[packed, packed],
        compiler_params=_params("parallel", "parallel"),
        name="cmp_kv",
    )(cmp_x, cmp_x, pe, w1, w2k, w2v, _k_gain_row(k_gain))


def _nsa_proj_kernel(x_ref, g_ref, wq_ref, wg_ref, qg_ref, q_ref, gate_ref):
    xn = _rms(x_ref[...], g_ref[...]).astype(BF16)
    q = _dot(xn, wq_ref[...])
    lane = lax.broadcasted_iota(jnp.int32, (1, LANES), 1)
    lo = lane < HEAD_DIM
    outs = []
    for cb in range(q.shape[1] // LANES):
        y = q[:, cb * LANES:(cb + 1) * LANES]
        y2 = y * y
        ms_lo = jnp.sum(jnp.where(lo, y2, 0.0), axis=-1, keepdims=True) * (1.0 / HEAD_DIM)
        ms_hi = jnp.sum(jnp.where(lo, 0.0, y2), axis=-1, keepdims=True) * (1.0 / HEAD_DIM)
        scale = jnp.where(lo, lax.rsqrt(ms_lo + EPS), lax.rsqrt(ms_hi + EPS))
        outs.append(((y * scale) * qg_ref[...]) * (HEAD_DIM ** -0.5 * LOG2E))
    q_ref[...] = jnp.concatenate(outs, axis=1).astype(BF16)
    gate_ref[...] = jax.nn.sigmoid(_dot(xn, wg_ref[...])).astype(BF16)


def nsa_proj(h2, g, w_in, q_norm):
    n, d = h2.shape
    wb = w_in.astype(BF16)
    wq = wb[:, :N_QH * HEAD_DIM]
    wg = wb[:, N_QH * HEAD_DIM:].reshape(d, 3, N_KVH, GROUP).transpose(0, 2, 1, 3).reshape(d, N_KVH, 3 * GROUP)
    wg = jnp.pad(wg, ((0, 0), (0, 0), (0, LANES - 3 * GROUP))).reshape(d, N_KVH * LANES)
    qg = jnp.tile(q_norm.astype(F32), LANES // HEAD_DIM).reshape(1, LANES)
    tm = TOKEN_TILE
    tile = lambda w: pl.BlockSpec((tm, w), lambda i: (i, 0))
    return pl.pallas_call(
        _nsa_proj_kernel,
        grid=(n // tm,),
        in_specs=[tile(d), _resident((1, d)), _resident(wq.shape), _resident(wg.shape), _resident((1, LANES))],
        out_specs=[tile(N_QH * HEAD_DIM), tile(N_KVH * LANES)],
        out_shape=[jax.ShapeDtypeStruct((n, N_QH * HEAD_DIM), BF16),
                   jax.ShapeDtypeStruct((n, N_KVH * LANES), BF16)],
        compiler_params=_params("parallel"),
        name="nsa_proj",
    )(h2, g.reshape(1, d), wq, wg, qg)


def _masked_exp2(s3, mask):
    sm = jnp.where(mask[None], s3, NEG)
    mx = jnp.max(sm, axis=-1, keepdims=True)
    return jnp.exp2(sm - mx), mx > 0.5 * NEG


def _normalise(acc, values_lo):
    lo = lax.broadcasted_iota(jnp.int32, acc.shape, 1) < HEAD_DIM
    return acc / jnp.where(lo == values_lo, pltpu.roll(acc, HEAD_DIM, 1), 1.0)


def _nsa_attn_kernel(q_ref, gate_ref, cmpk_ref, cmpv_ref, selk_ref, selv_ref, wink_ref, winv_ref,
                     ov_ref, wbias_ref, spread_ref, o_ref, s_ref, mx_ref, acc_ref, *, tk):
    tq = q_ref.shape[1]
    rows = GROUP * tq
    s0 = pl.program_id(2) * tq

    lane = lax.broadcasted_iota(jnp.int32, (tq, LANES), 1)
    lo = lane < HEAD_DIM
    qs = []
    for g in range(GROUP):
        blk = q_ref[0, :, (g // 2) * LANES:(g // 2 + 1) * LANES].astype(F32)
        if g % 2 == 1:
            blk = pltpu.roll(blk, HEAD_DIM, 1)
        qs.append(jnp.where(lo, blk, 0.0))
    q4 = jnp.concatenate(qs, axis=0)
    qst = q4.astype(BF16)
    t_q = s0 + lax.broadcasted_iota(jnp.int32, (tq, 1), 0)

    kc = cmpk_ref[0, 0]
    n_cmp = kc.shape[0]
    cmp_end = lax.broadcasted_iota(jnp.int32, (1, n_cmp), 1) * CMP_STRIDE + (CMP_BLOCK - 1)
    e_c, ok_c = _masked_exp2(_dot_nt(qst, kc).reshape(GROUP, tq, n_cmp), cmp_end <= t_q)
    e_c = e_c.reshape(rows, n_cmp)
    ok_c = ok_c.reshape(rows, 1)
    both = _dot(e_c.astype(BF16), jnp.concatenate([cmpv_ref[0, 0], ov_ref[...]], axis=1))
    o_c = jnp.where(ok_c, _normalise(both[:, :LANES], False), 0.0)

    imp4 = jnp.where(ok_c, _normalise(both[:, LANES:], True), 0.0).reshape(GROUP, tq, LANES)
    imp = imp4[0] + imp4[1] + imp4[2] + imp4[3]

    n_sel = selk_ref.shape[1] // SEL_BLOCK
    valid = (lane * SEL_BLOCK <= t_q) & (lane < n_sel)
    cur = lax.shift_right_logical(t_q, 6)
    forced = (lane == 0) | (lane == cur) | (lane == cur - 1)
    score = jnp.where(forced & valid, FORCE_SCORE, jnp.where(valid, imp, -1.0))
    score = jnp.where(lane < n_sel, score, -2.0)
    s_t = score.T[:HEAD_DIM]
    sub = lax.broadcasted_iota(jnp.int32, (8, tq), 0)
    groups = [s_t[8 * v:8 * v + 8] for v in range(HEAD_DIM // 8)]
    ranks = [jnp.zeros((8, tq), F32) for _ in groups]
    for i in range(n_sel):
        s_i = s_t[i:i + 1, :]
        for v, s_v in enumerate(groups):
            if 8 * v + 7 < i:
                ahead = s_i > s_v
            elif 8 * v > i:
                ahead = s_i >= s_v
            else:
                ahead = (s_i > s_v) | ((s_i == s_v) & (sub > i - 8 * v))
            ranks[v] = ranks[v] + jnp.where(ahead, 1.0, 0.0)
    rank = jnp.concatenate(ranks, axis=0)
    blk_t = lax.broadcasted_iota(jnp.int32, (HEAD_DIM, tq), 0)
    t_row = s0 + lax.broadcasted_iota(jnp.int32, (1, tq), 1)
    dropped_t = jnp.where((rank < SEL_TOPK) & (blk_t * SEL_BLOCK <= t_row), 0.0, 1.0)
    dropped = jnp.concatenate([jnp.zeros((HEAD_DIM, tq), F32), dropped_t], axis=0).T
    qx = (q4 + jnp.concatenate([dropped] * GROUP, axis=0)).astype(BF16)

    wk = WINDOW + tq
    w0 = pl.multiple_of(jnp.maximum(s0 - WINDOW, 0), tq)
    s_w = _dot_nt(qst, wink_ref[0, pl.ds(w0, wk), :]).reshape(GROUP, tq, wk)
    s_w = s_w + wbias_ref[jnp.minimum(pl.program_id(2), WINDOW // tq)][None]
    e_w = jnp.exp2(s_w - jnp.max(s_w, axis=-1, keepdims=True))
    o_w = _normalise(_dot(e_w.reshape(rows, wk).astype(BF16), winv_ref[0, pl.ds(w0, wk), :]), False)

    nblk = tk // LANES
    last = s0 // tk
    mx_ref[...] = jnp.full(mx_ref.shape, NEG, F32)

    def scores(c, causal):
        base = pl.multiple_of(c * tk, tk)
        s = _dot_nt(qx, selk_ref[0, pl.ds(base, tk), :])
        if causal:
            kpos = base + lax.broadcasted_iota(jnp.int32, (1, tk), 1)
            s = jnp.where((kpos <= t_q)[None], s.reshape(GROUP, tq, tk), NEG).reshape(rows, tk)
        s_ref[c] = s
        m = mx_ref[...]
        for j in range(nblk):
            m = jnp.maximum(m, s[:, j * LANES:(j + 1) * LANES])
        mx_ref[...] = m

    def pass1(i, carry):
        scores(2 * i, causal=False)
        scores(2 * i + 1, causal=False)
        return carry

    lax.fori_loop(0, last // 2, pass1, 0)

    @pl.when(last % 2 == 1)
    def _():
        scores(last - 1, causal=False)

    scores(last, causal=True)
    mx_ref[...] = jnp.broadcast_to(jnp.max(mx_ref[...], axis=-1, keepdims=True), mx_ref.shape)

    acc_ref[...] = jnp.zeros_like(acc_ref)

    def weighted(c):
        base = pl.multiple_of(c * tk, tk)
        s = s_ref[c]
        m = mx_ref[...]
        p = jnp.concatenate([jnp.exp2(s[:, j * LANES:(j + 1) * LANES] - m) for j in range(nblk)], axis=1)
        return _dot(p.astype(BF16), selv_ref[0, pl.ds(base, tk), :])

    def pass2(i, carry):
        acc_ref[...] += weighted(2 * i) + weighted(2 * i + 1)
        return carry

    lax.fori_loop(0, (last + 1) // 2, pass2, 0)

    @pl.when(last % 2 == 0)
    def _():
        acc_ref[...] += weighted(last)

    o_s = _normalise(acc_ref[...], False)

    g_rep = _dot(gate_ref[0], spread_ref[...])

    def gate(branch):
        first = branch * GROUP
        return jnp.concatenate([g_rep[:, (first + g) * LANES:(first + g + 1) * LANES] for g in range(GROUP)], axis=0)

    o = (gate(0) * o_c + gate(1) * o_s + gate(2) * o_w).reshape(GROUP, tq, LANES)
    out01 = jnp.where(lo, pltpu.roll(o[0], HEAD_DIM, 1), o[1])
    out23 = jnp.where(lo, pltpu.roll(o[2], HEAD_DIM, 1), o[3])
    o_ref[0] = jnp.concatenate([out01, out23], axis=1).astype(BF16)


def _overlap_matrix(n_cmp_rows, t):
    n_sel = t // SEL_BLOCK
    c0 = np.arange(n_cmp_rows) * CMP_STRIDE
    s_0 = np.arange(n_sel) * SEL_BLOCK
    ov = (c0[:, None] < s_0[None, :] + SEL_BLOCK) & (c0[:, None] + CMP_BLOCK > s_0[None, :])
    ov[(t - CMP_BLOCK) // CMP_STRIDE + 1:] = False
    out = np.zeros((n_cmp_rows, LANES), np.float32)
    out[:, :n_sel] = ov
    out[:, HEAD_DIM:] = 1.0
    return jnp.asarray(out, BF16)


def _window_bias(tq):
    off = np.arange(WINDOW // tq + 1).reshape(-1, 1, 1) * tq
    i = np.arange(tq).reshape(1, tq, 1)
    j = np.arange(WINDOW + tq).reshape(1, 1, -1)
    visible = (j <= off + i) & (j > off + i - WINDOW)
    return jnp.asarray(np.where(visible, 0.0, NEG).astype(np.float32))


def _gate_spread():
    n_gate = 3 * GROUP
    src = np.arange(LANES).reshape(LANES, 1)
    dst_group = np.arange(n_gate * LANES).reshape(1, -1) // LANES
    return jnp.asarray((src == dst_group).astype(np.float32), BF16)


def nsa_attn(q, gates, kv):
    cmpk, cmpv, selk, selv, wink, winv = kv
    b, t, _ = q.shape
    tq = Q_TILE
    tk = min(SEL_KEY_CHUNK, t)
    n_cmp_rows = cmpk.shape[2]
    assert t // SEL_BLOCK <= LANES - HEAD_DIM and (t // SEL_BLOCK) % 8 == 0
    assert t % tk == 0 and tk % tq == 0 and t >= WINDOW + tq
    assert WINDOW % tq == 0
    ov = _overlap_matrix(n_cmp_rows, t)
    wbias = _window_bias(tq)
    spread = _gate_spread()
    gw = GROUP * HEAD_DIM
    rows = GROUP * tq
    kv_spec = pl.BlockSpec((1, t, LANES), lambda i, h, j: (i, 0, h))
    cmp_spec = pl.BlockSpec((1, 1, n_cmp_rows, LANES), lambda i, h, j: (i, h, 0, 0))
    return pl.pallas_call(
        functools.partial(_nsa_attn_kernel, tk=tk),
        grid=(b, N_KVH, t // tq),
        in_specs=[
            pl.BlockSpec((1, tq, gw), lambda i, h, j: (i, j, h)),
            pl.BlockSpec((1, tq, LANES), lambda i, h, j: (i, j, h)),
            cmp_spec, cmp_spec, kv_spec, kv_spec, kv_spec, kv_spec,
            _resident(ov.shape), _resident(wbias.shape), _resident(spread.shape),
        ],
        out_specs=pl.BlockSpec((1, tq, gw), lambda i, h, j: (i, j, h)),
        out_shape=jax.ShapeDtypeStruct((b, t, N_QH * HEAD_DIM), BF16),
        scratch_shapes=[pltpu.VMEM((t // tk, rows, tk), F32), pltpu.VMEM((rows, LANES), F32),
                        pltpu.VMEM((rows, LANES), F32)],
        compiler_params=_params("parallel", "parallel", "arbitrary"),
        name="nsa_attn",
    )(q, gates, cmpk, cmpv, selk, selv, wink, winv, ov, wbias, spread)


def mlstm_mixer(h2, b, t, g, w_in, b_if, g_head):
    qk, v, og, gates = mlstm_proj(h2, g, w_in, b_if)
    L = MLSTM_CHUNK
    gates_t = gates[:, :2 * M_HEADS].reshape(b, t // L, L, 2, M_HEADS // 2, 2).transpose(0, 4, 3, 5, 1, 2)
    hn = mlstm_chunk(qk.reshape(b, t, -1), v.reshape(b, t, -1), og.reshape(b, t, -1), gates_t, g_head)
    return hn.reshape(b * t, -1)


def nsa_shared_kv(h2, b, t, kv_norm, kv_w, cmp_pe, cmp_w1, cmp_w2, k_norm):
    cmp_x, *packs = kv_proj(h2, t, kv_norm, kv_w, k_norm)
    cmpk, cmpv = cmp_kv(cmp_x, b, t, cmp_pe, cmp_w1, cmp_w2, k_norm[0])
    return (cmpk, cmpv) + tuple(p.reshape(b, t, -1) for p in packs)


def nsa_mixer(h2, b, t, kv, g, w_in, q_norm):
    q, gates = nsa_proj(h2, g, w_in, q_norm)
    o = nsa_attn(q.reshape(b, t, -1), gates.reshape(b, t, -1), kv)
    return o.reshape(b * t, -1)


def kernel(x, ffn_norm, ffn_w_in, ffn_w_out, mix_norm, a_w_in, a_b_if, a_g_head, a_w_out, kv_norm, kv_w, cmp_pe, cmp_w1, cmp_w2, k_norm, b_w_in, b_q_norm, b_w_out):
    b, t, d = x.shape
    depth = ffn_norm.shape[0]
    n_a = a_w_in.shape[0]
    h = x.reshape(b * t, d)
    w_in_bf, w_out_bf = ffn_w_in, ffn_w_out
    kv = None
    for layer in range(depth):
        h = ffn(h, ffn_norm[layer, 0], w_in_bf, w_out_bf, layer, 0)
        if layer < n_a:
            mixed = mlstm_mixer(h, b, t, mix_norm[layer], a_w_in[layer], a_b_if[layer], a_g_head[layer])
            w_mix = a_w_out[layer]
        else:
            j = layer - n_a
            mixed = nsa_mixer(h, b, t, kv, mix_norm[layer], b_w_in[j], b_q_norm[j])
            w_mix = b_w_out[j]
        h = ffn(h, ffn_norm[layer, 1], w_in_bf, w_out_bf, layer, 1, mixer=(mixed, w_mix))
        if layer == n_a - 1:
            kv = nsa_shared_kv(h, b, t, kv_norm, kv_w, cmp_pe, cmp_w1, cmp_w2, k_norm)
    return h.reshape(b, t, d)
```

```python
import functools

import jax
import jax.numpy as jnp
import numpy as np
from jax import lax
from jax.experimental import pallas as pl
from jax.experimental.pallas import tpu as pltpu

F32 = jnp.float32
BF16 = jnp.bfloat16

D_MODEL = 1024
D_FF = 2816
EPS = 1e-6
NEG = -1e30

M_HEADS = 8
M_DV = 128
M_DQK = 64

N_QH = 16
N_KVH = 4
GROUP = N_QH // N_KVH
HEAD_DIM = 64
CMP_BLOCK = 32
CMP_STRIDE = 16
CMP_HIDDEN = 256
SEL_BLOCK = 64
SEL_TOPK = 16
WINDOW = 512
FORCE_SCORE = 1e4
LOG2E = 1.4426950408889634
SEL_BIAS = -2.0 ** 100

LANES = 128
VMEM_LIMIT = 56 * 1024 * 1024

TOKEN_TILE = 512
FFN_CHUNK = 256
MLSTM_CHUNK = 256
MLSTM_HEADS_PER_STEP = 4
Q_TILE = 256
SEL_KEY_CHUNK = 512

_NT = (((1,), (1,)), ((), ()))


def _params(*sem):
    return pltpu.CompilerParams(dimension_semantics=sem, vmem_limit_bytes=VMEM_LIMIT)


def _rms(x, g):
    ms = jnp.mean(x * x, axis=-1, keepdims=True)
    return (x * lax.rsqrt(ms + EPS)) * g


def _dot(a, b):
    return jnp.dot(a, b, preferred_element_type=F32)


def _dot_nt(a, b, precision=None):
    return lax.dot_general(a, b, _NT, precision=precision, preferred_element_type=F32)


def _resident(shape):
    nd = len(shape)
    return pl.BlockSpec(shape, lambda *_: (0,) * nd)


def _ffn_kernel(*refs, mixer_proj):
    if mixer_proj:
        x_ref, a_ref, wmix_ref, g_ref, win_ref, wout_ref, o_ref, xn_ref, acc_ref, x_scr = refs
        x_scr[...] = x_ref[...] + _dot(a_ref[...], wmix_ref[...])
        x_ref = x_scr
    else:
        x_ref, g_ref, win_ref, wout_ref, o_ref, xn_ref, acc_ref = refs
    f = wout_ref.shape[0]
    x = x_ref[...]
    xn_ref[...] = _rms(x, g_ref[...]).astype(BF16)
    for c in range(f // FFN_CHUNK):
        cols = slice(c * FFN_CHUNK, (c + 1) * FFN_CHUNK)
        up_cols = slice(f + c * FFN_CHUNK, f + (c + 1) * FFN_CHUNK)
        xn = xn_ref[...]
        a = _dot(xn, win_ref[:, cols].astype(BF16))
        b = _dot(xn, win_ref[:, up_cols].astype(BF16))
        hid = ((a * jax.nn.sigmoid(a)) * b).astype(BF16)
        part = _dot(hid, wout_ref[cols, :].astype(BF16))
        if c == 0:
            acc_ref[...] = part
        else:
            acc_ref[...] += part
    o_ref[...] = x_ref[...] + 0.5 * acc_ref[...]


def ffn(x2, g, w_in_all, w_out_all, layer, j, mixer=None):
    n, d = x2.shape
    f = w_out_all.shape[2]
    assert f % FFN_CHUNK == 0
    tm = TOKEN_TILE
    tile = lambda w: pl.BlockSpec((tm, w), lambda i: (i, 0))
    ins, specs, scratch = [x2], [tile(d)], [pltpu.VMEM((tm, d), BF16), pltpu.VMEM((tm, d), F32)]
    if mixer is not None:
        a2, w_mix = mixer
        ins += [a2, w_mix.astype(BF16)]
        specs += [tile(a2.shape[1]), _resident(w_mix.shape)]
        scratch.append(pltpu.VMEM((tm, d), F32))
    return pl.pallas_call(
        functools.partial(_ffn_kernel, mixer_proj=mixer is not None),
        grid=(n // tm,),
        in_specs=specs + [
            _resident((1, d)),
            pl.BlockSpec((None, None, d, 2 * f), lambda i: (layer, j, 0, 0), pipeline_mode=pl.Buffered(1)),
            pl.BlockSpec((None, None, f, d), lambda i: (layer, j, 0, 0), pipeline_mode=pl.Buffered(1)),
        ],
        out_specs=tile(d),
        out_shape=jax.ShapeDtypeStruct((n, d), F32),
        scratch_shapes=scratch,
        compiler_params=_params("parallel"),
        name="ffn",
    )(*ins, g.reshape(1, d), w_in_all, w_out_all)


def _mlstm_proj_kernel(x_ref, g_ref, wqk_ref, wv_ref, wog_ref, wg_ref, bg_ref,
                       qk_ref, v_ref, og_ref, gate_ref):
    xn = _rms(x_ref[...], g_ref[...]).astype(BF16)
    qk = _dot(xn, wqk_ref[...])
    is_q = lax.broadcasted_iota(jnp.int32, (1, qk.shape[1]), 1) < qk.shape[1] // 2
    qk_ref[...] = (qk * jnp.where(is_q, M_DQK ** -0.5, 1.0)).astype(BF16)
    v_ref[...] = _dot(xn, wv_ref[...]).astype(BF16)
    og_ref[...] = jax.nn.sigmoid(_dot(xn, wog_ref[...])).astype(BF16)
    gate_ref[...] = _dot(xn, wg_ref[...]) + bg_ref[...]


def mlstm_proj(h2, g, w_in, b_if):
    n, d = h2.shape
    qk_w, v_w = M_HEADS * M_DQK, M_HEADS * M_DV
    wb = w_in.astype(BF16)
    wqk = wb[:, :2 * qk_w]
    wv = wb[:, 2 * qk_w:2 * qk_w + v_w]
    wgate = jnp.pad(wb[:, 2 * qk_w + v_w:2 * qk_w + v_w + 2 * M_HEADS], ((0, 0), (0, LANES - 2 * M_HEADS)))
    wog = wb[:, 2 * qk_w + v_w + 2 * M_HEADS:]
    bg = jnp.pad(b_if.astype(F32).reshape(1, 2 * M_HEADS), ((0, 0), (0, LANES - 2 * M_HEADS)))
    tm = TOKEN_TILE
    tile = lambda w: pl.BlockSpec((tm, w), lambda i: (i, 0))
    return pl.pallas_call(
        _mlstm_proj_kernel,
        grid=(n // tm,),
        in_specs=[tile(d), _resident((1, d)), _resident(wqk.shape), _resident(wv.shape),
                  _resident(wog.shape), _resident(wgate.shape), _resident(bg.shape)],
        out_specs=[tile(2 * qk_w), tile(v_w), tile(d), tile(LANES)],
        out_shape=[jax.ShapeDtypeStruct((n, 2 * qk_w), BF16), jax.ShapeDtypeStruct((n, v_w), BF16),
                   jax.ShapeDtypeStruct((n, d), BF16), jax.ShapeDtypeStruct((n, LANES), F32)],
        compiler_params=_params("parallel"),
        name="mlstm_proj",
    )(h2, g.reshape(1, d), wqk, wv, wog, wgate, bg)


def _log_sigmoid(x):
    return jnp.minimum(x, 0.0) - jnp.log1p(jnp.exp(-jnp.abs(x)))


def _mlstm_chunk_kernel(q_ref, k_ref, v_ref, og_ref, gate_ref, ghead_ref, o_ref, c_ref, g_r_ref, b_r_ref, cm_r_ref):
    heads = c_ref.shape[0]
    first_head = pl.program_id(1) * heads
    L = MLSTM_CHUNK
    n_chunks = q_ref.shape[1] // L

    r_i = lax.broadcasted_iota(jnp.int32, (L, L), 0)
    c_i = lax.broadcasted_iota(jnp.int32, (L, L), 1)
    causal = c_i <= r_i
    tril = causal.astype(F32)
    lane = lax.broadcasted_iota(jnp.int32, (L, LANES), 1)
    ones_blk = jnp.ones((L, LANES), BF16)

    c_ref[...] = jnp.zeros_like(c_ref)

    t_i = lax.broadcasted_iota(jnp.int32, (n_chunks, L), 1)
    for hh in range(heads):
        b = _dot_nt(_log_sigmoid(gate_ref[0, 0, 1, hh]), tril, lax.Precision.HIGHEST)
        g = gate_ref[0, 0, 0, hh] - b
        cm = g
        shift = 1
        while shift < L:
            cm = jnp.maximum(cm, jnp.where(t_i >= shift, pltpu.roll(cm, shift, 1), -jnp.inf))
            shift *= 2
        b_r_ref[hh] = b
        g_r_ref[hh] = g
        cm_r_ref[hh] = cm

    def column(row):
        return jnp.broadcast_to(row, (LANES, L)).T

    def body(c, m_prev):
        base = pl.multiple_of(c * L, L)
        m_next = []
        for hh in range(heads):
            head = first_head + hh
            pair_cols = slice((hh // 2) * LANES, (hh // 2 + 1) * LANES)
            in_head = (lane >= (hh % 2) * M_DQK) & (lane < (hh % 2 + 1) * M_DQK)
            qh = q_ref[0, pl.ds(base, L), pair_cols]
            kh = jnp.where(in_head, k_ref[0, pl.ds(base, L), pair_cols], 0)
            v_aug = jnp.concatenate([v_ref[0, pl.ds(base, L), hh * M_DV:(hh + 1) * M_DV], ones_blk], axis=1)

            g_r = g_r_ref[hh, pl.ds(c, 1), :]
            b_r = b_r_ref[hh, pl.ds(c, 1), :]
            g_tot = b_r[:, L - 1:L]
            m_p = m_prev[hh]
            c_prev = c_ref[hh]

            a_r = g_tot + g_r
            m_loc = jnp.max(a_r, axis=1, keepdims=True)
            kw_t = (kh.astype(F32).T * jnp.exp(a_r - m_loc)).astype(BF16)
            c_loc = _dot(kw_t, v_aug)

            mm = jnp.maximum(m_p, column(cm_r_ref[hh, pl.ds(c, 1), :]))
            mm_wide = jnp.concatenate([mm] * (L // LANES), axis=1)
            w = jnp.where(causal, jnp.exp(g_r - mm_wide), 0.0) * _dot_nt(qh, kh)
            num = jnp.concatenate([jnp.exp(m_p - mm)] * 2, axis=1) * _dot(qh, c_prev.astype(BF16)) \
                + _dot(w.astype(BF16), v_aug)
            floor = jnp.exp(-(column(b_r) + mm))
            h_out = num[:, :M_DV] / jnp.maximum(jnp.abs(num[:, M_DV:]), floor)
            h_out = _rms(h_out, ghead_ref[pl.ds(head, 1), :])
            cols = slice(hh * M_DV, (hh + 1) * M_DV)
            o_ref[0, pl.ds(base, L), cols] = (h_out * og_ref[0, pl.ds(base, L), cols]).astype(BF16)

            m_new = jnp.maximum(g_tot + m_p, m_loc)
            c_ref[hh] = jnp.exp(g_tot + m_p - m_new) * c_prev + jnp.exp(m_loc - m_new) * c_loc
            m_next.append(m_new)
        return tuple(m_next)

    lax.fori_loop(0, n_chunks, body, (jnp.zeros((1, 1), F32),) * heads)


def mlstm_chunk(qk, v, og, gates_t, g_head):
    b, t, _ = v.shape
    L = MLSTM_CHUNK
    nc = t // L
    hp = MLSTM_HEADS_PER_STEP
    groups = M_HEADS // hp
    qk_cols, v_cols = hp * M_DQK, hp * M_DV
    row_scratch = pltpu.VMEM((hp, nc, L), F32)
    return pl.pallas_call(
        _mlstm_chunk_kernel,
        grid=(b, groups),
        in_specs=[
            pl.BlockSpec((1, t, qk_cols), lambda i, p: (i, 0, p)),
            pl.BlockSpec((1, t, qk_cols), lambda i, p: (i, 0, groups + p)),
            pl.BlockSpec((1, t, v_cols), lambda i, p: (i, 0, p)),
            pl.BlockSpec((1, t, v_cols), lambda i, p: (i, 0, p)),
            pl.BlockSpec((1, 1, 2, hp, nc, L), lambda i, p: (i, p, 0, 0, 0, 0)),
            _resident((M_HEADS, M_DV)),
        ],
        out_specs=pl.BlockSpec((1, t, v_cols), lambda i, p: (i, 0, p)),
        out_shape=jax.ShapeDtypeStruct((b, t, M_HEADS * M_DV), BF16),
        scratch_shapes=[pltpu.VMEM((hp, 2 * M_DQK, 2 * M_DV), F32), row_scratch, row_scratch, row_scratch],
        compiler_params=_params("parallel", "parallel"),
        name="mlstm_chunk",
    )(qk, qk, v, og, gates_t, g_head.astype(F32))


def _kv_packs(y, gain, extra):
    lane = lax.broadcasted_iota(jnp.int32, (1, LANES), 1)
    lo = lane < HEAD_DIM
    kp, vp = [], []
    for h in range(N_KVH):
        yh = y[:, h * LANES:(h + 1) * LANES]
        ms = jnp.sum(jnp.where(lo, yh * yh, 0.0), axis=-1, keepdims=True) * (1.0 / HEAD_DIM)
        kp.append(jnp.where(lo, (yh * lax.rsqrt(ms + EPS)) * gain, extra))
        vp.append(jnp.where(lo, 1.0, yh))
    return jnp.concatenate(kp, axis=1), jnp.concatenate(vp, axis=1)


def _kv_proj_kernel(x_ref, g_ref, wc_ref, ws_ref, ww_ref, gs_ref, gw_ref,
                    cmp_ref, selk_ref, selv_ref, wink_ref, winv_ref, y_ref, *, seq_len):
    tm = x_ref.shape[0]
    xn = _rms(x_ref[...], g_ref[...]).astype(BF16)

    y = _dot(xn, wc_ref[...])
    n_slab = y.shape[1] // LANES
    for s in range(n_slab):
        y_ref[s] = y[:, s * LANES:(s + 1) * LANES]
    groups = tm // CMP_STRIDE
    lo = lax.broadcasted_iota(jnp.int32, (1, LANES), 1) < HEAD_DIM
    for s in range(n_slab):
        for rp in range(CMP_STRIDE // 2):
            even = y_ref[s, pl.ds(2 * rp, groups, stride=CMP_STRIDE), :]
            odd = y_ref[s, pl.ds(2 * rp + 1, groups, stride=CMP_STRIDE), :]
            cols = slice(rp * LANES, (rp + 1) * LANES)
            cmp_ref[2 * s, :, cols] = jnp.where(lo, even, pltpu.roll(odd, HEAD_DIM, 1))
            cmp_ref[2 * s + 1, :, cols] = jnp.where(lo, pltpu.roll(even, HEAD_DIM, 1), odd)

    pos = (pl.program_id(0) * tm) % seq_len + lax.broadcasted_iota(jnp.int32, (tm, 1), 0)
    lane = lax.broadcasted_iota(jnp.int32, (1, LANES), 1)
    block_bias = jnp.where(lane == HEAD_DIM + lax.shift_right_logical(pos, 6), SEL_BIAS, 0.0)
    kp, vp = _kv_packs(_dot(xn, ws_ref[...]), gs_ref[...], block_bias)
    selk_ref[...] = kp.astype(BF16)
    selv_ref[...] = vp.astype(BF16)
    kp, vp = _kv_packs(_dot(xn, ww_ref[...]), gw_ref[...], 0.0)
    wink_ref[...] = kp.astype(BF16)
    winv_ref[...] = vp.astype(BF16)


def _kv_pack_weights(wk, wv):
    d = wk.shape[0]
    wk = wk.reshape(d, N_KVH, HEAD_DIM)
    wv = wv.reshape(d, N_KVH, HEAD_DIM)
    return jnp.concatenate([wk, wv], axis=-1).reshape(d, N_KVH * LANES)


def _k_gain_row(gain):
    return jnp.concatenate([gain.astype(F32), jnp.ones((HEAD_DIM,), F32)]).reshape(1, LANES)


def kv_proj(h2, t, g, kv_w, k_norm):
    n, d = h2.shape
    wb = kv_w.astype(BF16)
    kc, vc, ks, vs, kw, vw = jnp.split(wb, 6, axis=1)
    wcmp = jnp.concatenate([kc, vc], axis=1)
    wsel = _kv_pack_weights(ks, vs)
    wwin = _kv_pack_weights(kw, vw)
    wd = N_KVH * LANES
    tm = TOKEN_TILE
    assert t % tm == 0 and t // SEL_BLOCK <= LANES - HEAD_DIM
    tile = lambda w: pl.BlockSpec((tm, w), lambda i: (i, 0))
    packed = jax.ShapeDtypeStruct((n, wd), BF16)
    feat = CMP_STRIDE * HEAD_DIM
    groups = tm // CMP_STRIDE
    return pl.pallas_call(
        functools.partial(_kv_proj_kernel, seq_len=t),
        grid=(n // tm,),
        in_specs=[tile(d), _resident((1, d)), _resident(wcmp.shape), _resident(wsel.shape),
                  _resident(wwin.shape), _resident((1, LANES)), _resident((1, LANES))],
        out_specs=[pl.BlockSpec((2 * N_KVH, groups, feat), lambda i: (0, i, 0))] + [tile(wd)] * 4,
        out_shape=[jax.ShapeDtypeStruct((2 * N_KVH, n // CMP_STRIDE, feat), F32), packed, packed, packed, packed],
        scratch_shapes=[pltpu.VMEM((2 * N_KVH * HEAD_DIM // LANES, tm, LANES), F32)],
        compiler_params=_params("parallel"),
        name="kv_proj",
    )(h2, g.reshape(1, d), wcmp, wsel, wwin, _k_gain_row(k_norm[1]), _k_gain_row(k_norm[2]))


def _cmp_kernel(xk_ref, xv_ref, pe_ref, w1_ref, w2k_ref, w2v_ref, gk_ref, ok_ref, ov_ref):
    def compress(x, kv, w2):
        u = _dot((x + pe_ref[kv, 0]).astype(BF16), w1_ref[kv, 0])
        v = _dot((x + pe_ref[kv, 1]).astype(BF16), w1_ref[kv, 1])
        nrow = v.shape[0]
        hid = u + pltpu.roll(v, nrow - 1, 0)
        return _dot((hid * jax.nn.sigmoid(hid)).astype(BF16), w2)

    yk = compress(xk_ref[0], 0, w2k_ref[...])
    yv = compress(xv_ref[0], 1, w2v_ref[...])
    ms = jnp.sum(yk * yk, axis=-1, keepdims=True) * (1.0 / HEAD_DIM)
    ok_ref[0, 0] = ((yk * lax.rsqrt(ms + EPS)) * gk_ref[...]).astype(BF16)
    lane = lax.broadcasted_iota(jnp.int32, (1, LANES), 1)
    ov_ref[0, 0] = jnp.where(lane < HEAD_DIM, 1.0, yv).astype(BF16)


def cmp_kv(cmp_x, b, t, cmp_pe, cmp_w1, cmp_w2, k_gain):
    ng = t // CMP_STRIDE
    feat = CMP_STRIDE * HEAD_DIM
    pe = cmp_pe.astype(F32).reshape(2, 2, 1, feat)
    w1 = cmp_w1.astype(BF16).reshape(2, 2, feat, CMP_HIDDEN)
    zeros = jnp.zeros((CMP_HIDDEN, HEAD_DIM), BF16)
    w2k = jnp.concatenate([cmp_w2[0].astype(BF16), zeros], axis=1)
    w2v = jnp.concatenate([zeros, cmp_w2[1].astype(BF16)], axis=1)
    ospec = pl.BlockSpec((1, 1, ng, LANES), lambda i, h: (i, h, 0, 0))
    packed = jax.ShapeDtypeStruct((b, N_KVH, ng, LANES), BF16)
    return pl.pallas_call(
        _cmp_kernel,
        grid=(b, N_KVH),
        in_specs=[pl.BlockSpec((1, ng, feat), lambda i, h: (h, i, 0)),
                  pl.BlockSpec((1, ng, feat), lambda i, h: (N_KVH + h, i, 0)),
                  _resident(pe.shape), _resident(w1.shape), _resident(w2k.shape),
                  _resident(w2v.shape), _resident((1, LANES))],
        out_specs=[ospec, ospec],
        out_shape=[packed, packed],
        compiler_params=_params("parallel", "parallel"),
        name="cmp_kv",
    )(cmp_x, cmp_x, pe, w1, w2k, w2v, _k_gain_row(k_gain))


def _nsa_proj_kernel(x_ref, g_ref, wq_ref, wg_ref, qg_ref, q_ref, gate_ref):
    xn = _rms(x_ref[...], g_ref[...]).astype(BF16)
    q = _dot(xn, wq_ref[...])
    lane = lax.broadcasted_iota(jnp.int32, (1, LANES), 1)
    lo = lane < HEAD_DIM
    outs = []
    for cb in range(q.shape[1] // LANES):
        y = q[:, cb * LANES:(cb + 1) * LANES]
        y2 = y * y
        ms_lo = jnp.sum(jnp.where(lo, y2, 0.0), axis=-1, keepdims=True) * (1.0 / HEAD_DIM)
        ms_hi = jnp.sum(jnp.where(lo, 0.0, y2), axis=-1, keepdims=True) * (1.0 / HEAD_DIM)
        scale = jnp.where(lo, lax.rsqrt(ms_lo + EPS), lax.rsqrt(ms_hi + EPS))
        outs.append(((y * scale) * qg_ref[...]) * (HEAD_DIM ** -0.5 * LOG2E))
    q_ref[...] = jnp.concatenate(outs, axis=1).astype(BF16)
    gate_ref[...] = jax.nn.sigmoid(_dot(xn, wg_ref[...])).astype(BF16)


def nsa_proj(h2, g, w_in, q_norm):
    n, d = h2.shape
    wb = w_in.astype(BF16)
    wq = wb[:, :N_QH * HEAD_DIM]
    wg = wb[:, N_QH * HEAD_DIM:].reshape(d, 3, N_KVH, GROUP).transpose(0, 2, 1, 3).reshape(d, N_KVH, 3 * GROUP)
    wg = jnp.pad(wg, ((0, 0), (0, 0), (0, LANES - 3 * GROUP))).reshape(d, N_KVH * LANES)
    qg = jnp.tile(q_norm.astype(F32), LANES // HEAD_DIM).reshape(1, LANES)
    tm = TOKEN_TILE
    tile = lambda w: pl.BlockSpec((tm, w), lambda i: (i, 0))
    return pl.pallas_call(
        _nsa_proj_kernel,
        grid=(n // tm,),
        in_specs=[tile(d), _resident((1, d)), _resident(wq.shape), _resident(wg.shape), _resident((1, LANES))],
        out_specs=[tile(N_QH * HEAD_DIM), tile(N_KVH * LANES)],
        out_shape=[jax.ShapeDtypeStruct((n, N_QH * HEAD_DIM), BF16),
                   jax.ShapeDtypeStruct((n, N_KVH * LANES), BF16)],
        compiler_params=_params("parallel"),
        name="nsa_proj",
    )(h2, g.reshape(1, d), wq, wg, qg)


def _masked_exp2(s3, mask):
    sm = jnp.where(mask[None], s3, NEG)
    mx = jnp.max(sm, axis=-1, keepdims=True)
    return jnp.exp2(sm - mx), mx > 0.5 * NEG


def _normalise(acc, values_lo):
    lo = lax.broadcasted_iota(jnp.int32, acc.shape, 1) < HEAD_DIM
    return acc / jnp.where(lo == values_lo, pltpu.roll(acc, HEAD_DIM, 1), 1.0)


def _nsa_attn_kernel(q_ref, gate_ref, cmpk_ref, cmpv_ref, selk_ref, selv_ref, wink_ref, winv_ref,
                     ov_ref, wbias_ref, spread_ref, o_ref, s_ref, mx_ref, acc_ref, *, tk):
    tq = q_ref.shape[1]
    rows = GROUP * tq
    s0 = pl.program_id(2) * tq

    lane = lax.broadcasted_iota(jnp.int32, (tq, LANES), 1)
    lo = lane < HEAD_DIM
    qs = []
    for g in range(GROUP):
        blk = q_ref[0, :, (g // 2) * LANES:(g // 2 + 1) * LANES].astype(F32)
        if g % 2 == 1:
            blk = pltpu.roll(blk, HEAD_DIM, 1)
        qs.append(jnp.where(lo, blk, 0.0))
    q4 = jnp.concatenate(qs, axis=0)
    qst = q4.astype(BF16)
    t_q = s0 + lax.broadcasted_iota(jnp.int32, (tq, 1), 0)

    kc = cmpk_ref[0, 0]
    n_cmp = kc.shape[0]
    cmp_end = lax.broadcasted_iota(jnp.int32, (1, n_cmp), 1) * CMP_STRIDE + (CMP_BLOCK - 1)
    e_c, ok_c = _masked_exp2(_dot_nt(qst, kc).reshape(GROUP, tq, n_cmp), cmp_end <= t_q)
    e_c = e_c.reshape(rows, n_cmp)
    ok_c = ok_c.reshape(rows, 1)
    both = _dot(e_c.astype(BF16), jnp.concatenate([cmpv_ref[0, 0], ov_ref[...]], axis=1))
    o_c = jnp.where(ok_c, _normalise(both[:, :LANES], False), 0.0)

    imp4 = jnp.where(ok_c, _normalise(both[:, LANES:], True), 0.0).reshape(GROUP, tq, LANES)
    imp = imp4[0] + imp4[1] + imp4[2] + imp4[3]

    n_sel = selk_ref.shape[1] // SEL_BLOCK
    valid = (lane * SEL_BLOCK <= t_q) & (lane < n_sel)
    cur = lax.shift_right_logical(t_q, 6)
    forced = (lane == 0) | (lane == cur) | (lane == cur - 1)
    score = jnp.where(forced & valid, FORCE_SCORE, jnp.where(valid, imp, -1.0))
    score = jnp.where(lane < n_sel, score, -2.0)
    s_t = score.T[:HEAD_DIM]
    sub = lax.broadcasted_iota(jnp.int32, (8, tq), 0)
    groups = [s_t[8 * v:8 * v + 8] for v in range(HEAD_DIM // 8)]
    ranks = [jnp.zeros((8, tq), F32) for _ in groups]
    for i in range(n_sel):
        s_i = s_t[i:i + 1, :]
        for v, s_v in enumerate(groups):
            if 8 * v + 7 < i:
                ahead = s_i > s_v
            elif 8 * v > i:
                ahead = s_i >= s_v
            else:
                ahead = (s_i > s_v) | ((s_i == s_v) & (sub > i - 8 * v))
            ranks[v] = ranks[v] + jnp.where(ahead, 1.0, 0.0)
    rank = jnp.concatenate(ranks, axis=0)
    blk_t = lax.broadcasted_iota(jnp.int32, (HEAD_DIM, tq), 0)
    t_row = s0 + lax.broadcasted_iota(jnp.int32, (1, tq), 1)
    dropped_t = jnp.where((rank < SEL_TOPK) & (blk_t * SEL_BLOCK <= t_row), 0.0, 1.0)
    dropped = jnp.concatenate([jnp.zeros((HEAD_DIM, tq), F32), dropped_t], axis=0).T
    qx = (q4 + jnp.concatenate([dropped] * GROUP, axis=0)).astype(BF16)

    wk = WINDOW + tq
    w0 = pl.multiple_of(jnp.maximum(s0 - WINDOW, 0), tq)
    s_w = _dot_nt(qst, wink_ref[0, pl.ds(w0, wk), :]).reshape(GROUP, tq, wk)
    s_w = s_w + wbias_ref[jnp.minimum(pl.program_id(2), WINDOW // tq)][None]
    e_w = jnp.exp2(s_w - jnp.max(s_w, axis=-1, keepdims=True))
    o_w = _normalise(_dot(e_w.reshape(rows, wk).astype(BF16), winv_ref[0, pl.ds(w0, wk), :]), False)

    nblk = tk // LANES
    last = s0 // tk
    mx_ref[...] = jnp.full(mx_ref.shape, NEG, F32)

    def scores(c, causal):
        base = pl.multiple_of(c * tk, tk)
        s = _dot_nt(qx, selk_ref[0, pl.ds(base, tk), :])
        if causal:
            kpos = base + lax.broadcasted_iota(jnp.int32, (1, tk), 1)
            s = jnp.where((kpos <= t_q)[None], s.reshape(GROUP, tq, tk), NEG).reshape(rows, tk)
        s_ref[c] = s
        m = mx_ref[...]
        for j in range(nblk):
            m = jnp.maximum(m, s[:, j * LANES:(j + 1) * LANES])
        mx_ref[...] = m

    def pass1(i, carry):
        scores(2 * i, causal=False)
        scores(2 * i + 1, causal=False)
        return carry

    lax.fori_loop(0, last // 2, pass1, 0)

    @pl.when(last % 2 == 1)
    def _():
        scores(last - 1, causal=False)

    scores(last, causal=True)
    mx_ref[...] = jnp.broadcast_to(jnp.max(mx_ref[...], axis=-1, keepdims=True), mx_ref.shape)

    acc_ref[...] = jnp.zeros_like(acc_ref)

    def weighted(c):
        base = pl.multiple_of(c * tk, tk)
        s = s_ref[c]
        m = mx_ref[...]
        p = jnp.concatenate([jnp.exp2(s[:, j * LANES:(j + 1) * LANES] - m) for j in range(nblk)], axis=1)
        return _dot(p.astype(BF16), selv_ref[0, pl.ds(base, tk), :])

    def pass2(i, carry):
        acc_ref[...] += weighted(2 * i) + weighted(2 * i + 1)
        return carry

    lax.fori_loop(0, (last + 1) // 2, pass2, 0)

    @pl.when(last % 2 == 0)
    def _():
        acc_ref[...] += weighted(last)

    o_s = _normalise(acc_ref[...], False)

    g_rep = _dot(gate_ref[0], spread_ref[...])

    def gate(branch):
        first = branch * GROUP
        return jnp.concatenate([g_rep[:, (first + g) * LANES:(first + g + 1) * LANES] for g in range(GROUP)], axis=0)

    o = (gate(0) * o_c + gate(1) * o_s + gate(2) * o_w).reshape(GROUP, tq, LANES)
    out01 = jnp.where(lo, pltpu.roll(o[0], HEAD_DIM, 1), o[1])
    out23 = jnp.where(lo, pltpu.roll(o[2], HEAD_DIM, 1), o[3])
    o_ref[0] = jnp.concatenate([out01, out23], axis=1).astype(BF16)


def _overlap_matrix(n_cmp_rows, t):
    n_sel = t // SEL_BLOCK
    c0 = np.arange(n_cmp_rows) * CMP_STRIDE
    s_0 = np.arange(n_sel) * SEL_BLOCK
    ov = (c0[:, None] < s_0[None, :] + SEL_BLOCK) & (c0[:, None] + CMP_BLOCK > s_0[None, :])
    ov[(t - CMP_BLOCK) // CMP_STRIDE + 1:] = False
    out = np.zeros((n_cmp_rows, LANES), np.float32)
    out[:, :n_sel] = ov
    out[:, HEAD_DIM:] = 1.0
    return jnp.asarray(out, BF16)


def _window_bias(tq):
    off = np.arange(WINDOW // tq + 1).reshape(-1, 1, 1) * tq
    i = np.arange(tq).reshape(1, tq, 1)
    j = np.arange(WINDOW + tq).reshape(1, 1, -1)
    visible = (j <= off + i) & (j > off + i - WINDOW)
    return jnp.asarray(np.where(visible, 0.0, NEG).astype(np.float32))


def _gate_spread():
    n_gate = 3 * GROUP
    src = np.arange(LANES).reshape(LANES, 1)
    dst_group = np.arange(n_gate * LANES).reshape(1, -1) // LANES
    return jnp.asarray((src == dst_group).astype(np.float32), BF16)


def nsa_attn(q, gates, kv):
    cmpk, cmpv, selk, selv, wink, winv = kv
    b, t, _ = q.shape
    tq = Q_TILE
    tk = min(SEL_KEY_CHUNK, t)
    n_cmp_rows = cmpk.shape[2]
    assert t // SEL_BLOCK <= LANES - HEAD_DIM and (t // SEL_BLOCK) % 8 == 0
    assert t % tk == 0 and tk % tq == 0 and t >= WINDOW + tq
    assert WINDOW % tq == 0
    ov = _overlap_matrix(n_cmp_rows, t)
    wbias = _window_bias(tq)
    spread = _gate_spread()
    gw = GROUP * HEAD_DIM
    rows = GROUP * tq
    kv_spec = pl.BlockSpec((1, t, LANES), lambda i, h, j: (i, 0, h))
    cmp_spec = pl.BlockSpec((1, 1, n_cmp_rows, LANES), lambda i, h, j: (i, h, 0, 0))
    return pl.pallas_call(
        functools.partial(_nsa_attn_kernel, tk=tk),
        grid=(b, N_KVH, t // tq),
        in_specs=[
            pl.BlockSpec((1, tq, gw), lambda i, h, j: (i, j, h)),
            pl.BlockSpec((1, tq, LANES), lambda i, h, j: (i, j, h)),
            cmp_spec, cmp_spec, kv_spec, kv_spec, kv_spec, kv_spec,
            _resident(ov.shape), _resident(wbias.shape), _resident(spread.shape),
        ],
        out_specs=pl.BlockSpec((1, tq, gw), lambda i, h, j: (i, j, h)),
        out_shape=jax.ShapeDtypeStruct((b, t, N_QH * HEAD_DIM), BF16),
        scratch_shapes=[pltpu.VMEM((t // tk, rows, tk), F32), pltpu.VMEM((rows, LANES), F32),
                        pltpu.VMEM((rows, LANES), F32)],
        compiler_params=_params("parallel", "parallel", "arbitrary"),
        name="nsa_attn",
    )(q, gates, cmpk, cmpv, selk, selv, wink, winv, ov, wbias, spread)


def mlstm_mixer(h2, b, t, g, w_in, b_if, g_head):
    qk, v, og, gates = mlstm_proj(h2, g, w_in, b_if)
    L = MLSTM_CHUNK
    hp = MLSTM_HEADS_PER_STEP
    gates_t = gates[:, :2 * M_HEADS].reshape(b, t // L, L, 2, M_HEADS // hp, hp).transpose(0, 4, 3, 5, 1, 2)
    hn = mlstm_chunk(qk.reshape(b, t, -1), v.reshape(b, t, -1), og.reshape(b, t, -1), gates_t, g_head)
    return hn.reshape(b * t, -1)


def nsa_shared_kv(h2, b, t, kv_norm, kv_w, cmp_pe, cmp_w1, cmp_w2, k_norm):
    cmp_x, *packs = kv_proj(h2, t, kv_norm, kv_w, k_norm)
    cmpk, cmpv = cmp_kv(cmp_x, b, t, cmp_pe, cmp_w1, cmp_w2, k_norm[0])
    return (cmpk, cmpv) + tuple(p.reshape(b, t, -1) for p in packs)


def nsa_mixer(h2, b, t, kv, g, w_in, q_norm):
    q, gates = nsa_proj(h2, g, w_in, q_norm)
    o = nsa_attn(q.reshape(b, t, -1), gates.reshape(b, t, -1), kv)
    return o.reshape(b * t, -1)


def kernel(x, ffn_norm, ffn_w_in, ffn_w_out, mix_norm, a_w_in, a_b_if, a_g_head, a_w_out, kv_norm, kv_w, cmp_pe, cmp_w1, cmp_w2, k_norm, b_w_in, b_q_norm, b_w_out):
    b, t, d = x.shape
    depth = ffn_norm.shape[0]
    n_a = a_w_in.shape[0]
    h = x.reshape(b * t, d)
    w_in_bf, w_out_bf = ffn_w_in, ffn_w_out
    kv = None
    for layer in range(depth):
        h = ffn(h, ffn_norm[layer, 0], w_in_bf, w_out_bf, layer, 0)
        if layer < n_a:
            mixed = mlstm_mixer(h, b, t, mix_norm[layer], a_w_in[layer], a_b_if[layer], a_g_head[layer])
            w_mix = a_w_out[layer]
        else:
            j = layer - n_a
            mixed = nsa_mixer(h, b, t, kv, mix_norm[layer], b_w_in[j], b_q_norm[j])
            w_mix = b_w_out[j]
        h = ffn(h, ffn_norm[layer, 1], w_in_bf, w_out_bf, layer, 1, mixer=(mixed, w_mix))
        if layer == n_a - 1:
            kv = nsa_shared_kv(h, b, t, kv_norm, kv_w, cmp_pe, cmp_w1, cmp_w2, k_norm)
    return h.reshape(b, t, d)
```

```python
import functools

import jax
import jax.numpy as jnp
import numpy as np
from jax import lax
from jax.experimental import pallas as pl
from jax.experimental.pallas import tpu as pltpu

F32 = jnp.float32
BF16 = jnp.bfloat16

D_MODEL = 1024
D_FF = 2816
EPS = 1e-6
NEG = -1e30

M_HEADS = 8
M_DV = 128
M_DQK = 64

N_QH = 16
N_KVH = 4
GROUP = N_QH // N_KVH
HEAD_DIM = 64
CMP_BLOCK = 32
CMP_STRIDE = 16
CMP_HIDDEN = 256
SEL_BLOCK = 64
SEL_TOPK = 16
WINDOW = 512
FORCE_SCORE = 1e4
LOG2E = 1.4426950408889634
SEL_BIAS = -2.0 ** 100

LANES = 128
VMEM_LIMIT = 56 * 1024 * 1024

TOKEN_TILE = 512
FFN_CHUNK = 256
MLSTM_CHUNK = 256
MLSTM_HEADS_PER_STEP = 4
Q_TILE = 256
SEL_KEY_CHUNK = 512

_NT = (((1,), (1,)), ((), ()))


def _params(*sem):
    return pltpu.CompilerParams(dimension_semantics=sem, vmem_limit_bytes=VMEM_LIMIT)


def _rms(x, g):
    ms = jnp.mean(x * x, axis=-1, keepdims=True)
    return (x * lax.rsqrt(ms + EPS)) * g


def _dot(a, b):
    return jnp.dot(a, b, preferred_element_type=F32)


def _dot_nt(a, b, precision=None):
    return lax.dot_general(a, b, _NT, precision=precision, preferred_element_type=F32)


def _resident(shape):
    nd = len(shape)
    return pl.BlockSpec(shape, lambda *_: (0,) * nd)


def _ffn_kernel(*refs, mixer_proj):
    if mixer_proj:
        x_ref, a_ref, wmix_ref, g_ref, win_ref, wout_ref, o_ref, xn_ref, acc_ref, x_scr = refs
        x_scr[...] = x_ref[...] + _dot(a_ref[...], wmix_ref[...])
        x_ref = x_scr
    else:
        x_ref, g_ref, win_ref, wout_ref, o_ref, xn_ref, acc_ref = refs
    f = wout_ref.shape[0]
    x = x_ref[...]
    xn_ref[...] = _rms(x, g_ref[...]).astype(BF16)
    for c in range(f // FFN_CHUNK):
        cols = slice(c * FFN_CHUNK, (c + 1) * FFN_CHUNK)
        up_cols = slice(f + c * FFN_CHUNK, f + (c + 1) * FFN_CHUNK)
        xn = xn_ref[...]
        a = _dot(xn, win_ref[:, cols].astype(BF16))
        b = _dot(xn, win_ref[:, up_cols].astype(BF16))
        hid = ((a * jax.nn.sigmoid(a)) * b).astype(BF16)
        part = _dot(hid, wout_ref[cols, :].astype(BF16))
        if c == 0:
            acc_ref[...] = part
        else:
            acc_ref[...] += part
    o_ref[...] = x_ref[...] + 0.5 * acc_ref[...]


def ffn(x2, g, w_in_all, w_out_all, layer, j, mixer=None):
    n, d = x2.shape
    f = w_out_all.shape[2]
    assert f % FFN_CHUNK == 0
    tm = TOKEN_TILE
    tile = lambda w: pl.BlockSpec((tm, w), lambda i: (i, 0))
    ins, specs, scratch = [x2], [tile(d)], [pltpu.VMEM((tm, d), BF16), pltpu.VMEM((tm, d), F32)]
    if mixer is not None:
        a2, w_mix = mixer
        ins += [a2, w_mix.astype(BF16)]
        specs += [tile(a2.shape[1]), _resident(w_mix.shape)]
        scratch.append(pltpu.VMEM((tm, d), F32))
    return pl.pallas_call(
        functools.partial(_ffn_kernel, mixer_proj=mixer is not None),
        grid=(n // tm,),
        in_specs=specs + [
            _resident((1, d)),
            pl.BlockSpec((None, None, d, 2 * f), lambda i: (layer, j, 0, 0), pipeline_mode=pl.Buffered(1)),
            pl.BlockSpec((None, None, f, d), lambda i: (layer, j, 0, 0), pipeline_mode=pl.Buffered(1)),
        ],
        out_specs=tile(d),
        out_shape=jax.ShapeDtypeStruct((n, d), F32),
        scratch_shapes=scratch,
        compiler_params=_params("parallel"),
        name="ffn",
    )(*ins, g.reshape(1, d), w_in_all, w_out_all)


def _mlstm_proj_kernel(x_ref, g_ref, wqk_ref, wv_ref, wog_ref, wg_ref, bg_ref,
                       qk_ref, v_ref, og_ref, gate_ref):
    xn = _rms(x_ref[...], g_ref[...]).astype(BF16)
    qk = _dot(xn, wqk_ref[...])
    is_q = lax.broadcasted_iota(jnp.int32, (1, qk.shape[1]), 1) < qk.shape[1] // 2
    qk_ref[...] = (qk * jnp.where(is_q, M_DQK ** -0.5, 1.0)).astype(BF16)
    v_ref[...] = _dot(xn, wv_ref[...]).astype(BF16)
    og_ref[...] = jax.nn.sigmoid(_dot(xn, wog_ref[...])).astype(BF16)
    gate_ref[...] = _dot(xn, wg_ref[...]) + bg_ref[...]


def mlstm_proj(h2, g, w_in, b_if):
    n, d = h2.shape
    qk_w, v_w = M_HEADS * M_DQK, M_HEADS * M_DV
    wb = w_in.astype(BF16)
    wqk = wb[:, :2 * qk_w]
    wv = wb[:, 2 * qk_w:2 * qk_w + v_w]
    wgate = jnp.pad(wb[:, 2 * qk_w + v_w:2 * qk_w + v_w + 2 * M_HEADS], ((0, 0), (0, LANES - 2 * M_HEADS)))
    wog = wb[:, 2 * qk_w + v_w + 2 * M_HEADS:]
    bg = jnp.pad(b_if.astype(F32).reshape(1, 2 * M_HEADS), ((0, 0), (0, LANES - 2 * M_HEADS)))
    tm = TOKEN_TILE
    tile = lambda w: pl.BlockSpec((tm, w), lambda i: (i, 0))
    return pl.pallas_call(
        _mlstm_proj_kernel,
        grid=(n // tm,),
        in_specs=[tile(d), _resident((1, d)), _resident(wqk.shape), _resident(wv.shape),
                  _resident(wog.shape), _resident(wgate.shape), _resident(bg.shape)],
        out_specs=[tile(2 * qk_w), tile(v_w), tile(d), tile(LANES)],
        out_shape=[jax.ShapeDtypeStruct((n, 2 * qk_w), BF16), jax.ShapeDtypeStruct((n, v_w), BF16),
                   jax.ShapeDtypeStruct((n, d), BF16), jax.ShapeDtypeStruct((n, LANES), F32)],
        compiler_params=_params("parallel"),
        name="mlstm_proj",
    )(h2, g.reshape(1, d), wqk, wv, wog, wgate, bg)


def _log_sigmoid(x):
    return jnp.minimum(x, 0.0) - jnp.log1p(jnp.exp(-jnp.abs(x)))


def _mlstm_chunk_kernel(q_ref, k_ref, v_ref, og_ref, gate_ref, ghead_ref, o_ref, c_ref, g_r_ref, b_r_ref, cm_r_ref):
    heads = c_ref.shape[0]
    first_head = pl.program_id(1) * heads
    L = MLSTM_CHUNK
    n_chunks = q_ref.shape[1] // L

    r_i = lax.broadcasted_iota(jnp.int32, (L, L), 0)
    c_i = lax.broadcasted_iota(jnp.int32, (L, L), 1)
    causal = c_i <= r_i
    tril = causal.astype(F32)
    lane = lax.broadcasted_iota(jnp.int32, (L, LANES), 1)
    ones_blk = jnp.ones((L, LANES), BF16)

    c_ref[...] = jnp.zeros_like(c_ref)

    t_i = lax.broadcasted_iota(jnp.int32, (n_chunks, L), 1)
    for hh in range(heads):
        b = _dot_nt(_log_sigmoid(gate_ref[0, 0, 1, hh]), tril, lax.Precision.HIGHEST)
        g = gate_ref[0, 0, 0, hh] - b
        cm = g
        shift = 1
        while shift < L:
            cm = jnp.maximum(cm, jnp.where(t_i >= shift, pltpu.roll(cm, shift, 1), -jnp.inf))
            shift *= 2
        b_r_ref[hh] = b
        g_r_ref[hh] = g
        cm_r_ref[hh] = cm

    def column(row):
        return jnp.broadcast_to(row, (LANES, L)).T

    def body(c, m_prev):
        base = pl.multiple_of(c * L, L)
        m_next = []
        for hh in range(heads):
            head = first_head + hh
            pair_cols = slice((hh // 2) * LANES, (hh // 2 + 1) * LANES)
            in_head = (lane >= (hh % 2) * M_DQK) & (lane < (hh % 2 + 1) * M_DQK)
            qh = q_ref[0, pl.ds(base, L), pair_cols]
            kh = jnp.where(in_head, k_ref[0, pl.ds(base, L), pair_cols], 0)
            v_aug = jnp.concatenate([v_ref[0, pl.ds(base, L), hh * M_DV:(hh + 1) * M_DV], ones_blk], axis=1)

            g_r = g_r_ref[hh, pl.ds(c, 1), :]
            b_r = b_r_ref[hh, pl.ds(c, 1), :]
            g_tot = b_r[:, L - 1:L]
            m_p = m_prev[hh]
            c_prev = c_ref[hh]

            a_r = g_tot + g_r
            m_loc = jnp.max(a_r, axis=1, keepdims=True)
            kw_t = (kh.astype(F32).T * jnp.exp(a_r - m_loc)).astype(BF16)
            c_loc = _dot(kw_t, v_aug)

            mm = jnp.maximum(m_p, column(cm_r_ref[hh, pl.ds(c, 1), :]))
            mm_wide = jnp.concatenate([mm] * (L // LANES), axis=1)
            w = jnp.where(causal, jnp.exp(g_r - mm_wide), 0.0) * _dot_nt(qh, kh)
            num = jnp.concatenate([jnp.exp(m_p - mm)] * 2, axis=1) * _dot(qh, c_prev.astype(BF16)) \
                + _dot(w.astype(BF16), v_aug)
            floor = jnp.exp(-(column(b_r) + mm))
            h_out = num[:, :M_DV] / jnp.maximum(jnp.abs(num[:, M_DV:]), floor)
            h_out = _rms(h_out, ghead_ref[pl.ds(head, 1), :])
            cols = slice(hh * M_DV, (hh + 1) * M_DV)
            o_ref[0, pl.ds(base, L), cols] = (h_out * og_ref[0, pl.ds(base, L), cols]).astype(BF16)

            m_new = jnp.maximum(g_tot + m_p, m_loc)
            c_ref[hh] = jnp.exp(g_tot + m_p - m_new) * c_prev + jnp.exp(m_loc - m_new) * c_loc
            m_next.append(m_new)
        return tuple(m_next)

    lax.fori_loop(0, n_chunks, body, (jnp.zeros((1, 1), F32),) * heads)


def mlstm_chunk(qk, v, og, gates_t, g_head):
    b, t, _ = v.shape
    L = MLSTM_CHUNK
    nc = t // L
    hp = MLSTM_HEADS_PER_STEP
    groups = M_HEADS // hp
    qk_cols, v_cols = hp * M_DQK, hp * M_DV
    row_scratch = pltpu.VMEM((hp, nc, L), F32)
    return pl.pallas_call(
        _mlstm_chunk_kernel,
        grid=(b, groups),
        in_specs=[
            pl.BlockSpec((1, t, qk_cols), lambda i, p: (i, 0, p)),
            pl.BlockSpec((1, t, qk_cols), lambda i, p: (i, 0, groups + p)),
            pl.BlockSpec((1, t, v_cols), lambda i, p: (i, 0, p)),
            pl.BlockSpec((1, t, v_cols), lambda i, p: (i, 0, p)),
            pl.BlockSpec((1, 1, 2, hp, nc, L), lambda i, p: (i, p, 0, 0, 0, 0)),
            _resident((M_HEADS, M_DV)),
        ],
        out_specs=pl.BlockSpec((1, t, v_cols), lambda i, p: (i, 0, p)),
        out_shape=jax.ShapeDtypeStruct((b, t, M_HEADS * M_DV), BF16),
        scratch_shapes=[pltpu.VMEM((hp, 2 * M_DQK, 2 * M_DV), F32), row_scratch, row_scratch, row_scratch],
        compiler_params=_params("parallel", "parallel"),
        name="mlstm_chunk",
    )(qk, qk, v, og, gates_t, g_head.astype(F32))


def _kv_packs(y, gain, extra):
    lane = lax.broadcasted_iota(jnp.int32, (1, LANES), 1)
    lo = lane < HEAD_DIM
    kp, vp = [], []
    for h in range(N_KVH):
        yh = y[:, h * LANES:(h + 1) * LANES]
        ms = jnp.sum(jnp.where(lo, yh * yh, 0.0), axis=-1, keepdims=True) * (1.0 / HEAD_DIM)
        kp.append(jnp.where(lo, (yh * lax.rsqrt(ms + EPS)) * gain, extra))
        vp.append(jnp.where(lo, 1.0, yh))
    return jnp.concatenate(kp, axis=1), jnp.concatenate(vp, axis=1)


def _kv_proj_kernel(x_ref, g_ref, wc_ref, ws_ref, ww_ref, gs_ref, gw_ref,
                    cmp_ref, selk_ref, selv_ref, wink_ref, winv_ref, y_ref, *, seq_len):
    tm = x_ref.shape[0]
    xn = _rms(x_ref[...], g_ref[...]).astype(BF16)

    y = _dot(xn, wc_ref[...])
    n_slab = y.shape[1] // LANES
    for s in range(n_slab):
        y_ref[s] = y[:, s * LANES:(s + 1) * LANES]
    groups = tm // CMP_STRIDE
    lo = lax.broadcasted_iota(jnp.int32, (1, LANES), 1) < HEAD_DIM
    for s in range(n_slab):
        for rp in range(CMP_STRIDE // 2):
            even = y_ref[s, pl.ds(2 * rp, groups, stride=CMP_STRIDE), :]
            odd = y_ref[s, pl.ds(2 * rp + 1, groups, stride=CMP_STRIDE), :]
            cols = slice(rp * LANES, (rp + 1) * LANES)
            cmp_ref[2 * s, :, cols] = jnp.where(lo, even, pltpu.roll(odd, HEAD_DIM, 1))
            cmp_ref[2 * s + 1, :, cols] = jnp.where(lo, pltpu.roll(even, HEAD_DIM, 1), odd)

    pos = (pl.program_id(0) * tm) % seq_len + lax.broadcasted_iota(jnp.int32, (tm, 1), 0)
    lane = lax.broadcasted_iota(jnp.int32, (1, LANES), 1)
    block_bias = jnp.where(lane == HEAD_DIM + lax.shift_right_logical(pos, 6), SEL_BIAS, 0.0)
    kp, vp = _kv_packs(_dot(xn, ws_ref[...]), gs_ref[...], block_bias)
    selk_ref[...] = kp.astype(BF16)
    selv_ref[...] = vp.astype(BF16)
    kp, vp = _kv_packs(_dot(xn, ww_ref[...]), gw_ref[...], 0.0)
    wink_ref[...] = kp.astype(BF16)
    winv_ref[...] = vp.astype(BF16)


def _kv_pack_weights(wk, wv):
    d = wk.shape[0]
    wk = wk.reshape(d, N_KVH, HEAD_DIM)
    wv = wv.reshape(d, N_KVH, HEAD_DIM)
    return jnp.concatenate([wk, wv], axis=-1).reshape(d, N_KVH * LANES)


def _k_gain_row(gain):
    return jnp.concatenate([gain.astype(F32), jnp.ones((HEAD_DIM,), F32)]).reshape(1, LANES)


def kv_proj(h2, t, g, kv_w, k_norm):
    n, d = h2.shape
    wb = kv_w.astype(BF16)
    kc, vc, ks, vs, kw, vw = jnp.split(wb, 6, axis=1)
    wcmp = jnp.concatenate([kc, vc], axis=1)
    wsel = _kv_pack_weights(ks, vs)
    wwin = _kv_pack_weights(kw, vw)
    wd = N_KVH * LANES
    tm = TOKEN_TILE
    assert t % tm == 0 and t // SEL_BLOCK <= LANES - HEAD_DIM
    tile = lambda w: pl.BlockSpec((tm, w), lambda i: (i, 0))
    packed = jax.ShapeDtypeStruct((n, wd), BF16)
    feat = CMP_STRIDE * HEAD_DIM
    groups = tm // CMP_STRIDE
    return pl.pallas_call(
        functools.partial(_kv_proj_kernel, seq_len=t),
        grid=(n // tm,),
        in_specs=[tile(d), _resident((1, d)), _resident(wcmp.shape), _resident(wsel.shape),
                  _resident(wwin.shape), _resident((1, LANES)), _resident((1, LANES))],
        out_specs=[pl.BlockSpec((2 * N_KVH, groups, feat), lambda i: (0, i, 0))] + [tile(wd)] * 4,
        out_shape=[jax.ShapeDtypeStruct((2 * N_KVH, n // CMP_STRIDE, feat), F32), packed, packed, packed, packed],
        scratch_shapes=[pltpu.VMEM((2 * N_KVH * HEAD_DIM // LANES, tm, LANES), F32)],
        compiler_params=_params("parallel"),
        name="kv_proj",
    )(h2, g.reshape(1, d), wcmp, wsel, wwin, _k_gain_row(k_norm[1]), _k_gain_row(k_norm[2]))


def _cmp_kernel(xk_ref, xv_ref, pe_ref, w1_ref, w2k_ref, w2v_ref, gk_ref, ok_ref, ov_ref):
    def compress(x, kv, w2):
        u = _dot((x + pe_ref[kv, 0]).astype(BF16), w1_ref[kv, 0])
        v = _dot((x + pe_ref[kv, 1]).astype(BF16), w1_ref[kv, 1])
        nrow = v.shape[0]
        hid = u + pltpu.roll(v, nrow - 1, 0)
        return _dot((hid * jax.nn.sigmoid(hid)).astype(BF16), w2)

    yk = compress(xk_ref[0], 0, w2k_ref[...])
    yv = compress(xv_ref[0], 1, w2v_ref[...])
    ms = jnp.sum(yk * yk, axis=-1, keepdims=True) * (1.0 / HEAD_DIM)
    ok_ref[0, 0] = ((yk * lax.rsqrt(ms + EPS)) * gk_ref[...]).astype(BF16)
    lane = lax.broadcasted_iota(jnp.int32, (1, LANES), 1)
    ov_ref[0, 0] = jnp.where(lane < HEAD_DIM, 1.0, yv).astype(BF16)


def cmp_kv(cmp_x, b, t, cmp_pe, cmp_w1, cmp_w2, k_gain):
    ng = t // CMP_STRIDE
    feat = CMP_STRIDE * HEAD_DIM
    pe = cmp_pe.astype(F32).reshape(2, 2, 1, feat)
    w1 = cmp_w1.astype(BF16).reshape(2, 2, feat, CMP_HIDDEN)
    zeros = jnp.zeros((CMP_HIDDEN, HEAD_DIM), BF16)
    w2k = jnp.concatenate([cmp_w2[0].astype(BF16), zeros], axis=1)
    w2v = jnp.concatenate([zeros, cmp_w2[1].astype(BF16)], axis=1)
    ospec = pl.BlockSpec((1, 1, ng, LANES), lambda i, h: (i, h, 0, 0))
    packed = jax.ShapeDtypeStruct((b, N_KVH, ng, LANES), BF16)
    return pl.pallas_call(
        _cmp_kernel,
        grid=(b, N_KVH),
        in_specs=[pl.BlockSpec((1, ng, feat), lambda i, h: (h, i, 0)),
                  pl.BlockSpec((1, ng, feat), lambda i, h: (N_KVH + h, i, 0)),
                  _resident(pe.shape), _resident(w1.shape), _resident(w2k.shape),
                  _resident(w2v.shape), _resident((1, LANES))],
        out_specs=[ospec, ospec],
        out_shape=[packed, packed],
        compiler_params=_params("parallel", "parallel"),
        name="cmp_kv",
    )(cmp_x, cmp_x, pe, w1, w2k, w2v, _k_gain_row(k_gain))


def _nsa_proj_kernel(x_ref, g_ref, wq_ref, wg_ref, qg_ref, q_ref, gate_ref):
    xn = _rms(x_ref[...], g_ref[...]).astype(BF16)
    q = _dot(xn, wq_ref[...])
    lane = lax.broadcasted_iota(jnp.int32, (1, LANES), 1)
    lo = lane < HEAD_DIM
    outs = []
    for cb in range(q.shape[1] // LANES):
        y = q[:, cb * LANES:(cb + 1) * LANES]
        y2 = y * y
        ms_lo = jnp.sum(jnp.where(lo, y2, 0.0), axis=-1, keepdims=True) * (1.0 / HEAD_DIM)
        ms_hi = jnp.sum(jnp.where(lo, 0.0, y2), axis=-1, keepdims=True) * (1.0 / HEAD_DIM)
        scale = jnp.where(lo, lax.rsqrt(ms_lo + EPS), lax.rsqrt(ms_hi + EPS))
        outs.append(((y * scale) * qg_ref[...]) * (HEAD_DIM ** -0.5 * LOG2E))
    q_ref[...] = jnp.concatenate(outs, axis=1).astype(BF16)
    gate_ref[...] = jax.nn.sigmoid(_dot(xn, wg_ref[...])).astype(BF16)


def nsa_proj(h2, g, w_in, q_norm):
    n, d = h2.shape
    wb = w_in.astype(BF16)
    wq = wb[:, :N_QH * HEAD_DIM]
    wg = wb[:, N_QH * HEAD_DIM:].reshape(d, 3, N_KVH, GROUP).transpose(0, 2, 1, 3).reshape(d, N_KVH, 3 * GROUP)
    wg = jnp.pad(wg, ((0, 0), (0, 0), (0, LANES - 3 * GROUP))).reshape(d, N_KVH * LANES)
    qg = jnp.tile(q_norm.astype(F32), LANES // HEAD_DIM).reshape(1, LANES)
    tm = TOKEN_TILE
    tile = lambda w: pl.BlockSpec((tm, w), lambda i: (i, 0))
    return pl.pallas_call(
        _nsa_proj_kernel,
        grid=(n // tm,),
        in_specs=[tile(d), _resident((1, d)), _resident(wq.shape), _resident(wg.shape), _resident((1, LANES))],
        out_specs=[tile(N_QH * HEAD_DIM), tile(N_KVH * LANES)],
        out_shape=[jax.ShapeDtypeStruct((n, N_QH * HEAD_DIM), BF16),
                   jax.ShapeDtypeStruct((n, N_KVH * LANES), BF16)],
        compiler_params=_params("parallel"),
        name="nsa_proj",
    )(h2, g.reshape(1, d), wq, wg, qg)


def _masked_exp2(s3, mask):
    sm = jnp.where(mask[None], s3, NEG)
    mx = jnp.max(sm, axis=-1, keepdims=True)
    return jnp.exp2(sm - mx), mx > 0.5 * NEG


def _normalise(acc, values_lo):
    lo = lax.broadcasted_iota(jnp.int32, acc.shape, 1) < HEAD_DIM
    return acc / jnp.where(lo == values_lo, pltpu.roll(acc, HEAD_DIM, 1), 1.0)


def _nsa_attn_kernel(q_ref, gate_ref, cmpk_ref, cmpv_ref, selk_ref, selv_ref, wink_ref, winv_ref,
                     ov_ref, wbias_ref, spread_ref, o_ref, s_ref, mx_ref, acc_ref, *, tq, tk):
    rows = GROUP * tq

    def tile(tile_idx, carry):
        s0 = pl.multiple_of(tile_idx * tq, tq)

        lane = lax.broadcasted_iota(jnp.int32, (tq, LANES), 1)
        lo = lane < HEAD_DIM
        qs = []
        for g in range(GROUP):
            blk = q_ref[0, pl.ds(s0, tq), (g // 2) * LANES:(g // 2 + 1) * LANES].astype(F32)
            if g % 2 == 1:
                blk = pltpu.roll(blk, HEAD_DIM, 1)
            qs.append(jnp.where(lo, blk, 0.0))
        q4 = jnp.concatenate(qs, axis=0)
        qst = q4.astype(BF16)
        t_q = s0 + lax.broadcasted_iota(jnp.int32, (tq, 1), 0)

        kc = cmpk_ref[0, 0]
        n_cmp = kc.shape[0]
        cmp_end = lax.broadcasted_iota(jnp.int32, (1, n_cmp), 1) * CMP_STRIDE + (CMP_BLOCK - 1)
        e_c, ok_c = _masked_exp2(_dot_nt(qst, kc).reshape(GROUP, tq, n_cmp), cmp_end <= t_q)
        e_c = e_c.reshape(rows, n_cmp)
        ok_c = ok_c.reshape(rows, 1)
        both = _dot(e_c.astype(BF16), jnp.concatenate([cmpv_ref[0, 0], ov_ref[...]], axis=1))
        o_c = jnp.where(ok_c, _normalise(both[:, :LANES], False), 0.0)

        imp4 = jnp.where(ok_c, _normalise(both[:, LANES:], True), 0.0).reshape(GROUP, tq, LANES)
        imp = imp4[0] + imp4[1] + imp4[2] + imp4[3]

        n_sel = selk_ref.shape[1] // SEL_BLOCK
        valid = (lane * SEL_BLOCK <= t_q) & (lane < n_sel)
        cur = lax.shift_right_logical(t_q, 6)
        forced = (lane == 0) | (lane == cur) | (lane == cur - 1)
        score = jnp.where(forced & valid, FORCE_SCORE, jnp.where(valid, imp, -1.0))
        score = jnp.where(lane < n_sel, score, -2.0)
        s_t = score.T[:HEAD_DIM]
        sub = lax.broadcasted_iota(jnp.int32, (8, tq), 0)
        groups = [s_t[8 * v:8 * v + 8] for v in range(HEAD_DIM // 8)]
        ranks = [jnp.zeros((8, tq), F32) for _ in groups]
        for i in range(n_sel):
            s_i = s_t[i:i + 1, :]
            for v, s_v in enumerate(groups):
                if 8 * v + 7 < i:
                    ahead = s_i > s_v
                elif 8 * v > i:
                    ahead = s_i >= s_v
                else:
                    ahead = (s_i > s_v) | ((s_i == s_v) & (sub > i - 8 * v))
                ranks[v] = ranks[v] + jnp.where(ahead, 1.0, 0.0)
        rank = jnp.concatenate(ranks, axis=0)
        blk_t = lax.broadcasted_iota(jnp.int32, (HEAD_DIM, tq), 0)
        t_row = s0 + lax.broadcasted_iota(jnp.int32, (1, tq), 1)
        dropped_t = jnp.where((rank < SEL_TOPK) & (blk_t * SEL_BLOCK <= t_row), 0.0, 1.0)
        dropped = jnp.concatenate([jnp.zeros((HEAD_DIM, tq), F32), dropped_t], axis=0).T
        qx = (q4 + jnp.concatenate([dropped] * GROUP, axis=0)).astype(BF16)

        wk = WINDOW + tq
        w0 = pl.multiple_of(jnp.maximum(s0 - WINDOW, 0), tq)
        s_w = _dot_nt(qst, wink_ref[0, pl.ds(w0, wk), :]).reshape(GROUP, tq, wk)
        s_w = s_w + wbias_ref[jnp.minimum(tile_idx, WINDOW // tq)][None]
        e_w = jnp.exp2(s_w - jnp.max(s_w, axis=-1, keepdims=True))
        o_w = _normalise(_dot(e_w.reshape(rows, wk).astype(BF16), winv_ref[0, pl.ds(w0, wk), :]), False)

        nblk = tk // LANES
        last = s0 // tk
        mx_ref[...] = jnp.full(mx_ref.shape, NEG, F32)

        def scores(c, causal):
            base = pl.multiple_of(c * tk, tk)
            s = _dot_nt(qx, selk_ref[0, pl.ds(base, tk), :])
            if causal:
                kpos = base + lax.broadcasted_iota(jnp.int32, (1, tk), 1)
                s = jnp.where((kpos <= t_q)[None], s.reshape(GROUP, tq, tk), NEG).reshape(rows, tk)
            s_ref[c] = s
            m = mx_ref[...]
            for j in range(nblk):
                m = jnp.maximum(m, s[:, j * LANES:(j + 1) * LANES])
            mx_ref[...] = m

        def pass1(i, carry):
            scores(2 * i, causal=False)
            scores(2 * i + 1, causal=False)
            return carry

        lax.fori_loop(0, last // 2, pass1, 0)

        @pl.when(last % 2 == 1)
        def _():
            scores(last - 1, causal=False)

        scores(last, causal=True)
        mx_ref[...] = jnp.broadcast_to(jnp.max(mx_ref[...], axis=-1, keepdims=True), mx_ref.shape)

        acc_ref[...] = jnp.zeros_like(acc_ref)

        def weighted(c):
            base = pl.multiple_of(c * tk, tk)
            s = s_ref[c]
            m = mx_ref[...]
            p = jnp.concatenate([jnp.exp2(s[:, j * LANES:(j + 1) * LANES] - m) for j in range(nblk)], axis=1)
            return _dot(p.astype(BF16), selv_ref[0, pl.ds(base, tk), :])

        def pass2(i, carry):
            acc_ref[...] += weighted(2 * i) + weighted(2 * i + 1)
            return carry

        lax.fori_loop(0, (last + 1) // 2, pass2, 0)

        @pl.when(last % 2 == 0)
        def _():
            acc_ref[...] += weighted(last)

        o_s = _normalise(acc_ref[...], False)

        g_rep = _dot(gate_ref[0, pl.ds(s0, tq), :], spread_ref[...])

        def gate(branch):
            first = branch * GROUP
            return jnp.concatenate([g_rep[:, (first + g) * LANES:(first + g + 1) * LANES] for g in range(GROUP)], axis=0)

        o = (gate(0) * o_c + gate(1) * o_s + gate(2) * o_w).reshape(GROUP, tq, LANES)
        out01 = jnp.where(lo, pltpu.roll(o[0], HEAD_DIM, 1), o[1])
        out23 = jnp.where(lo, pltpu.roll(o[2], HEAD_DIM, 1), o[3])
        o_ref[0, pl.ds(s0, tq), :] = jnp.concatenate([out01, out23], axis=1).astype(BF16)
        return carry

    lax.fori_loop(0, q_ref.shape[1] // tq, tile, 0)


def _overlap_matrix(n_cmp_rows, t):
    n_sel = t // SEL_BLOCK
    c0 = np.arange(n_cmp_rows) * CMP_STRIDE
    s_0 = np.arange(n_sel) * SEL_BLOCK
    ov = (c0[:, None] < s_0[None, :] + SEL_BLOCK) & (c0[:, None] + CMP_BLOCK > s_0[None, :])
    ov[(t - CMP_BLOCK) // CMP_STRIDE + 1:] = False
    out = np.zeros((n_cmp_rows, LANES), np.float32)
    out[:, :n_sel] = ov
    out[:, HEAD_DIM:] = 1.0
    return jnp.asarray(out, BF16)


def _window_bias(tq):
    off = np.arange(WINDOW // tq + 1).reshape(-1, 1, 1) * tq
    i = np.arange(tq).reshape(1, tq, 1)
    j = np.arange(WINDOW + tq).reshape(1, 1, -1)
    visible = (j <= off + i) & (j > off + i - WINDOW)
    return jnp.asarray(np.where(visible, 0.0, NEG).astype(np.float32))


def _gate_spread():
    n_gate = 3 * GROUP
    src = np.arange(LANES).reshape(LANES, 1)
    dst_group = np.arange(n_gate * LANES).reshape(1, -1) // LANES
    return jnp.asarray((src == dst_group).astype(np.float32), BF16)


def nsa_attn(q, gates, kv):
    cmpk, cmpv, selk, selv, wink, winv = kv
    b, t, _ = q.shape
    tq = Q_TILE
    tk = min(SEL_KEY_CHUNK, t)
    n_cmp_rows = cmpk.shape[2]
    assert t // SEL_BLOCK <= LANES - HEAD_DIM and (t // SEL_BLOCK) % 8 == 0
    assert t % tk == 0 and tk % tq == 0 and t >= WINDOW + tq
    assert WINDOW % tq == 0
    ov = _overlap_matrix(n_cmp_rows, t)
    wbias = _window_bias(tq)
    spread = _gate_spread()
    gw = GROUP * HEAD_DIM
    rows = GROUP * tq
    kv_spec = pl.BlockSpec((1, t, LANES), lambda i, h: (i, 0, h))
    cmp_spec = pl.BlockSpec((1, 1, n_cmp_rows, LANES), lambda i, h: (i, h, 0, 0))
    return pl.pallas_call(
        functools.partial(_nsa_attn_kernel, tq=tq, tk=tk),
        grid=(b, N_KVH),
        in_specs=[
            pl.BlockSpec((1, t, gw), lambda i, h: (i, 0, h)),
            kv_spec,
            cmp_spec, cmp_spec, kv_spec, kv_spec, kv_spec, kv_spec,
            _resident(ov.shape), _resident(wbias.shape), _resident(spread.shape),
        ],
        out_specs=pl.BlockSpec((1, t, gw), lambda i, h: (i, 0, h)),
        out_shape=jax.ShapeDtypeStruct((b, t, N_QH * HEAD_DIM), BF16),
        scratch_shapes=[pltpu.VMEM((t // tk, rows, tk), F32), pltpu.VMEM((rows, LANES), F32),
                        pltpu.VMEM((rows, LANES), F32)],
        compiler_params=_params("parallel", "parallel"),
        name="nsa_attn",
    )(q, gates, cmpk, cmpv, selk, selv, wink, winv, ov, wbias, spread)


def mlstm_mixer(h2, b, t, g, w_in, b_if, g_head):
    qk, v, og, gates = mlstm_proj(h2, g, w_in, b_if)
    L = MLSTM_CHUNK
    hp = MLSTM_HEADS_PER_STEP
    gates_t = gates[:, :2 * M_HEADS].reshape(b, t // L, L, 2, M_HEADS // hp, hp).transpose(0, 4, 3, 5, 1, 2)
    hn = mlstm_chunk(qk.reshape(b, t, -1), v.reshape(b, t, -1), og.reshape(b, t, -1), gates_t, g_head)
    return hn.reshape(b * t, -1)


def nsa_shared_kv(h2, b, t, kv_norm, kv_w, cmp_pe, cmp_w1, cmp_w2, k_norm):
    cmp_x, *packs = kv_proj(h2, t, kv_norm, kv_w, k_norm)
    cmpk, cmpv = cmp_kv(cmp_x, b, t, cmp_pe, cmp_w1, cmp_w2, k_norm[0])
    return (cmpk, cmpv) + tuple(p.reshape(b, t, -1) for p in packs)


def nsa_mixer(h2, b, t, kv, g, w_in, q_norm):
    q, gates = nsa_proj(h2, g, w_in, q_norm)
    o = nsa_attn(q.reshape(b, t, -1), gates.reshape(b, t, -1), kv)
    return o.reshape(b * t, -1)


def kernel(x, ffn_norm, ffn_w_in, ffn_w_out, mix_norm, a_w_in, a_b_if, a_g_head, a_w_out, kv_norm, kv_w, cmp_pe, cmp_w1, cmp_w2, k_norm, b_w_in, b_q_norm, b_w_out):
    b, t, d = x.shape
    depth = ffn_norm.shape[0]
    n_a = a_w_in.shape[0]
    h = x.reshape(b * t, d)
    w_in_bf, w_out_bf = ffn_w_in, ffn_w_out
    kv = None
    for layer in range(depth):
        h = ffn(h, ffn_norm[layer, 0], w_in_bf, w_out_bf, layer, 0)
        if layer < n_a:
            mixed = mlstm_mixer(h, b, t, mix_norm[layer], a_w_in[layer], a_b_if[layer], a_g_head[layer])
            w_mix = a_w_out[layer]
        else:
            j = layer - n_a
            mixed = nsa_mixer(h, b, t, kv, mix_norm[layer], b_w_in[j], b_q_norm[j])
            w_mix = b_w_out[j]
        h = ffn(h, ffn_norm[layer, 1], w_in_bf, w_out_bf, layer, 1, mixer=(mixed, w_mix))
        if layer == n_a - 1:
            kv = nsa_shared_kv(h, b, t, kv_norm, kv_w, cmp_pe, cmp_w1, cmp_w2, k_norm)
    return h.reshape(b, t, d)
```

```python
import functools

import jax
import jax.numpy as jnp
import numpy as np
from jax import lax
from jax.experimental import pallas as pl
from jax.experimental.pallas import tpu as pltpu

F32 = jnp.float32
BF16 = jnp.bfloat16

D_MODEL = 1024
D_FF = 2816
EPS = 1e-6
NEG = -1e30

M_HEADS = 8
M_DV = 128
M_DQK = 64

N_QH = 16
N_KVH = 4
GROUP = N_QH // N_KVH
HEAD_DIM = 64
CMP_BLOCK = 32
CMP_STRIDE = 16
CMP_HIDDEN = 256
SEL_BLOCK = 64
SEL_TOPK = 16
WINDOW = 512
FORCE_SCORE = 1e4
LOG2E = 1.4426950408889634
SEL_BIAS = -2.0 ** 100

LANES = 128
VMEM_LIMIT = 56 * 1024 * 1024

TOKEN_TILE = 512
FFN_CHUNK = 256
MLSTM_CHUNK = 256
MLSTM_HEADS_PER_STEP = 4
Q_TILE = 256
SEL_KEY_CHUNK = 512

_NT = (((1,), (1,)), ((), ()))


def _params(*sem):
    return pltpu.CompilerParams(dimension_semantics=sem, vmem_limit_bytes=VMEM_LIMIT)


def _rms(x, g):
    ms = jnp.mean(x * x, axis=-1, keepdims=True)
    return (x * lax.rsqrt(ms + EPS)) * g


def _dot(a, b):
    return jnp.dot(a, b, preferred_element_type=F32)


def _dot_nt(a, b, precision=None):
    return lax.dot_general(a, b, _NT, precision=precision, preferred_element_type=F32)


def _resident(shape):
    nd = len(shape)
    return pl.BlockSpec(shape, lambda *_: (0,) * nd)


def _ffn_kernel(*refs, mixer_proj):
    if mixer_proj:
        x_ref, a_ref, wmix_ref, g_ref, win_ref, wout_ref, o_ref, xn_ref, acc_ref, x_scr = refs
        x_scr[...] = x_ref[...] + _dot(a_ref[...], wmix_ref[...])
        x_ref = x_scr
    else:
        x_ref, g_ref, win_ref, wout_ref, o_ref, xn_ref, acc_ref = refs
    f = wout_ref.shape[0]
    x = x_ref[...]
    xn_ref[...] = _rms(x, g_ref[...]).astype(BF16)
    for c in range(f // FFN_CHUNK):
        cols = slice(c * FFN_CHUNK, (c + 1) * FFN_CHUNK)
        up_cols = slice(f + c * FFN_CHUNK, f + (c + 1) * FFN_CHUNK)
        xn = xn_ref[...]
        a = _dot(xn, win_ref[:, cols].astype(BF16))
        b = _dot(xn, win_ref[:, up_cols].astype(BF16))
        hid = ((a * jax.nn.sigmoid(a)) * b).astype(BF16)
        part = _dot(hid, wout_ref[cols, :].astype(BF16))
        if c == 0:
            acc_ref[...] = part
        else:
            acc_ref[...] += part
    o_ref[...] = x_ref[...] + 0.5 * acc_ref[...]


def ffn(x2, g, w_in_all, w_out_all, layer, j, mixer=None):
    n, d = x2.shape
    f = w_out_all.shape[2]
    assert f % FFN_CHUNK == 0
    tm = TOKEN_TILE
    tile = lambda w: pl.BlockSpec((tm, w), lambda i: (i, 0))
    ins, specs, scratch = [x2], [tile(d)], [pltpu.VMEM((tm, d), BF16), pltpu.VMEM((tm, d), F32)]
    if mixer is not None:
        a2, w_mix = mixer
        ins += [a2, w_mix.astype(BF16)]
        specs += [tile(a2.shape[1]), _resident(w_mix.shape)]
        scratch.append(pltpu.VMEM((tm, d), F32))
    return pl.pallas_call(
        functools.partial(_ffn_kernel, mixer_proj=mixer is not None),
        grid=(n // tm,),
        in_specs=specs + [
            _resident((1, d)),
            pl.BlockSpec((None, None, d, 2 * f), lambda i: (layer, j, 0, 0), pipeline_mode=pl.Buffered(1)),
            pl.BlockSpec((None, None, f, d), lambda i: (layer, j, 0, 0), pipeline_mode=pl.Buffered(1)),
        ],
        out_specs=tile(d),
        out_shape=jax.ShapeDtypeStruct((n, d), F32),
        scratch_shapes=scratch,
        compiler_params=_params("parallel"),
        name="ffn",
    )(*ins, g.reshape(1, d), w_in_all, w_out_all)


def _mlstm_proj_kernel(x_ref, g_ref, wqk_ref, wv_ref, wog_ref, wg_ref, bg_ref,
                       qk_ref, v_ref, og_ref, gate_ref):
    xn = _rms(x_ref[...], g_ref[...]).astype(BF16)
    qk = _dot(xn, wqk_ref[...])
    is_q = lax.broadcasted_iota(jnp.int32, (1, qk.shape[1]), 1) < qk.shape[1] // 2
    qk_ref[...] = (qk * jnp.where(is_q, M_DQK ** -0.5, 1.0)).astype(BF16)
    v_ref[...] = _dot(xn, wv_ref[...]).astype(BF16)
    og_ref[...] = jax.nn.sigmoid(_dot(xn, wog_ref[...])).astype(BF16)
    gate_ref[...] = _dot(xn, wg_ref[...]) + bg_ref[...]


def mlstm_proj(h2, g, w_in, b_if):
    n, d = h2.shape
    qk_w, v_w = M_HEADS * M_DQK, M_HEADS * M_DV
    wb = w_in.astype(BF16)
    wqk = wb[:, :2 * qk_w]
    wv = wb[:, 2 * qk_w:2 * qk_w + v_w]
    wgate = jnp.pad(wb[:, 2 * qk_w + v_w:2 * qk_w + v_w + 2 * M_HEADS], ((0, 0), (0, LANES - 2 * M_HEADS)))
    wog = wb[:, 2 * qk_w + v_w + 2 * M_HEADS:]
    bg = jnp.pad(b_if.astype(F32).reshape(1, 2 * M_HEADS), ((0, 0), (0, LANES - 2 * M_HEADS)))
    tm = TOKEN_TILE
    tile = lambda w: pl.BlockSpec((tm, w), lambda i: (i, 0))
    return pl.pallas_call(
        _mlstm_proj_kernel,
        grid=(n // tm,),
        in_specs=[tile(d), _resident((1, d)), _resident(wqk.shape), _resident(wv.shape),
                  _resident(wog.shape), _resident(wgate.shape), _resident(bg.shape)],
        out_specs=[tile(2 * qk_w), tile(v_w), tile(d), tile(LANES)],
        out_shape=[jax.ShapeDtypeStruct((n, 2 * qk_w), BF16), jax.ShapeDtypeStruct((n, v_w), BF16),
                   jax.ShapeDtypeStruct((n, d), BF16), jax.ShapeDtypeStruct((n, LANES), F32)],
        compiler_params=_params("parallel"),
        name="mlstm_proj",
    )(h2, g.reshape(1, d), wqk, wv, wog, wgate, bg)


def _log_sigmoid(x):
    return jnp.minimum(x, 0.0) - jnp.log1p(jnp.exp(-jnp.abs(x)))


def _mlstm_chunk_kernel(q_ref, k_ref, v_ref, og_ref, gate_ref, ghead_ref, o_ref, c_ref, g_r_ref, b_r_ref, cm_r_ref):
    heads = c_ref.shape[0]
    first_head = pl.program_id(1) * heads
    L = MLSTM_CHUNK
    n_chunks = q_ref.shape[1] // L

    r_i = lax.broadcasted_iota(jnp.int32, (L, L), 0)
    c_i = lax.broadcasted_iota(jnp.int32, (L, L), 1)
    causal = c_i <= r_i
    tril = causal.astype(F32)
    lane = lax.broadcasted_iota(jnp.int32, (L, LANES), 1)
    ones_blk = jnp.ones((L, LANES), BF16)

    c_ref[...] = jnp.zeros_like(c_ref)

    t_i = lax.broadcasted_iota(jnp.int32, (n_chunks, L), 1)
    for hh in range(heads):
        b = _dot_nt(_log_sigmoid(gate_ref[0, 0, 1, hh]), tril, lax.Precision.HIGHEST)
        g = gate_ref[0, 0, 0, hh] - b
        cm = g
        shift = 1
        while shift < L:
            cm = jnp.maximum(cm, jnp.where(t_i >= shift, pltpu.roll(cm, shift, 1), -jnp.inf))
            shift *= 2
        b_r_ref[hh] = b
        g_r_ref[hh] = g
        cm_r_ref[hh] = cm

    def column(row):
        return jnp.broadcast_to(row, (LANES, L)).T

    def body(c, m_prev):
        base = pl.multiple_of(c * L, L)
        m_next = []
        for hh in range(heads):
            head = first_head + hh
            pair_cols = slice((hh // 2) * LANES, (hh // 2 + 1) * LANES)
            in_head = (lane >= (hh % 2) * M_DQK) & (lane < (hh % 2 + 1) * M_DQK)
            qh = q_ref[0, pl.ds(base, L), pair_cols]
            kh = jnp.where(in_head, k_ref[0, pl.ds(base, L), pair_cols], 0)
            v_aug = jnp.concatenate([v_ref[0, pl.ds(base, L), hh * M_DV:(hh + 1) * M_DV], ones_blk], axis=1)

            g_r = g_r_ref[hh, pl.ds(c, 1), :]
            b_r = b_r_ref[hh, pl.ds(c, 1), :]
            g_tot = b_r[:, L - 1:L]
            m_p = m_prev[hh]
            c_prev = c_ref[hh]

            a_r = g_tot + g_r
            m_loc = jnp.max(a_r, axis=1, keepdims=True)
            kw_t = (kh.astype(F32).T * jnp.exp(a_r - m_loc)).astype(BF16)
            c_loc = _dot(kw_t, v_aug)

            mm = jnp.maximum(m_p, column(cm_r_ref[hh, pl.ds(c, 1), :]))
            mm_wide = jnp.concatenate([mm] * (L // LANES), axis=1)
            w = jnp.where(causal, jnp.exp(g_r - mm_wide), 0.0) * _dot_nt(qh, kh)
            num = jnp.concatenate([jnp.exp(m_p - mm)] * 2, axis=1) * _dot(qh, c_prev.astype(BF16)) \
                + _dot(w.astype(BF16), v_aug)
            floor = jnp.exp(-(column(b_r) + mm))
            h_out = num[:, :M_DV] / jnp.maximum(jnp.abs(num[:, M_DV:]), floor)
            h_out = _rms(h_out, ghead_ref[pl.ds(head, 1), :])
            cols = slice(hh * M_DV, (hh + 1) * M_DV)
            o_ref[0, pl.ds(base, L), cols] = (h_out * og_ref[0, pl.ds(base, L), cols]).astype(BF16)

            m_new = jnp.maximum(g_tot + m_p, m_loc)
            c_ref[hh] = jnp.exp(g_tot + m_p - m_new) * c_prev + jnp.exp(m_loc - m_new) * c_loc
            m_next.append(m_new)
        return tuple(m_next)

    lax.fori_loop(0, n_chunks, body, (jnp.zeros((1, 1), F32),) * heads)


def mlstm_chunk(qk, v, og, gates_t, g_head):
    b, t, _ = v.shape
    L = MLSTM_CHUNK
    nc = t // L
    hp = MLSTM_HEADS_PER_STEP
    groups = M_HEADS // hp
    qk_cols, v_cols = hp * M_DQK, hp * M_DV
    row_scratch = pltpu.VMEM((hp, nc, L), F32)
    return pl.pallas_call(
        _mlstm_chunk_kernel,
        grid=(b, groups),
        in_specs=[
            pl.BlockSpec((1, t, qk_cols), lambda i, p: (i, 0, p)),
            pl.BlockSpec((1, t, qk_cols), lambda i, p: (i, 0, groups + p)),
            pl.BlockSpec((1, t, v_cols), lambda i, p: (i, 0, p)),
            pl.BlockSpec((1, t, v_cols), lambda i, p: (i, 0, p)),
            pl.BlockSpec((1, 1, 2, hp, nc, L), lambda i, p: (i, p, 0, 0, 0, 0)),
            _resident((M_HEADS, M_DV)),
        ],
        out_specs=pl.BlockSpec((1, t, v_cols), lambda i, p: (i, 0, p)),
        out_shape=jax.ShapeDtypeStruct((b, t, M_HEADS * M_DV), BF16),
        scratch_shapes=[pltpu.VMEM((hp, 2 * M_DQK, 2 * M_DV), F32), row_scratch, row_scratch, row_scratch],
        compiler_params=_params("parallel", "parallel"),
        name="mlstm_chunk",
    )(qk, qk, v, og, gates_t, g_head.astype(F32))


def _kv_packs(y, gain, extra):
    lane = lax.broadcasted_iota(jnp.int32, (1, LANES), 1)
    lo = lane < HEAD_DIM
    kp, vp = [], []
    for h in range(N_KVH):
        yh = y[:, h * LANES:(h + 1) * LANES]
        ms = jnp.sum(jnp.where(lo, yh * yh, 0.0), axis=-1, keepdims=True) * (1.0 / HEAD_DIM)
        kp.append(jnp.where(lo, (yh * lax.rsqrt(ms + EPS)) * gain, extra))
        vp.append(jnp.where(lo, 1.0, yh))
    return jnp.concatenate(kp, axis=1), jnp.concatenate(vp, axis=1)


def _kv_proj_kernel(x_ref, g_ref, wc_ref, ws_ref, ww_ref, gs_ref, gw_ref,
                    cmp_ref, selk_ref, selv_ref, wink_ref, winv_ref, y_ref, *, seq_len):
    tm = x_ref.shape[0]
    xn = _rms(x_ref[...], g_ref[...]).astype(BF16)

    y = _dot(xn, wc_ref[...])
    n_slab = y.shape[1] // LANES
    for s in range(n_slab):
        y_ref[s] = y[:, s * LANES:(s + 1) * LANES]
    groups = tm // CMP_STRIDE
    lo = lax.broadcasted_iota(jnp.int32, (1, LANES), 1) < HEAD_DIM
    for s in range(n_slab):
        for rp in range(CMP_STRIDE // 2):
            even = y_ref[s, pl.ds(2 * rp, groups, stride=CMP_STRIDE), :]
            odd = y_ref[s, pl.ds(2 * rp + 1, groups, stride=CMP_STRIDE), :]
            cols = slice(rp * LANES, (rp + 1) * LANES)
            cmp_ref[2 * s, :, cols] = jnp.where(lo, even, pltpu.roll(odd, HEAD_DIM, 1))
            cmp_ref[2 * s + 1, :, cols] = jnp.where(lo, pltpu.roll(even, HEAD_DIM, 1), odd)

    pos = (pl.program_id(0) * tm) % seq_len + lax.broadcasted_iota(jnp.int32, (tm, 1), 0)
    lane = lax.broadcasted_iota(jnp.int32, (1, LANES), 1)
    block_bias = jnp.where(lane == HEAD_DIM + lax.shift_right_logical(pos, 6), SEL_BIAS, 0.0)
    kp, vp = _kv_packs(_dot(xn, ws_ref[...]), gs_ref[...], block_bias)
    selk_ref[...] = kp.astype(BF16)
    selv_ref[...] = vp.astype(BF16)
    kp, vp = _kv_packs(_dot(xn, ww_ref[...]), gw_ref[...], 0.0)
    wink_ref[...] = kp.astype(BF16)
    winv_ref[...] = vp.astype(BF16)


def _kv_pack_weights(wk, wv):
    d = wk.shape[0]
    wk = wk.reshape(d, N_KVH, HEAD_DIM)
    wv = wv.reshape(d, N_KVH, HEAD_DIM)
    return jnp.concatenate([wk, wv], axis=-1).reshape(d, N_KVH * LANES)


def _k_gain_row(gain):
    return jnp.concatenate([gain.astype(F32), jnp.ones((HEAD_DIM,), F32)]).reshape(1, LANES)


def kv_proj(h2, t, g, kv_w, k_norm):
    n, d = h2.shape
    wb = kv_w.astype(BF16)
    kc, vc, ks, vs, kw, vw = jnp.split(wb, 6, axis=1)
    wcmp = jnp.concatenate([kc, vc], axis=1)
    wsel = _kv_pack_weights(ks, vs)
    wwin = _kv_pack_weights(kw, vw)
    wd = N_KVH * LANES
    tm = TOKEN_TILE
    assert t % tm == 0 and t // SEL_BLOCK <= LANES - HEAD_DIM
    tile = lambda w: pl.BlockSpec((tm, w), lambda i: (i, 0))
    packed = jax.ShapeDtypeStruct((n, wd), BF16)
    feat = CMP_STRIDE * HEAD_DIM
    groups = tm // CMP_STRIDE
    return pl.pallas_call(
        functools.partial(_kv_proj_kernel, seq_len=t),
        grid=(n // tm,),
        in_specs=[tile(d), _resident((1, d)), _resident(wcmp.shape), _resident(wsel.shape),
                  _resident(wwin.shape), _resident((1, LANES)), _resident((1, LANES))],
        out_specs=[pl.BlockSpec((2 * N_KVH, groups, feat), lambda i: (0, i, 0))] + [tile(wd)] * 4,
        out_shape=[jax.ShapeDtypeStruct((2 * N_KVH, n // CMP_STRIDE, feat), F32), packed, packed, packed, packed],
        scratch_shapes=[pltpu.VMEM((2 * N_KVH * HEAD_DIM // LANES, tm, LANES), F32)],
        compiler_params=_params("parallel"),
        name="kv_proj",
    )(h2, g.reshape(1, d), wcmp, wsel, wwin, _k_gain_row(k_norm[1]), _k_gain_row(k_norm[2]))


def _cmp_kernel(xk_ref, xv_ref, pe_ref, w1_ref, w2k_ref, w2v_ref, gk_ref, ok_ref, ov_ref):
    def compress(x, kv, w2):
        u = _dot((x + pe_ref[kv, 0]).astype(BF16), w1_ref[kv, 0])
        v = _dot((x + pe_ref[kv, 1]).astype(BF16), w1_ref[kv, 1])
        nrow = v.shape[0]
        hid = u + pltpu.roll(v, nrow - 1, 0)
        return _dot((hid * jax.nn.sigmoid(hid)).astype(BF16), w2)

    yk = compress(xk_ref[0], 0, w2k_ref[...])
    yv = compress(xv_ref[0], 1, w2v_ref[...])
    ms = jnp.sum(yk * yk, axis=-1, keepdims=True) * (1.0 / HEAD_DIM)
    ok_ref[0, 0] = ((yk * lax.rsqrt(ms + EPS)) * gk_ref[...]).astype(BF16)
    lane = lax.broadcasted_iota(jnp.int32, (1, LANES), 1)
    ov_ref[0, 0] = jnp.where(lane < HEAD_DIM, 1.0, yv).astype(BF16)


def cmp_kv(cmp_x, b, t, cmp_pe, cmp_w1, cmp_w2, k_gain):
    ng = t // CMP_STRIDE
    feat = CMP_STRIDE * HEAD_DIM
    pe = cmp_pe.astype(F32).reshape(2, 2, 1, feat)
    w1 = cmp_w1.astype(BF16).reshape(2, 2, feat, CMP_HIDDEN)
    zeros = jnp.zeros((CMP_HIDDEN, HEAD_DIM), BF16)
    w2k = jnp.concatenate([cmp_w2[0].astype(BF16), zeros], axis=1)
    w2v = jnp.concatenate([zeros, cmp_w2[1].astype(BF16)], axis=1)
    ospec = pl.BlockSpec((1, 1, ng, LANES), lambda i, h: (i, h, 0, 0))
    packed = jax.ShapeDtypeStruct((b, N_KVH, ng, LANES), BF16)
    return pl.pallas_call(
        _cmp_kernel,
        grid=(b, N_KVH),
        in_specs=[pl.BlockSpec((1, ng, feat), lambda i, h: (h, i, 0)),
                  pl.BlockSpec((1, ng, feat), lambda i, h: (N_KVH + h, i, 0)),
                  _resident(pe.shape), _resident(w1.shape), _resident(w2k.shape),
                  _resident(w2v.shape), _resident((1, LANES))],
        out_specs=[ospec, ospec],
        out_shape=[packed, packed],
        compiler_params=_params("parallel", "parallel"),
        name="cmp_kv",
    )(cmp_x, cmp_x, pe, w1, w2k, w2v, _k_gain_row(k_gain))


def _nsa_proj_kernel(x_ref, g_ref, wq_ref, wg_ref, qg_ref, q_ref, gate_ref):
    xn = _rms(x_ref[...], g_ref[...]).astype(BF16)
    q = _dot(xn, wq_ref[...])
    lane = lax.broadcasted_iota(jnp.int32, (1, LANES), 1)
    lo = lane < HEAD_DIM
    outs = []
    for cb in range(q.shape[1] // LANES):
        y = q[:, cb * LANES:(cb + 1) * LANES]
        y2 = y * y
        ms_lo = jnp.sum(jnp.where(lo, y2, 0.0), axis=-1, keepdims=True) * (1.0 / HEAD_DIM)
        ms_hi = jnp.sum(jnp.where(lo, 0.0, y2), axis=-1, keepdims=True) * (1.0 / HEAD_DIM)
        scale = jnp.where(lo, lax.rsqrt(ms_lo + EPS), lax.rsqrt(ms_hi + EPS))
        outs.append(((y * scale) * qg_ref[...]) * (HEAD_DIM ** -0.5 * LOG2E))
    q_ref[...] = jnp.concatenate(outs, axis=1).astype(BF16)
    gate_ref[...] = jax.nn.sigmoid(_dot(xn, wg_ref[...])).astype(BF16)


def nsa_proj(h2, g, w_in, q_norm):
    n, d = h2.shape
    wb = w_in.astype(BF16)
    wq = wb[:, :N_QH * HEAD_DIM]
    wg = wb[:, N_QH * HEAD_DIM:].reshape(d, 3, N_KVH, GROUP).transpose(0, 2, 1, 3).reshape(d, N_KVH * 3 * GROUP)
    wg = jnp.pad(wg, ((0, 0), (0, LANES - N_KVH * 3 * GROUP)))
    qg = jnp.tile(q_norm.astype(F32), LANES // HEAD_DIM).reshape(1, LANES)
    tm = TOKEN_TILE
    tile = lambda w: pl.BlockSpec((tm, w), lambda i: (i, 0))
    return pl.pallas_call(
        _nsa_proj_kernel,
        grid=(n // tm,),
        in_specs=[tile(d), _resident((1, d)), _resident(wq.shape), _resident(wg.shape), _resident((1, LANES))],
        out_specs=[tile(N_QH * HEAD_DIM), tile(LANES)],
        out_shape=[jax.ShapeDtypeStruct((n, N_QH * HEAD_DIM), BF16),
                   jax.ShapeDtypeStruct((n, LANES), BF16)],
        compiler_params=_params("parallel"),
        name="nsa_proj",
    )(h2, g.reshape(1, d), wq, wg, qg)


def _masked_exp2(s3, mask):
    sm = jnp.where(mask[None], s3, NEG)
    mx = jnp.max(sm, axis=-1, keepdims=True)
    return jnp.exp2(sm - mx), mx > 0.5 * NEG


def _normalise(acc, values_lo):
    lo = lax.broadcasted_iota(jnp.int32, acc.shape, 1) < HEAD_DIM
    return acc / jnp.where(lo == values_lo, pltpu.roll(acc, HEAD_DIM, 1), 1.0)


def _nsa_attn_kernel(q_ref, gate_ref, cmpk_ref, cmpv_ref, selk_ref, selv_ref, wink_ref, winv_ref,
                     ov_ref, wbias_ref, spread_ref, o_ref, s_ref, mx_ref, acc_ref, *, tq, tk):
    rows = GROUP * tq

    def tile(tile_idx, carry):
        s0 = pl.multiple_of(tile_idx * tq, tq)

        lane = lax.broadcasted_iota(jnp.int32, (tq, LANES), 1)
        lo = lane < HEAD_DIM
        qs = []
        for g in range(GROUP):
            blk = q_ref[0, pl.ds(s0, tq), (g // 2) * LANES:(g // 2 + 1) * LANES].astype(F32)
            if g % 2 == 1:
                blk = pltpu.roll(blk, HEAD_DIM, 1)
            qs.append(jnp.where(lo, blk, 0.0))
        q4 = jnp.concatenate(qs, axis=0)
        qst = q4.astype(BF16)
        t_q = s0 + lax.broadcasted_iota(jnp.int32, (tq, 1), 0)

        kc = cmpk_ref[0, 0]
        n_cmp = kc.shape[0]
        cmp_end = lax.broadcasted_iota(jnp.int32, (1, n_cmp), 1) * CMP_STRIDE + (CMP_BLOCK - 1)
        e_c, ok_c = _masked_exp2(_dot_nt(qst, kc).reshape(GROUP, tq, n_cmp), cmp_end <= t_q)
        e_c = e_c.reshape(rows, n_cmp)
        ok_c = ok_c.reshape(rows, 1)
        both = _dot(e_c.astype(BF16), jnp.concatenate([cmpv_ref[0, 0], ov_ref[...]], axis=1))
        o_c = jnp.where(ok_c, _normalise(both[:, :LANES], False), 0.0)

        imp4 = jnp.where(ok_c, _normalise(both[:, LANES:], True), 0.0).reshape(GROUP, tq, LANES)
        imp = imp4[0] + imp4[1] + imp4[2] + imp4[3]

        n_sel = selk_ref.shape[1] // SEL_BLOCK
        valid = (lane * SEL_BLOCK <= t_q) & (lane < n_sel)
        cur = lax.shift_right_logical(t_q, 6)
        forced = (lane == 0) | (lane == cur) | (lane == cur - 1)
        score = jnp.where(forced & valid, FORCE_SCORE, jnp.where(valid, imp, -1.0))
        score = jnp.where(lane < n_sel, score, -2.0)
        s_t = score.T[:HEAD_DIM]
        sub = lax.broadcasted_iota(jnp.int32, (8, tq), 0)
        groups = [s_t[8 * v:8 * v + 8] for v in range(HEAD_DIM // 8)]
        ranks = [jnp.zeros((8, tq), F32) for _ in groups]
        for i in range(n_sel):
            s_i = s_t[i:i + 1, :]
            for v, s_v in enumerate(groups):
                if 8 * v + 7 < i:
                    ahead = s_i > s_v
                elif 8 * v > i:
                    ahead = s_i >= s_v
                else:
                    ahead = (s_i > s_v) | ((s_i == s_v) & (sub > i - 8 * v))
                ranks[v] = ranks[v] + jnp.where(ahead, 1.0, 0.0)
        rank = jnp.concatenate(ranks, axis=0)
        blk_t = lax.broadcasted_iota(jnp.int32, (HEAD_DIM, tq), 0)
        t_row = s0 + lax.broadcasted_iota(jnp.int32, (1, tq), 1)
        dropped_t = jnp.where((rank < SEL_TOPK) & (blk_t * SEL_BLOCK <= t_row), 0.0, 1.0)
        dropped = jnp.concatenate([jnp.zeros((HEAD_DIM, tq), F32), dropped_t], axis=0).T
        qx = (q4 + jnp.concatenate([dropped] * GROUP, axis=0)).astype(BF16)

        wk = WINDOW + tq
        w0 = pl.multiple_of(jnp.maximum(s0 - WINDOW, 0), tq)
        s_w = _dot_nt(qst, wink_ref[0, pl.ds(w0, wk), :]).reshape(GROUP, tq, wk)
        s_w = s_w + wbias_ref[jnp.minimum(tile_idx, WINDOW // tq)][None]
        e_w = jnp.exp2(s_w - jnp.max(s_w, axis=-1, keepdims=True))
        o_w = _normalise(_dot(e_w.reshape(rows, wk).astype(BF16), winv_ref[0, pl.ds(w0, wk), :]), False)

        nblk = tk // LANES
        last = s0 // tk
        mx_ref[...] = jnp.full(mx_ref.shape, NEG, F32)

        def scores(c, causal):
            base = pl.multiple_of(c * tk, tk)
            s = _dot_nt(qx, selk_ref[0, pl.ds(base, tk), :])
            if causal:
                kpos = base + lax.broadcasted_iota(jnp.int32, (1, tk), 1)
                s = jnp.where((kpos <= t_q)[None], s.reshape(GROUP, tq, tk), NEG).reshape(rows, tk)
            s_ref[c] = s
            m = mx_ref[...]
            for j in range(nblk):
                m = jnp.maximum(m, s[:, j * LANES:(j + 1) * LANES])
            mx_ref[...] = m

        def pass1(i, carry):
            for u in range(4):
                scores(4 * i + u, causal=False)
            return carry

        lax.fori_loop(0, last // 4, pass1, 0)
        done = 4 * (last // 4)

        @pl.when(last - done >= 2)
        def _():
            scores(done, causal=False)
            scores(done + 1, causal=False)

        @pl.when((last - done) % 2 == 1)
        def _():
            scores(last - 1, causal=False)

        scores(last, causal=True)
        mx_ref[...] = jnp.broadcast_to(jnp.max(mx_ref[...], axis=-1, keepdims=True), mx_ref.shape)

        acc_ref[...] = jnp.zeros_like(acc_ref)

        def weighted(c):
            base = pl.multiple_of(c * tk, tk)
            s = s_ref[c]
            m = mx_ref[...]
            p = jnp.concatenate([jnp.exp2(s[:, j * LANES:(j + 1) * LANES] - m) for j in range(nblk)], axis=1)
            return _dot(p.astype(BF16), selv_ref[0, pl.ds(base, tk), :])

        def pass2(i, carry):
            acc_ref[...] += (weighted(4 * i) + weighted(4 * i + 1)) + (weighted(4 * i + 2) + weighted(4 * i + 3))
            return carry

        n_chunks = last + 1
        lax.fori_loop(0, n_chunks // 4, pass2, 0)
        done2 = 4 * (n_chunks // 4)

        @pl.when(n_chunks - done2 >= 2)
        def _():
            acc_ref[...] += weighted(done2) + weighted(done2 + 1)

        @pl.when((n_chunks - done2) % 2 == 1)
        def _():
            acc_ref[...] += weighted(last)

        o_s = _normalise(acc_ref[...], False)

        g_rep = _dot(gate_ref[0, pl.ds(s0, tq), :], spread_ref[pl.program_id(1)])

        def gate(branch):
            first = branch * GROUP
            return jnp.concatenate([g_rep[:, (first + g) * LANES:(first + g + 1) * LANES] for g in range(GROUP)], axis=0)

        o = (gate(0) * o_c + gate(1) * o_s + gate(2) * o_w).reshape(GROUP, tq, LANES)
        out01 = jnp.where(lo, pltpu.roll(o[0], HEAD_DIM, 1), o[1])
        out23 = jnp.where(lo, pltpu.roll(o[2], HEAD_DIM, 1), o[3])
        o_ref[0, pl.ds(s0, tq), :] = jnp.concatenate([out01, out23], axis=1).astype(BF16)
        return carry

    lax.fori_loop(0, q_ref.shape[1] // tq, tile, 0)


def _overlap_matrix(n_cmp_rows, t):
    n_sel = t // SEL_BLOCK
    c0 = np.arange(n_cmp_rows) * CMP_STRIDE
    s_0 = np.arange(n_sel) * SEL_BLOCK
    ov = (c0[:, None] < s_0[None, :] + SEL_BLOCK) & (c0[:, None] + CMP_BLOCK > s_0[None, :])
    ov[(t - CMP_BLOCK) // CMP_STRIDE + 1:] = False
    out = np.zeros((n_cmp_rows, LANES), np.float32)
    out[:, :n_sel] = ov
    out[:, HEAD_DIM:] = 1.0
    return jnp.asarray(out, BF16)


def _window_bias(tq):
    off = np.arange(WINDOW // tq + 1).reshape(-1, 1, 1) * tq
    i = np.arange(tq).reshape(1, tq, 1)
    j = np.arange(WINDOW + tq).reshape(1, 1, -1)
    visible = (j <= off + i) & (j > off + i - WINDOW)
    return jnp.asarray(np.where(visible, 0.0, NEG).astype(np.float32))


def _gate_spread():
    n_gate = 3 * GROUP
    head = np.arange(N_KVH).reshape(N_KVH, 1, 1)
    src = np.arange(LANES).reshape(1, LANES, 1)
    dst_group = np.arange(n_gate * LANES).reshape(1, 1, -1) // LANES
    return jnp.asarray((src == n_gate * head + dst_group).astype(np.float32), BF16)


def nsa_attn(q, gates, kv):
    cmpk, cmpv, selk, selv, wink, winv = kv
    b, t, _ = q.shape
    tq = Q_TILE
    tk = min(SEL_KEY_CHUNK, t)
    n_cmp_rows = cmpk.shape[2]
    assert t // SEL_BLOCK <= LANES - HEAD_DIM and (t // SEL_BLOCK) % 8 == 0
    assert t % tk == 0 and tk % tq == 0 and t >= WINDOW + tq
    assert WINDOW % tq == 0
    ov = _overlap_matrix(n_cmp_rows, t)
    wbias = _window_bias(tq)
    spread = _gate_spread()
    gw = GROUP * HEAD_DIM
    rows = GROUP * tq
    kv_spec = pl.BlockSpec((1, t, LANES), lambda i, h: (i, 0, h))
    cmp_spec = pl.BlockSpec((1, 1, n_cmp_rows, LANES), lambda i, h: (i, h, 0, 0))
    return pl.pallas_call(
        functools.partial(_nsa_attn_kernel, tq=tq, tk=tk),
        grid=(b, N_KVH),
        in_specs=[
            pl.BlockSpec((1, t, gw), lambda i, h: (i, 0, h)),
            pl.BlockSpec((1, t, LANES), lambda i, h: (i, 0, 0)),
            cmp_spec, cmp_spec, kv_spec, kv_spec, kv_spec, kv_spec,
            _resident(ov.shape), _resident(wbias.shape), _resident(spread.shape),
        ],
        out_specs=pl.BlockSpec((1, t, gw), lambda i, h: (i, 0, h)),
        out_shape=jax.ShapeDtypeStruct((b, t, N_QH * HEAD_DIM), BF16),
        scratch_shapes=[pltpu.VMEM((t // tk, rows, tk), F32), pltpu.VMEM((rows, LANES), F32),
                        pltpu.VMEM((rows, LANES), F32)],
        compiler_params=_params("parallel", "parallel"),
        name="nsa_attn",
    )(q, gates, cmpk, cmpv, selk, selv, wink, winv, ov, wbias, spread)


def mlstm_mixer(h2, b, t, g, w_in, b_if, g_head):
    qk, v, og, gates = mlstm_proj(h2, g, w_in, b_if)
    L = MLSTM_CHUNK
    hp = MLSTM_HEADS_PER_STEP
    gates_t = gates[:, :2 * M_HEADS].reshape(b, t // L, L, 2, M_HEADS // hp, hp).transpose(0, 4, 3, 5, 1, 2)
    hn = mlstm_chunk(qk.reshape(b, t, -1), v.reshape(b, t, -1), og.reshape(b, t, -1), gates_t, g_head)
    return hn.reshape(b * t, -1)


def nsa_shared_kv(h2, b, t, kv_norm, kv_w, cmp_pe, cmp_w1, cmp_w2, k_norm):
    cmp_x, *packs = kv_proj(h2, t, kv_norm, kv_w, k_norm)
    cmpk, cmpv = cmp_kv(cmp_x, b, t, cmp_pe, cmp_w1, cmp_w2, k_norm[0])
    return (cmpk, cmpv) + tuple(p.reshape(b, t, -1) for p in packs)


def nsa_mixer(h2, b, t, kv, g, w_in, q_norm):
    q, gates = nsa_proj(h2, g, w_in, q_norm)
    o = nsa_attn(q.reshape(b, t, -1), gates.reshape(b, t, -1), kv)
    return o.reshape(b * t, -1)


def kernel(x, ffn_norm, ffn_w_in, ffn_w_out, mix_norm, a_w_in, a_b_if, a_g_head, a_w_out, kv_norm, kv_w, cmp_pe, cmp_w1, cmp_w2, k_norm, b_w_in, b_q_norm, b_w_out):
    b, t, d = x.shape
    depth = ffn_norm.shape[0]
    n_a = a_w_in.shape[0]
    h = x.reshape(b * t, d)
    w_in_bf, w_out_bf = ffn_w_in, ffn_w_out
    kv = None
    for layer in range(depth):
        h = ffn(h, ffn_norm[layer, 0], w_in_bf, w_out_bf, layer, 0)
        if layer < n_a:
            mixed = mlstm_mixer(h, b, t, mix_norm[layer], a_w_in[layer], a_b_if[layer], a_g_head[layer])
            w_mix = a_w_out[layer]
        else:
            j = layer - n_a
            mixed = nsa_mixer(h, b, t, kv, mix_norm[layer], b_w_in[j], b_q_norm[j])
            w_mix = b_w_out[j]
        h = ffn(h, ffn_norm[layer, 1], w_in_bf, w_out_bf, layer, 1, mixer=(mixed, w_mix))
        if layer == n_a - 1:
            kv = nsa_shared_kv(h, b, t, kv_norm, kv_w, cmp_pe, cmp_w1, cmp_w2, k_norm)
    return h.reshape(b, t, d)
```

```python
import functools

import jax
import jax.numpy as jnp
import numpy as np
from jax import lax
from jax.experimental import pallas as pl
from jax.experimental.pallas import tpu as pltpu

F32 = jnp.float32
BF16 = jnp.bfloat16

EPS = 1e-6
NEG = -1e30

M_HEADS = 8
M_DV = 128
M_DQK = 64

N_QH = 16
N_KVH = 4
GROUP = N_QH // N_KVH
HEAD_DIM = 64
CMP_BLOCK = 32
CMP_STRIDE = 16
CMP_HIDDEN = 256
SEL_BLOCK = 64
SEL_TOPK = 16
WINDOW = 512
FORCE_SCORE = 1e4
LOG2E = 1.4426950408889634
SEL_BIAS = -2.0 ** 100
SEL_BLOCK_SHIFT = 6
MAX_SINGLE_PASS_SCORE = 48.0

LANES = 128
VMEM_LIMIT = 56 * 1024 * 1024

TOKEN_TILE = 512
FFN_CHUNK = 256
MLSTM_CHUNK = 256
MLSTM_HEADS_PER_STEP = 4
Q_TILE = 256
SEL_KEY_CHUNK = 512

_NT = (((1,), (1,)), ((), ()))


def _params(*sem):
    return pltpu.CompilerParams(dimension_semantics=sem, vmem_limit_bytes=VMEM_LIMIT)


def _rms(x, g):
    ms = jnp.mean(x * x, axis=-1, keepdims=True)
    return (x * lax.rsqrt(ms + EPS)) * g


def _dot(a, b):
    return jnp.dot(a, b, preferred_element_type=F32)


def _dot_nt(a, b, precision=None):
    return lax.dot_general(a, b, _NT, precision=precision, preferred_element_type=F32)


def _resident(shape):
    nd = len(shape)
    return pl.BlockSpec(shape, lambda *_: (0,) * nd)


def _ffn_kernel(*refs, mixer_proj):
    if mixer_proj:
        x_ref, a_ref, wmix_ref, g_ref, win_ref, wout_ref, o_ref, xn_ref, acc_ref, x_scr = refs
        x_scr[...] = x_ref[...] + _dot(a_ref[...], wmix_ref[...])
        x_ref = x_scr
    else:
        x_ref, g_ref, win_ref, wout_ref, o_ref, xn_ref, acc_ref = refs
    f = wout_ref.shape[0]
    x = x_ref[...]
    xn_ref[...] = _rms(x, g_ref[...]).astype(BF16)
    for c in range(f // FFN_CHUNK):
        cols = slice(c * FFN_CHUNK, (c + 1) * FFN_CHUNK)
        up_cols = slice(f + c * FFN_CHUNK, f + (c + 1) * FFN_CHUNK)
        xn = xn_ref[...]
        a = _dot(xn, win_ref[:, cols].astype(BF16))
        b = _dot(xn, win_ref[:, up_cols].astype(BF16))
        hid = ((a * jax.nn.sigmoid(a)) * b).astype(BF16)
        part = _dot(hid, wout_ref[cols, :].astype(BF16))
        if c == 0:
            acc_ref[...] = part
        else:
            acc_ref[...] += part
    o_ref[...] = x_ref[...] + 0.5 * acc_ref[...]


def ffn(x2, g, w_in_all, w_out_all, layer, j, mixer=None):
    n, d = x2.shape
    f = w_out_all.shape[2]
    assert f % FFN_CHUNK == 0
    tm = TOKEN_TILE
    tile = lambda w: pl.BlockSpec((tm, w), lambda i: (i, 0))
    ins, specs, scratch = [x2], [tile(d)], [pltpu.VMEM((tm, d), BF16), pltpu.VMEM((tm, d), F32)]
    if mixer is not None:
        a2, w_mix = mixer
        ins += [a2, w_mix.astype(BF16)]
        specs += [tile(a2.shape[1]), _resident(w_mix.shape)]
        scratch.append(pltpu.VMEM((tm, d), F32))
    return pl.pallas_call(
        functools.partial(_ffn_kernel, mixer_proj=mixer is not None),
        grid=(n // tm,),
        in_specs=specs + [
            _resident((1, d)),
            pl.BlockSpec((None, None, d, 2 * f), lambda i: (layer, j, 0, 0), pipeline_mode=pl.Buffered(1)),
            pl.BlockSpec((None, None, f, d), lambda i: (layer, j, 0, 0), pipeline_mode=pl.Buffered(1)),
        ],
        out_specs=tile(d),
        out_shape=jax.ShapeDtypeStruct((n, d), F32),
        scratch_shapes=scratch,
        compiler_params=_params("parallel"),
        name="ffn",
    )(*ins, g.reshape(1, d), w_in_all, w_out_all)


def _mlstm_proj_kernel(x_ref, g_ref, wqk_ref, wv_ref, wog_ref, wg_ref, bg_ref,
                       qk_ref, v_ref, og_ref, gate_ref):
    xn = _rms(x_ref[...], g_ref[...]).astype(BF16)
    qk = _dot(xn, wqk_ref[...])
    is_q = lax.broadcasted_iota(jnp.int32, (1, qk.shape[1]), 1) < qk.shape[1] // 2
    qk_ref[...] = (qk * jnp.where(is_q, M_DQK ** -0.5, 1.0)).astype(BF16)
    v_ref[...] = _dot(xn, wv_ref[...]).astype(BF16)
    og_ref[...] = jax.nn.sigmoid(_dot(xn, wog_ref[...])).astype(BF16)
    gate_ref[...] = _dot(xn, wg_ref[...]) + bg_ref[...]


def mlstm_proj(h2, g, w_in, b_if):
    n, d = h2.shape
    qk_w, v_w = M_HEADS * M_DQK, M_HEADS * M_DV
    wb = w_in.astype(BF16)
    wqk = wb[:, :2 * qk_w]
    wv = wb[:, 2 * qk_w:2 * qk_w + v_w]
    wgate = jnp.pad(wb[:, 2 * qk_w + v_w:2 * qk_w + v_w + 2 * M_HEADS], ((0, 0), (0, LANES - 2 * M_HEADS)))
    wog = wb[:, 2 * qk_w + v_w + 2 * M_HEADS:]
    bg = jnp.pad(b_if.astype(F32).reshape(1, 2 * M_HEADS), ((0, 0), (0, LANES - 2 * M_HEADS)))
    tm = TOKEN_TILE
    tile = lambda w: pl.BlockSpec((tm, w), lambda i: (i, 0))
    return pl.pallas_call(
        _mlstm_proj_kernel,
        grid=(n // tm,),
        in_specs=[tile(d), _resident((1, d)), _resident(wqk.shape), _resident(wv.shape),
                  _resident(wog.shape), _resident(wgate.shape), _resident(bg.shape)],
        out_specs=[tile(2 * qk_w), tile(v_w), tile(d), tile(LANES)],
        out_shape=[jax.ShapeDtypeStruct((n, 2 * qk_w), BF16), jax.ShapeDtypeStruct((n, v_w), BF16),
                   jax.ShapeDtypeStruct((n, d), BF16), jax.ShapeDtypeStruct((n, LANES), F32)],
        compiler_params=_params("parallel"),
        name="mlstm_proj",
    )(h2, g.reshape(1, d), wqk, wv, wog, wgate, bg)


def _log_sigmoid(x):
    return jnp.minimum(x, 0.0) - jnp.log1p(jnp.exp(-jnp.abs(x)))


def _mlstm_chunk_kernel(q_ref, k_ref, v_ref, og_ref, gate_ref, ghead_ref, o_ref, c_ref, g_r_ref, b_r_ref, cm_r_ref):
    heads = c_ref.shape[0]
    first_head = pl.program_id(1) * heads
    L = MLSTM_CHUNK
    n_chunks = q_ref.shape[1] // L

    r_i = lax.broadcasted_iota(jnp.int32, (L, L), 0)
    c_i = lax.broadcasted_iota(jnp.int32, (L, L), 1)
    causal = c_i <= r_i
    tril = causal.astype(F32)
    lane = lax.broadcasted_iota(jnp.int32, (L, LANES), 1)
    ones_blk = jnp.ones((L, LANES), BF16)

    c_ref[...] = jnp.zeros_like(c_ref)

    t_i = lax.broadcasted_iota(jnp.int32, (n_chunks, L), 1)
    for hh in range(heads):
        b = _dot_nt(_log_sigmoid(gate_ref[0, 0, 1, hh]), tril, lax.Precision.HIGHEST)
        g = gate_ref[0, 0, 0, hh] - b
        cm = g
        shift = 1
        while shift < L:
            cm = jnp.maximum(cm, jnp.where(t_i >= shift, pltpu.roll(cm, shift, 1), -jnp.inf))
            shift *= 2
        b_r_ref[hh] = b
        g_r_ref[hh] = g
        cm_r_ref[hh] = cm

    def column(row):
        return jnp.broadcast_to(row, (LANES, L)).T

    def body(c, m_prev):
        base = pl.multiple_of(c * L, L)
        m_next = []
        for hh in range(heads):
            head = first_head + hh
            pair_cols = slice((hh // 2) * LANES, (hh // 2 + 1) * LANES)
            in_head = (lane >= (hh % 2) * M_DQK) & (lane < (hh % 2 + 1) * M_DQK)
            qh = q_ref[0, pl.ds(base, L), pair_cols]
            kh = jnp.where(in_head, k_ref[0, pl.ds(base, L), pair_cols], 0)
            v_aug = jnp.concatenate([v_ref[0, pl.ds(base, L), hh * M_DV:(hh + 1) * M_DV], ones_blk], axis=1)

            g_r = g_r_ref[hh, pl.ds(c, 1), :]
            b_r = b_r_ref[hh, pl.ds(c, 1), :]
            g_tot = b_r[:, L - 1:L]
            m_p = m_prev[hh]
            c_prev = c_ref[hh]

            a_r = g_tot + g_r
            m_loc = jnp.max(a_r, axis=1, keepdims=True)
            kw_t = (kh.astype(F32).T * jnp.exp(a_r - m_loc)).astype(BF16)
            c_loc = _dot(kw_t, v_aug)

            mm = jnp.maximum(m_p, column(cm_r_ref[hh, pl.ds(c, 1), :]))
            mm_wide = jnp.concatenate([mm] * (L // LANES), axis=1)
            w = jnp.where(causal, jnp.exp(g_r - mm_wide), 0.0) * _dot_nt(qh, kh)
            num = jnp.concatenate([jnp.exp(m_p - mm)] * 2, axis=1) * _dot(qh, c_prev.astype(BF16)) \
                + _dot(w.astype(BF16), v_aug)
            floor = jnp.exp(-(column(b_r) + mm))
            h_out = num[:, :M_DV] / jnp.maximum(jnp.abs(num[:, M_DV:]), floor)
            h_out = _rms(h_out, ghead_ref[pl.ds(head, 1), :])
            cols = slice(hh * M_DV, (hh + 1) * M_DV)
            o_ref[0, pl.ds(base, L), cols] = (h_out * og_ref[0, pl.ds(base, L), cols]).astype(BF16)

            m_new = jnp.maximum(g_tot + m_p, m_loc)
            c_ref[hh] = jnp.exp(g_tot + m_p - m_new) * c_prev + jnp.exp(m_loc - m_new) * c_loc
            m_next.append(m_new)
        return tuple(m_next)

    lax.fori_loop(0, n_chunks, body, (jnp.zeros((1, 1), F32),) * heads)


def mlstm_chunk(qk, v, og, gates_t, g_head):
    b, t, _ = v.shape
    L = MLSTM_CHUNK
    nc = t // L
    hp = MLSTM_HEADS_PER_STEP
    groups = M_HEADS // hp
    qk_cols, v_cols = hp * M_DQK, hp * M_DV
    row_scratch = pltpu.VMEM((hp, nc, L), F32)
    return pl.pallas_call(
        _mlstm_chunk_kernel,
        grid=(b, groups),
        in_specs=[
            pl.BlockSpec((1, t, qk_cols), lambda i, p: (i, 0, p)),
            pl.BlockSpec((1, t, qk_cols), lambda i, p: (i, 0, groups + p)),
            pl.BlockSpec((1, t, v_cols), lambda i, p: (i, 0, p)),
            pl.BlockSpec((1, t, v_cols), lambda i, p: (i, 0, p)),
            pl.BlockSpec((1, 1, 2, hp, nc, L), lambda i, p: (i, p, 0, 0, 0, 0)),
            _resident((M_HEADS, M_DV)),
        ],
        out_specs=pl.BlockSpec((1, t, v_cols), lambda i, p: (i, 0, p)),
        out_shape=jax.ShapeDtypeStruct((b, t, M_HEADS * M_DV), BF16),
        scratch_shapes=[pltpu.VMEM((hp, 2 * M_DQK, 2 * M_DV), F32), row_scratch, row_scratch, row_scratch],
        compiler_params=_params("parallel", "parallel"),
        name="mlstm_chunk",
    )(qk, qk, v, og, gates_t, g_head.astype(F32))


def _kv_packs(y, gain, extra):
    lane = lax.broadcasted_iota(jnp.int32, (1, LANES), 1)
    lo = lane < HEAD_DIM
    kp, vp = [], []
    for h in range(N_KVH):
        yh = y[:, h * LANES:(h + 1) * LANES]
        ms = jnp.sum(jnp.where(lo, yh * yh, 0.0), axis=-1, keepdims=True) * (1.0 / HEAD_DIM)
        kp.append(jnp.where(lo, (yh * lax.rsqrt(ms + EPS)) * gain, extra))
        vp.append(jnp.where(lo, 1.0, yh))
    return jnp.concatenate(kp, axis=1), jnp.concatenate(vp, axis=1)


def _kv_proj_kernel(x_ref, g_ref, wc_ref, ws_ref, ww_ref, gs_ref, gw_ref,
                    cmp_ref, selk_ref, selv_ref, wink_ref, winv_ref, y_ref, *, seq_len):
    tm = x_ref.shape[0]
    xn = _rms(x_ref[...], g_ref[...]).astype(BF16)

    y = _dot(xn, wc_ref[...])
    n_slab = y.shape[1] // LANES
    for s in range(n_slab):
        y_ref[s] = y[:, s * LANES:(s + 1) * LANES]
    groups = tm // CMP_STRIDE
    lo = lax.broadcasted_iota(jnp.int32, (1, LANES), 1) < HEAD_DIM
    for s in range(n_slab):
        for rp in range(CMP_STRIDE // 2):
            even = y_ref[s, pl.ds(2 * rp, groups, stride=CMP_STRIDE), :]
            odd = y_ref[s, pl.ds(2 * rp + 1, groups, stride=CMP_STRIDE), :]
            cols = slice(rp * LANES, (rp + 1) * LANES)
            cmp_ref[2 * s, :, cols] = jnp.where(lo, even, pltpu.roll(odd, HEAD_DIM, 1))
            cmp_ref[2 * s + 1, :, cols] = jnp.where(lo, pltpu.roll(even, HEAD_DIM, 1), odd)

    pos = (pl.program_id(0) * tm) % seq_len + lax.broadcasted_iota(jnp.int32, (tm, 1), 0)
    lane = lax.broadcasted_iota(jnp.int32, (1, LANES), 1)
    block_bias = jnp.where(lane == HEAD_DIM + lax.shift_right_logical(pos, SEL_BLOCK_SHIFT), SEL_BIAS, 0.0)
    kp, vp = _kv_packs(_dot(xn, ws_ref[...]), gs_ref[...], block_bias)
    selk_ref[...] = kp.astype(BF16)
    selv_ref[...] = vp.astype(BF16)
    kp, vp = _kv_packs(_dot(xn, ww_ref[...]), gw_ref[...], 0.0)
    wink_ref[...] = kp.astype(BF16)
    winv_ref[...] = vp.astype(BF16)


def _kv_pack_weights(wk, wv):
    d = wk.shape[0]
    wk = wk.reshape(d, N_KVH, HEAD_DIM)
    wv = wv.reshape(d, N_KVH, HEAD_DIM)
    return jnp.concatenate([wk, wv], axis=-1).reshape(d, N_KVH * LANES)


def _k_gain_row(gain):
    return jnp.concatenate([gain.astype(F32), jnp.ones((HEAD_DIM,), F32)]).reshape(1, LANES)


def kv_proj(h2, t, g, kv_w, k_norm):
    n, d = h2.shape
    wb = kv_w.astype(BF16)
    kc, vc, ks, vs, kw, vw = jnp.split(wb, 6, axis=1)
    wcmp = jnp.concatenate([kc, vc], axis=1)
    wsel = _kv_pack_weights(ks, vs)
    wwin = _kv_pack_weights(kw, vw)
    wd = N_KVH * LANES
    tm = TOKEN_TILE
    assert t % tm == 0 and t // SEL_BLOCK <= LANES - HEAD_DIM
    tile = lambda w: pl.BlockSpec((tm, w), lambda i: (i, 0))
    packed = jax.ShapeDtypeStruct((n, wd), BF16)
    feat = CMP_STRIDE * HEAD_DIM
    groups = tm // CMP_STRIDE
    return pl.pallas_call(
        functools.partial(_kv_proj_kernel, seq_len=t),
        grid=(n // tm,),
        in_specs=[tile(d), _resident((1, d)), _resident(wcmp.shape), _resident(wsel.shape),
                  _resident(wwin.shape), _resident((1, LANES)), _resident((1, LANES))],
        out_specs=[pl.BlockSpec((2 * N_KVH, groups, feat), lambda i: (0, i, 0))] + [tile(wd)] * 4,
        out_shape=[jax.ShapeDtypeStruct((2 * N_KVH, n // CMP_STRIDE, feat), F32), packed, packed, packed, packed],
        scratch_shapes=[pltpu.VMEM((2 * N_KVH * HEAD_DIM // LANES, tm, LANES), F32)],
        compiler_params=_params("parallel"),
        name="kv_proj",
    )(h2, g.reshape(1, d), wcmp, wsel, wwin, _k_gain_row(k_norm[1]), _k_gain_row(k_norm[2]))


def _cmp_kernel(xk_ref, xv_ref, pe_ref, w1_ref, w2k_ref, w2v_ref, gk_ref, ok_ref, ov_ref):
    def compress(x, kv, w2):
        u = _dot((x + pe_ref[kv, 0]).astype(BF16), w1_ref[kv, 0])
        v = _dot((x + pe_ref[kv, 1]).astype(BF16), w1_ref[kv, 1])
        nrow = v.shape[0]
        hid = u + pltpu.roll(v, nrow - 1, 0)
        return _dot((hid * jax.nn.sigmoid(hid)).astype(BF16), w2)

    yk = compress(xk_ref[0], 0, w2k_ref[...])
    yv = compress(xv_ref[0], 1, w2v_ref[...])
    ms = jnp.sum(yk * yk, axis=-1, keepdims=True) * (1.0 / HEAD_DIM)
    ok_ref[0, 0] = ((yk * lax.rsqrt(ms + EPS)) * gk_ref[...]).astype(BF16)
    lane = lax.broadcasted_iota(jnp.int32, (1, LANES), 1)
    ov_ref[0, 0] = jnp.where(lane < HEAD_DIM, 1.0, yv).astype(BF16)


def cmp_kv(cmp_x, b, t, cmp_pe, cmp_w1, cmp_w2, k_gain):
    ng = t // CMP_STRIDE
    feat = CMP_STRIDE * HEAD_DIM
    pe = cmp_pe.astype(F32).reshape(2, 2, 1, feat)
    w1 = cmp_w1.astype(BF16).reshape(2, 2, feat, CMP_HIDDEN)
    zeros = jnp.zeros((CMP_HIDDEN, HEAD_DIM), BF16)
    w2k = jnp.concatenate([cmp_w2[0].astype(BF16), zeros], axis=1)
    w2v = jnp.concatenate([zeros, cmp_w2[1].astype(BF16)], axis=1)
    ospec = pl.BlockSpec((1, 1, ng, LANES), lambda i, h: (i, h, 0, 0))
    packed = jax.ShapeDtypeStruct((b, N_KVH, ng, LANES), BF16)
    return pl.pallas_call(
        _cmp_kernel,
        grid=(b, N_KVH),
        in_specs=[pl.BlockSpec((1, ng, feat), lambda i, h: (h, i, 0)),
                  pl.BlockSpec((1, ng, feat), lambda i, h: (N_KVH + h, i, 0)),
                  _resident(pe.shape), _resident(w1.shape), _resident(w2k.shape),
                  _resident(w2v.shape), _resident((1, LANES))],
        out_specs=[ospec, ospec],
        out_shape=[packed, packed],
        compiler_params=_params("parallel", "parallel"),
        name="cmp_kv",
    )(cmp_x, cmp_x, pe, w1, w2k, w2v, _k_gain_row(k_gain))


def _nsa_proj_kernel(x_ref, g_ref, wq_ref, wg_ref, qg_ref, q_ref, gate_ref):
    xn = _rms(x_ref[...], g_ref[...]).astype(BF16)
    q = _dot(xn, wq_ref[...])
    lane = lax.broadcasted_iota(jnp.int32, (1, LANES), 1)
    lo = lane < HEAD_DIM
    outs = []
    for cb in range(q.shape[1] // LANES):
        y = q[:, cb * LANES:(cb + 1) * LANES]
        y2 = y * y
        ms_lo = jnp.sum(jnp.where(lo, y2, 0.0), axis=-1, keepdims=True) * (1.0 / HEAD_DIM)
        ms_hi = jnp.sum(jnp.where(lo, 0.0, y2), axis=-1, keepdims=True) * (1.0 / HEAD_DIM)
        scale = jnp.where(lo, lax.rsqrt(ms_lo + EPS), lax.rsqrt(ms_hi + EPS))
        outs.append(((y * scale) * qg_ref[...]) * (HEAD_DIM ** -0.5 * LOG2E))
    q_ref[...] = jnp.concatenate(outs, axis=1).astype(BF16)
    gate_ref[...] = jax.nn.sigmoid(_dot(xn, wg_ref[...])).astype(BF16)


def nsa_proj(h2, g, w_in, q_norm):
    n, d = h2.shape
    wb = w_in.astype(BF16)
    wq = wb[:, :N_QH * HEAD_DIM]
    wg = wb[:, N_QH * HEAD_DIM:].reshape(d, 3, N_KVH, GROUP).transpose(0, 2, 1, 3).reshape(d, N_KVH * 3 * GROUP)
    wg = jnp.pad(wg, ((0, 0), (0, LANES - N_KVH * 3 * GROUP)))
    qg = jnp.tile(q_norm.astype(F32), LANES // HEAD_DIM).reshape(1, LANES)
    tm = TOKEN_TILE
    tile = lambda w: pl.BlockSpec((tm, w), lambda i: (i, 0))
    return pl.pallas_call(
        _nsa_proj_kernel,
        grid=(n // tm,),
        in_specs=[tile(d), _resident((1, d)), _resident(wq.shape), _resident(wg.shape), _resident((1, LANES))],
        out_specs=[tile(N_QH * HEAD_DIM), tile(LANES)],
        out_shape=[jax.ShapeDtypeStruct((n, N_QH * HEAD_DIM), BF16),
                   jax.ShapeDtypeStruct((n, LANES), BF16)],
        compiler_params=_params("parallel"),
        name="nsa_proj",
    )(h2, g.reshape(1, d), wq, wg, qg)


def _masked_exp2(s3, mask):
    sm = jnp.where(mask[None], s3, NEG)
    mx = jnp.max(sm, axis=-1, keepdims=True)
    return jnp.exp2(sm - mx), mx > 0.5 * NEG


def _normalise(acc, values_lo, guard_empty=False):
    lo = lax.broadcasted_iota(jnp.int32, acc.shape, 1) < HEAD_DIM
    den = pltpu.roll(acc, HEAD_DIM, 1)
    use = lo == values_lo
    if guard_empty:
        use = use & (den > 0.0)
    return acc / jnp.where(use, den, 1.0)


def _stacked_queries(q_ref, s0, tq):
    lo = lax.broadcasted_iota(jnp.int32, (tq, LANES), 1) < HEAD_DIM
    qs = []
    for g in range(GROUP):
        blk = q_ref[0, pl.ds(s0, tq), (g // 2) * LANES:(g // 2 + 1) * LANES].astype(F32)
        if g % 2 == 1:
            blk = pltpu.roll(blk, HEAD_DIM, 1)
        qs.append(jnp.where(lo, blk, 0.0))
    return jnp.concatenate(qs, axis=0)


def _dropped_blocks(imp, s0, tq, n_sel):
    lane = lax.broadcasted_iota(jnp.int32, (tq, LANES), 1)
    t_q = s0 + lax.broadcasted_iota(jnp.int32, (tq, 1), 0)
    valid = (lane * SEL_BLOCK <= t_q) & (lane < n_sel)
    cur = lax.shift_right_logical(t_q, SEL_BLOCK_SHIFT)
    forced = (lane == 0) | (lane == cur) | (lane == cur - 1)
    score = jnp.where(forced & valid, FORCE_SCORE, jnp.where(valid, imp, -1.0))
    score = jnp.where(lane < n_sel, score, -2.0)
    s_t = score.T[:HEAD_DIM]
    sub = lax.broadcasted_iota(jnp.int32, (8, tq), 0)
    groups = [s_t[8 * v:8 * v + 8] for v in range(HEAD_DIM // 8)]
    ranks = [jnp.zeros((8, tq), F32) for _ in groups]
    for i in range(n_sel):
        s_i = s_t[i:i + 1, :]
        for v, s_v in enumerate(groups):
            if 8 * v + 7 < i:
                ahead = s_i > s_v
            elif 8 * v > i:
                ahead = s_i >= s_v
            else:
                ahead = (s_i > s_v) | ((s_i == s_v) & (sub > i - 8 * v))
            ranks[v] = ranks[v] + jnp.where(ahead, 1.0, 0.0)
    rank = jnp.concatenate(ranks, axis=0)
    blk_t = lax.broadcasted_iota(jnp.int32, (HEAD_DIM, tq), 0)
    t_row = s0 + lax.broadcasted_iota(jnp.int32, (1, tq), 1)
    dropped_t = jnp.where((rank < SEL_TOPK) & (blk_t * SEL_BLOCK <= t_row), 0.0, 1.0)
    return jnp.concatenate([jnp.zeros((HEAD_DIM, tq), F32), dropped_t], axis=0).T


def _store_gated(o_ref, gate_ref, spread_ref, s0, tq, o_c, o_s, o_w):
    g_rep = _dot(gate_ref[0, pl.ds(s0, tq), :], spread_ref[pl.program_id(1)])

    def gate(branch):
        first = branch * GROUP
        return jnp.concatenate([g_rep[:, (first + g) * LANES:(first + g + 1) * LANES] for g in range(GROUP)], axis=0)

    o = (gate(0) * o_c + gate(1) * o_s + gate(2) * o_w).reshape(GROUP, tq, LANES)
    lo = lax.broadcasted_iota(jnp.int32, (tq, LANES), 1) < HEAD_DIM
    out01 = jnp.where(lo, pltpu.roll(o[0], HEAD_DIM, 1), o[1])
    out23 = jnp.where(lo, pltpu.roll(o[2], HEAD_DIM, 1), o[3])
    o_ref[0, pl.ds(s0, tq), :] = jnp.concatenate([out01, out23], axis=1).astype(BF16)


def _chunk_groups(n, per_chunk):
    def trip(i, carry):
        for u in range(4):
            per_chunk(4 * i + u)
        return carry

    lax.fori_loop(0, n // 4, trip, 0)
    done = 4 * (n // 4)

    @pl.when(n - done >= 2)
    def _():
        per_chunk(done)
        per_chunk(done + 1)

    @pl.when((n - done) % 2 == 1)
    def _():
        per_chunk(n - 1)


def _nsa_attn_bounded_kernel(bound_ref, q_ref, gate_ref, cmpk_ref, cmpv_ref, selk_ref, selv_ref, wink_ref,
                             winv_ref, ov_ref, wbias_ref, spread_ref, o_ref, acc_ref, *, tq, tk):
    rows = GROUP * tq
    n_sel = selk_ref.shape[1] // SEL_BLOCK
    b_cmp, b_sel, b_win = bound_ref[0], bound_ref[1], bound_ref[2]

    def tile(tile_idx, carry):
        s0 = pl.multiple_of(tile_idx * tq, tq)
        q4 = _stacked_queries(q_ref, s0, tq)
        qst = q4.astype(BF16)
        t_q = s0 + lax.broadcasted_iota(jnp.int32, (tq, 1), 0)

        kc = cmpk_ref[0, 0]
        n_cmp = kc.shape[0]
        cmp_end = lax.broadcasted_iota(jnp.int32, (1, n_cmp), 1) * CMP_STRIDE + (CMP_BLOCK - 1)
        s_c = jnp.where((cmp_end <= t_q)[None], _dot_nt(qst, kc).reshape(GROUP, tq, n_cmp), NEG)
        e_c = jnp.exp2(s_c - b_cmp).reshape(rows, n_cmp).astype(BF16)
        both = _dot(e_c, jnp.concatenate([cmpv_ref[0, 0], ov_ref[...]], axis=1))
        o_c = _normalise(both[:, :LANES], False, guard_empty=True)
        imp4 = _normalise(both[:, LANES:], True, guard_empty=True).reshape(GROUP, tq, LANES)
        dropped = _dropped_blocks(imp4[0] + imp4[1] + imp4[2] + imp4[3], s0, tq, n_sel)
        qx = (q4 + jnp.concatenate([dropped] * GROUP, axis=0)).astype(BF16)

        wk = WINDOW + tq
        w0 = pl.multiple_of(jnp.maximum(s0 - WINDOW, 0), tq)
        s_w = _dot_nt(qst, wink_ref[0, pl.ds(w0, wk), :]).reshape(GROUP, tq, wk)
        s_w = s_w + wbias_ref[jnp.minimum(tile_idx, WINDOW // tq)][None]
        e_w = jnp.exp2(s_w - b_win).reshape(rows, wk).astype(BF16)
        o_w = _normalise(_dot(e_w, winv_ref[0, pl.ds(w0, wk), :]), False)

        last = s0 // tk
        acc_ref[...] = jnp.zeros_like(acc_ref)

        def accumulate(c, causal=False):
            base = pl.multiple_of(c * tk, tk)
            s = _dot_nt(qx, selk_ref[0, pl.ds(base, tk), :])
            if causal:
                kpos = base + lax.broadcasted_iota(jnp.int32, (1, tk), 1)
                s = jnp.where((kpos <= t_q)[None], s.reshape(GROUP, tq, tk), NEG).reshape(rows, tk)
            acc_ref[...] += _dot(jnp.exp2(s - b_sel).astype(BF16), selv_ref[0, pl.ds(base, tk), :])

        _chunk_groups(last, accumulate)
        accumulate(last, causal=True)
        o_s = _normalise(acc_ref[...], False)

        _store_gated(o_ref, gate_ref, spread_ref, s0, tq, o_c, o_s, o_w)
        return carry

    lax.fori_loop(0, q_ref.shape[1] // tq, tile, 0)


def _nsa_attn_kernel(q_ref, gate_ref, cmpk_ref, cmpv_ref, selk_ref, selv_ref, wink_ref, winv_ref,
                     ov_ref, wbias_ref, spread_ref, o_ref, s_ref, mx_ref, acc_ref, *, tq, tk):
    rows = GROUP * tq

    def tile(tile_idx, carry):
        s0 = pl.multiple_of(tile_idx * tq, tq)

        lane = lax.broadcasted_iota(jnp.int32, (tq, LANES), 1)
        lo = lane < HEAD_DIM
        qs = []
        for g in range(GROUP):
            blk = q_ref[0, pl.ds(s0, tq), (g // 2) * LANES:(g // 2 + 1) * LANES].astype(F32)
            if g % 2 == 1:
                blk = pltpu.roll(blk, HEAD_DIM, 1)
            qs.append(jnp.where(lo, blk, 0.0))
        q4 = jnp.concatenate(qs, axis=0)
        qst = q4.astype(BF16)
        t_q = s0 + lax.broadcasted_iota(jnp.int32, (tq, 1), 0)

        kc = cmpk_ref[0, 0]
        n_cmp = kc.shape[0]
        cmp_end = lax.broadcasted_iota(jnp.int32, (1, n_cmp), 1) * CMP_STRIDE + (CMP_BLOCK - 1)
        e_c, ok_c = _masked_exp2(_dot_nt(qst, kc).reshape(GROUP, tq, n_cmp), cmp_end <= t_q)
        e_c = e_c.reshape(rows, n_cmp)
        ok_c = ok_c.reshape(rows, 1)
        both = _dot(e_c.astype(BF16), jnp.concatenate([cmpv_ref[0, 0], ov_ref[...]], axis=1))
        o_c = jnp.where(ok_c, _normalise(both[:, :LANES], False), 0.0)

        imp4 = jnp.where(ok_c, _normalise(both[:, LANES:], True), 0.0).reshape(GROUP, tq, LANES)
        imp = imp4[0] + imp4[1] + imp4[2] + imp4[3]

        n_sel = selk_ref.shape[1] // SEL_BLOCK
        valid = (lane * SEL_BLOCK <= t_q) & (lane < n_sel)
        cur = lax.shift_right_logical(t_q, SEL_BLOCK_SHIFT)
        forced = (lane == 0) | (lane == cur) | (lane == cur - 1)
        score = jnp.where(forced & valid, FORCE_SCORE, jnp.where(valid, imp, -1.0))
        score = jnp.where(lane < n_sel, score, -2.0)
        s_t = score.T[:HEAD_DIM]
        sub = lax.broadcasted_iota(jnp.int32, (8, tq), 0)
        groups = [s_t[8 * v:8 * v + 8] for v in range(HEAD_DIM // 8)]
        ranks = [jnp.zeros((8, tq), F32) for _ in groups]
        for i in range(n_sel):
            s_i = s_t[i:i + 1, :]
            for v, s_v in enumerate(groups):
                if 8 * v + 7 < i:
                    ahead = s_i > s_v
                elif 8 * v > i:
                    ahead = s_i >= s_v
                else:
                    ahead = (s_i > s_v) | ((s_i == s_v) & (sub > i - 8 * v))
                ranks[v] = ranks[v] + jnp.where(ahead, 1.0, 0.0)
        rank = jnp.concatenate(ranks, axis=0)
        blk_t = lax.broadcasted_iota(jnp.int32, (HEAD_DIM, tq), 0)
        t_row = s0 + lax.broadcasted_iota(jnp.int32, (1, tq), 1)
        dropped_t = jnp.where((rank < SEL_TOPK) & (blk_t * SEL_BLOCK <= t_row), 0.0, 1.0)
        dropped = jnp.concatenate([jnp.zeros((HEAD_DIM, tq), F32), dropped_t], axis=0).T
        qx = (q4 + jnp.concatenate([dropped] * GROUP, axis=0)).astype(BF16)

        wk = WINDOW + tq
        w0 = pl.multiple_of(jnp.maximum(s0 - WINDOW, 0), tq)
        s_w = _dot_nt(qst, wink_ref[0, pl.ds(w0, wk), :]).reshape(GROUP, tq, wk)
        s_w = s_w + wbias_ref[jnp.minimum(tile_idx, WINDOW // tq)][None]
        e_w = jnp.exp2(s_w - jnp.max(s_w, axis=-1, keepdims=True))
        o_w = _normalise(_dot(e_w.reshape(rows, wk).astype(BF16), winv_ref[0, pl.ds(w0, wk), :]), False)

        nblk = tk // LANES
        last = s0 // tk
        mx_ref[...] = jnp.full(mx_ref.shape, NEG, F32)

        def scores(c, causal):
            base = pl.multiple_of(c * tk, tk)
            s = _dot_nt(qx, selk_ref[0, pl.ds(base, tk), :])
            if causal:
                kpos = base + lax.broadcasted_iota(jnp.int32, (1, tk), 1)
                s = jnp.where((kpos <= t_q)[None], s.reshape(GROUP, tq, tk), NEG).reshape(rows, tk)
            s_ref[c] = s
            m = mx_ref[...]
            for j in range(nblk):
                m = jnp.maximum(m, s[:, j * LANES:(j + 1) * LANES])
            mx_ref[...] = m

        def pass1(i, carry):
            for u in range(4):
                scores(4 * i + u, causal=False)
            return carry

        lax.fori_loop(0, last // 4, pass1, 0)
        done = 4 * (last // 4)

        @pl.when(last - done >= 2)
        def _():
            scores(done, causal=False)
            scores(done + 1, causal=False)

        @pl.when((last - done) % 2 == 1)
        def _():
            scores(last - 1, causal=False)

        scores(last, causal=True)
        mx_ref[...] = jnp.broadcast_to(jnp.max(mx_ref[...], axis=-1, keepdims=True), mx_ref.shape)

        acc_ref[...] = jnp.zeros_like(acc_ref)

        def weighted(c):
            base = pl.multiple_of(c * tk, tk)
            s = s_ref[c]
            m = mx_ref[...]
            p = jnp.concatenate([jnp.exp2(s[:, j * LANES:(j + 1) * LANES] - m) for j in range(nblk)], axis=1)
            return _dot(p.astype(BF16), selv_ref[0, pl.ds(base, tk), :])

        def pass2(i, carry):
            acc_ref[...] += (weighted(4 * i) + weighted(4 * i + 1)) + (weighted(4 * i + 2) + weighted(4 * i + 3))
            return carry

        n_chunks = last + 1
        lax.fori_loop(0, n_chunks // 4, pass2, 0)
        done2 = 4 * (n_chunks // 4)

        @pl.when(n_chunks - done2 >= 2)
        def _():
            acc_ref[...] += weighted(done2) + weighted(done2 + 1)

        @pl.when((n_chunks - done2) % 2 == 1)
        def _():
            acc_ref[...] += weighted(last)

        o_s = _normalise(acc_ref[...], False)

        g_rep = _dot(gate_ref[0, pl.ds(s0, tq), :], spread_ref[pl.program_id(1)])

        def gate(branch):
            first = branch * GROUP
            return jnp.concatenate([g_rep[:, (first + g) * LANES:(first + g + 1) * LANES] for g in range(GROUP)], axis=0)

        o = (gate(0) * o_c + gate(1) * o_s + gate(2) * o_w).reshape(GROUP, tq, LANES)
        out01 = jnp.where(lo, pltpu.roll(o[0], HEAD_DIM, 1), o[1])
        out23 = jnp.where(lo, pltpu.roll(o[2], HEAD_DIM, 1), o[3])
        o_ref[0, pl.ds(s0, tq), :] = jnp.concatenate([out01, out23], axis=1).astype(BF16)
        return carry

    lax.fori_loop(0, q_ref.shape[1] // tq, tile, 0)


def _overlap_matrix(n_cmp_rows, t):
    n_sel = t // SEL_BLOCK
    c0 = np.arange(n_cmp_rows) * CMP_STRIDE
    s_0 = np.arange(n_sel) * SEL_BLOCK
    ov = (c0[:, None] < s_0[None, :] + SEL_BLOCK) & (c0[:, None] + CMP_BLOCK > s_0[None, :])
    ov[(t - CMP_BLOCK) // CMP_STRIDE + 1:] = False
    out = np.zeros((n_cmp_rows, LANES), np.float32)
    out[:, :n_sel] = ov
    out[:, HEAD_DIM:] = 1.0
    return jnp.asarray(out, BF16)


def _window_bias(tq):
    off = np.arange(WINDOW // tq + 1).reshape(-1, 1, 1) * tq
    i = np.arange(tq).reshape(1, tq, 1)
    j = np.arange(WINDOW + tq).reshape(1, 1, -1)
    visible = (j <= off + i) & (j > off + i - WINDOW)
    return jnp.asarray(np.where(visible, 0.0, NEG).astype(np.float32))


def _gate_spread():
    n_gate = 3 * GROUP
    head = np.arange(N_KVH).reshape(N_KVH, 1, 1)
    src = np.arange(LANES).reshape(1, LANES, 1)
    dst_group = np.arange(n_gate * LANES).reshape(1, 1, -1) // LANES
    return jnp.asarray((src == n_gate * head + dst_group).astype(np.float32), BF16)


def nsa_attn(q, gates, kv, score_bound):
    cmpk, cmpv, selk, selv, wink, winv = kv
    b, t, _ = q.shape
    tq = Q_TILE
    tk = min(SEL_KEY_CHUNK, t)
    n_cmp_rows = cmpk.shape[2]
    assert t // SEL_BLOCK <= LANES - HEAD_DIM and (t // SEL_BLOCK) % 8 == 0
    assert t % tk == 0 and tk % tq == 0 and t >= WINDOW + tq
    assert WINDOW % tq == 0
    ov = _overlap_matrix(n_cmp_rows, t)
    wbias = _window_bias(tq)
    spread = _gate_spread()
    gw = GROUP * HEAD_DIM
    rows = GROUP * tq
    kv_spec = pl.BlockSpec((1, t, LANES), lambda i, h: (i, 0, h))
    cmp_spec = pl.BlockSpec((1, 1, n_cmp_rows, LANES), lambda i, h: (i, h, 0, 0))
    operands = (q, gates, cmpk, cmpv, selk, selv, wink, winv, ov, wbias, spread)
    specs = [
        pl.BlockSpec((1, t, gw), lambda i, h: (i, 0, h)),
        pl.BlockSpec((1, t, LANES), lambda i, h: (i, 0, 0)),
        cmp_spec, cmp_spec, kv_spec, kv_spec, kv_spec, kv_spec,
        _resident(ov.shape), _resident(wbias.shape), _resident(spread.shape),
    ]
    common = dict(
        grid=(b, N_KVH),
        out_specs=pl.BlockSpec((1, t, gw), lambda i, h: (i, 0, h)),
        out_shape=jax.ShapeDtypeStruct((b, t, N_QH * HEAD_DIM), BF16),
        compiler_params=_params("parallel", "parallel"),
    )
    acc = pltpu.VMEM((rows, LANES), F32)

    def bounded():
        return pl.pallas_call(
            functools.partial(_nsa_attn_bounded_kernel, tq=tq, tk=tk),
            in_specs=[pl.BlockSpec(memory_space=pltpu.SMEM)] + specs,
            scratch_shapes=[acc], name="nsa_attn_bounded", **common,
        )(score_bound, *operands)

    def general():
        return pl.pallas_call(
            functools.partial(_nsa_attn_kernel, tq=tq, tk=tk),
            in_specs=specs,
            scratch_shapes=[pltpu.VMEM((t // tk, rows, tk), F32), pltpu.VMEM((rows, LANES), F32), acc],
            name="nsa_attn", **common,
        )(*operands)

    return lax.cond(jnp.max(score_bound) <= MAX_SINGLE_PASS_SCORE, bounded, general)


def mlstm_mixer(h2, b, t, g, w_in, b_if, g_head):
    qk, v, og, gates = mlstm_proj(h2, g, w_in, b_if)
    L = MLSTM_CHUNK
    hp = MLSTM_HEADS_PER_STEP
    gates_t = gates[:, :2 * M_HEADS].reshape(b, t // L, L, 2, M_HEADS // hp, hp).transpose(0, 4, 3, 5, 1, 2)
    hn = mlstm_chunk(qk.reshape(b, t, -1), v.reshape(b, t, -1), og.reshape(b, t, -1), gates_t, g_head)
    return hn.reshape(b * t, -1)


def nsa_shared_kv(h2, b, t, kv_norm, kv_w, cmp_pe, cmp_w1, cmp_w2, k_norm):
    cmp_x, *packs = kv_proj(h2, t, kv_norm, kv_w, k_norm)
    cmpk, cmpv = cmp_kv(cmp_x, b, t, cmp_pe, cmp_w1, cmp_w2, k_norm[0])
    return (cmpk, cmpv) + tuple(p.reshape(b, t, -1) for p in packs)


def nsa_mixer(h2, b, t, kv, g, w_in, q_norm, k_norm):
    q, gates = nsa_proj(h2, g, w_in, q_norm)
    q_len = jnp.max(jnp.abs(q_norm)) * (LOG2E * 1.01)
    k_len = jnp.max(jnp.abs(k_norm), axis=1) * (HEAD_DIM ** 0.5)
    o = nsa_attn(q.reshape(b, t, -1), gates.reshape(b, t, -1), kv, (q_len * k_len).astype(F32))
    return o.reshape(b * t, -1)


def kernel(x, ffn_norm, ffn_w_in, ffn_w_out, mix_norm, a_w_in, a_b_if, a_g_head, a_w_out, kv_norm, kv_w, cmp_pe, cmp_w1, cmp_w2, k_norm, b_w_in, b_q_norm, b_w_out):
    b, t, d = x.shape
    depth = ffn_norm.shape[0]
    n_a = a_w_in.shape[0]
    h = x.reshape(b * t, d)
    w_in_bf, w_out_bf = ffn_w_in, ffn_w_out
    kv = None
    for layer in range(depth):
        h = ffn(h, ffn_norm[layer, 0], w_in_bf, w_out_bf, layer, 0)
        if layer < n_a:
            mixed = mlstm_mixer(h, b, t, mix_norm[layer], a_w_in[layer], a_b_if[layer], a_g_head[layer])
            w_mix = a_w_out[layer]
        else:
            j = layer - n_a
            mixed = nsa_mixer(h, b, t, kv, mix_norm[layer], b_w_in[j], b_q_norm[j], k_norm)
            w_mix = b_w_out[j]
        h = ffn(h, ffn_norm[layer, 1], w_in_bf, w_out_bf, layer, 1, mixer=(mixed, w_mix))
        if layer == n_a - 1:
            kv = nsa_shared_kv(h, b, t, kv_norm, kv_w, cmp_pe, cmp_w1, cmp_w2, k_norm)
    return h.reshape(b, t, d)
```

```python
import functools

import jax
import jax.numpy as jnp
import numpy as np
from jax import lax
from jax.experimental import pallas as pl
from jax.experimental.pallas import tpu as pltpu

F32 = jnp.float32
BF16 = jnp.bfloat16

EPS = 1e-6
NEG = -1e30

M_HEADS = 8
M_DV = 128
M_DQK = 64

N_QH = 16
N_KVH = 4
GROUP = N_QH // N_KVH
HEAD_DIM = 64
CMP_BLOCK = 32
CMP_STRIDE = 16
CMP_HIDDEN = 256
SEL_BLOCK = 64
SEL_TOPK = 16
WINDOW = 512
FORCE_SCORE = 1e4
LOG2E = 1.4426950408889634
SEL_BIAS = -2.0 ** 100
SEL_BLOCK_SHIFT = 6
MAX_SINGLE_PASS_SCORE = 48.0

LANES = 128
VMEM_LIMIT = 56 * 1024 * 1024

TOKEN_TILE = 512
FFN_CHUNK = 256
MLSTM_CHUNK = 256
MLSTM_HEADS_PER_STEP = 4
Q_TILE = 256
Q_TILE_SINGLE_PASS = 512
SEL_KEY_CHUNK = 512

_NT = (((1,), (1,)), ((), ()))


def _params(*sem):
    return pltpu.CompilerParams(dimension_semantics=sem, vmem_limit_bytes=VMEM_LIMIT)


def _rms(x, g):
    ms = jnp.mean(x * x, axis=-1, keepdims=True)
    return (x * lax.rsqrt(ms + EPS)) * g


def _dot(a, b):
    return jnp.dot(a, b, preferred_element_type=F32)


def _dot_nt(a, b, precision=None):
    return lax.dot_general(a, b, _NT, precision=precision, preferred_element_type=F32)


def _resident(shape):
    nd = len(shape)
    return pl.BlockSpec(shape, lambda *_: (0,) * nd)


def _ffn_kernel(*refs, mixer_proj):
    if mixer_proj:
        x_ref, a_ref, wmix_ref, g_ref, win_ref, wout_ref, o_ref, xn_ref, acc_ref, x_scr = refs
        x_scr[...] = x_ref[...] + _dot(a_ref[...], wmix_ref[...])
        x_ref = x_scr
    else:
        x_ref, g_ref, win_ref, wout_ref, o_ref, xn_ref, acc_ref = refs
    f = wout_ref.shape[0]
    x = x_ref[...]
    xn_ref[...] = _rms(x, g_ref[...]).astype(BF16)
    for c in range(f // FFN_CHUNK):
        cols = slice(c * FFN_CHUNK, (c + 1) * FFN_CHUNK)
        up_cols = slice(f + c * FFN_CHUNK, f + (c + 1) * FFN_CHUNK)
        xn = xn_ref[...]
        a = _dot(xn, win_ref[:, cols].astype(BF16))
        b = _dot(xn, win_ref[:, up_cols].astype(BF16))
        hid = ((a * jax.nn.sigmoid(a)) * b).astype(BF16)
        part = _dot(hid, wout_ref[cols, :].astype(BF16))
        if c == 0:
            acc_ref[...] = part
        else:
            acc_ref[...] += part
    o_ref[...] = x_ref[...] + 0.5 * acc_ref[...]


def ffn(x2, g, w_in_all, w_out_all, layer, j, mixer=None):
    n, d = x2.shape
    f = w_out_all.shape[2]
    assert f % FFN_CHUNK == 0
    tm = TOKEN_TILE
    tile = lambda w: pl.BlockSpec((tm, w), lambda i: (i, 0))
    ins, specs, scratch = [x2], [tile(d)], [pltpu.VMEM((tm, d), BF16), pltpu.VMEM((tm, d), F32)]
    if mixer is not None:
        a2, w_mix = mixer
        ins += [a2, w_mix.astype(BF16)]
        specs += [tile(a2.shape[1]), _resident(w_mix.shape)]
        scratch.append(pltpu.VMEM((tm, d), F32))
    return pl.pallas_call(
        functools.partial(_ffn_kernel, mixer_proj=mixer is not None),
        grid=(n // tm,),
        in_specs=specs + [
            _resident((1, d)),
            pl.BlockSpec((None, None, d, 2 * f), lambda i: (layer, j, 0, 0), pipeline_mode=pl.Buffered(1)),
            pl.BlockSpec((None, None, f, d), lambda i: (layer, j, 0, 0), pipeline_mode=pl.Buffered(1)),
        ],
        out_specs=tile(d),
        out_shape=jax.ShapeDtypeStruct((n, d), F32),
        scratch_shapes=scratch,
        compiler_params=_params("parallel"),
        name="ffn",
    )(*ins, g.reshape(1, d), w_in_all, w_out_all)


def _mlstm_proj_kernel(x_ref, g_ref, wqk_ref, wv_ref, wog_ref, wg_ref, bg_ref,
                       qk_ref, v_ref, og_ref, gate_ref):
    xn = _rms(x_ref[...], g_ref[...]).astype(BF16)
    qk = _dot(xn, wqk_ref[...])
    is_q = lax.broadcasted_iota(jnp.int32, (1, qk.shape[1]), 1) < qk.shape[1] // 2
    qk_ref[...] = (qk * jnp.where(is_q, M_DQK ** -0.5, 1.0)).astype(BF16)
    v_ref[...] = _dot(xn, wv_ref[...]).astype(BF16)
    og_ref[...] = jax.nn.sigmoid(_dot(xn, wog_ref[...])).astype(BF16)
    gate_ref[...] = _dot(xn, wg_ref[...]) + bg_ref[...]


def mlstm_proj(h2, g, w_in, b_if):
    n, d = h2.shape
    qk_w, v_w = M_HEADS * M_DQK, M_HEADS * M_DV
    wb = w_in.astype(BF16)
    wqk = wb[:, :2 * qk_w]
    wv = wb[:, 2 * qk_w:2 * qk_w + v_w]
    wgate = jnp.pad(wb[:, 2 * qk_w + v_w:2 * qk_w + v_w + 2 * M_HEADS], ((0, 0), (0, LANES - 2 * M_HEADS)))
    wog = wb[:, 2 * qk_w + v_w + 2 * M_HEADS:]
    bg = jnp.pad(b_if.astype(F32).reshape(1, 2 * M_HEADS), ((0, 0), (0, LANES - 2 * M_HEADS)))
    tm = TOKEN_TILE
    tile = lambda w: pl.BlockSpec((tm, w), lambda i: (i, 0))
    return pl.pallas_call(
        _mlstm_proj_kernel,
        grid=(n // tm,),
        in_specs=[tile(d), _resident((1, d)), _resident(wqk.shape), _resident(wv.shape),
                  _resident(wog.shape), _resident(wgate.shape), _resident(bg.shape)],
        out_specs=[tile(2 * qk_w), tile(v_w), tile(d), tile(LANES)],
        out_shape=[jax.ShapeDtypeStruct((n, 2 * qk_w), BF16), jax.ShapeDtypeStruct((n, v_w), BF16),
                   jax.ShapeDtypeStruct((n, d), BF16), jax.ShapeDtypeStruct((n, LANES), F32)],
        compiler_params=_params("parallel"),
        name="mlstm_proj",
    )(h2, g.reshape(1, d), wqk, wv, wog, wgate, bg)


def _log_sigmoid(x):
    return jnp.minimum(x, 0.0) - jnp.log1p(jnp.exp(-jnp.abs(x)))


def _mlstm_chunk_kernel(q_ref, k_ref, v_ref, og_ref, gate_ref, ghead_ref, o_ref, c_ref, g_r_ref, b_r_ref, cm_r_ref):
    heads = c_ref.shape[0]
    first_head = pl.program_id(1) * heads
    L = MLSTM_CHUNK
    n_chunks = q_ref.shape[1] // L

    r_i = lax.broadcasted_iota(jnp.int32, (L, L), 0)
    c_i = lax.broadcasted_iota(jnp.int32, (L, L), 1)
    causal = c_i <= r_i
    tril = causal.astype(F32)
    lane = lax.broadcasted_iota(jnp.int32, (L, LANES), 1)
    ones_blk = jnp.ones((L, LANES), BF16)

    c_ref[...] = jnp.zeros_like(c_ref)

    t_i = lax.broadcasted_iota(jnp.int32, (n_chunks, L), 1)
    for hh in range(heads):
        b = _dot_nt(_log_sigmoid(gate_ref[0, 0, 1, hh]), tril, lax.Precision.HIGHEST)
        g = gate_ref[0, 0, 0, hh] - b
        cm = g
        shift = 1
        while shift < L:
            cm = jnp.maximum(cm, jnp.where(t_i >= shift, pltpu.roll(cm, shift, 1), -jnp.inf))
            shift *= 2
        b_r_ref[hh] = b
        g_r_ref[hh] = g
        cm_r_ref[hh] = cm

    def column(row):
        return jnp.broadcast_to(row, (LANES, L)).T

    def body(c, m_prev):
        base = pl.multiple_of(c * L, L)
        m_next = []
        for hh in range(heads):
            head = first_head + hh
            pair_cols = slice((hh // 2) * LANES, (hh // 2 + 1) * LANES)
            in_head = (lane >= (hh % 2) * M_DQK) & (lane < (hh % 2 + 1) * M_DQK)
            qh = q_ref[0, pl.ds(base, L), pair_cols]
            kh = jnp.where(in_head, k_ref[0, pl.ds(base, L), pair_cols], 0)
            v_aug = jnp.concatenate([v_ref[0, pl.ds(base, L), hh * M_DV:(hh + 1) * M_DV], ones_blk], axis=1)

            g_r = g_r_ref[hh, pl.ds(c, 1), :]
            b_r = b_r_ref[hh, pl.ds(c, 1), :]
            g_tot = b_r[:, L - 1:L]
            m_p = m_prev[hh]
            c_prev = c_ref[hh]

            a_r = g_tot + g_r
            m_loc = jnp.max(a_r, axis=1, keepdims=True)
            kw_t = (kh.astype(F32).T * jnp.exp(a_r - m_loc)).astype(BF16)
            c_loc = _dot(kw_t, v_aug)

            mm = jnp.maximum(m_p, column(cm_r_ref[hh, pl.ds(c, 1), :]))
            mm_wide = jnp.concatenate([mm] * (L // LANES), axis=1)
            w = jnp.where(causal, jnp.exp(g_r - mm_wide), 0.0) * _dot_nt(qh, kh)
            num = jnp.concatenate([jnp.exp(m_p - mm)] * 2, axis=1) * _dot(qh, c_prev.astype(BF16)) \
                + _dot(w.astype(BF16), v_aug)
            floor = jnp.exp(-(column(b_r) + mm))
            h_out = num[:, :M_DV] / jnp.maximum(jnp.abs(num[:, M_DV:]), floor)
            h_out = _rms(h_out, ghead_ref[pl.ds(head, 1), :])
            cols = slice(hh * M_DV, (hh + 1) * M_DV)
            o_ref[0, pl.ds(base, L), cols] = (h_out * og_ref[0, pl.ds(base, L), cols]).astype(BF16)

            m_new = jnp.maximum(g_tot + m_p, m_loc)
            c_ref[hh] = jnp.exp(g_tot + m_p - m_new) * c_prev + jnp.exp(m_loc - m_new) * c_loc
            m_next.append(m_new)
        return tuple(m_next)

    lax.fori_loop(0, n_chunks, body, (jnp.zeros((1, 1), F32),) * heads)


def mlstm_chunk(qk, v, og, gates_t, g_head):
    b, t, _ = v.shape
    L = MLSTM_CHUNK
    nc = t // L
    hp = MLSTM_HEADS_PER_STEP
    groups = M_HEADS // hp
    qk_cols, v_cols = hp * M_DQK, hp * M_DV
    row_scratch = pltpu.VMEM((hp, nc, L), F32)
    return pl.pallas_call(
        _mlstm_chunk_kernel,
        grid=(b, groups),
        in_specs=[
            pl.BlockSpec((1, t, qk_cols), lambda i, p: (i, 0, p)),
            pl.BlockSpec((1, t, qk_cols), lambda i, p: (i, 0, groups + p)),
            pl.BlockSpec((1, t, v_cols), lambda i, p: (i, 0, p)),
            pl.BlockSpec((1, t, v_cols), lambda i, p: (i, 0, p)),
            pl.BlockSpec((1, 1, 2, hp, nc, L), lambda i, p: (i, p, 0, 0, 0, 0)),
            _resident((M_HEADS, M_DV)),
        ],
        out_specs=pl.BlockSpec((1, t, v_cols), lambda i, p: (i, 0, p)),
        out_shape=jax.ShapeDtypeStruct((b, t, M_HEADS * M_DV), BF16),
        scratch_shapes=[pltpu.VMEM((hp, 2 * M_DQK, 2 * M_DV), F32), row_scratch, row_scratch, row_scratch],
        compiler_params=_params("parallel", "parallel"),
        name="mlstm_chunk",
    )(qk, qk, v, og, gates_t, g_head.astype(F32))


def _kv_packs(y, gain, extra):
    lane = lax.broadcasted_iota(jnp.int32, (1, LANES), 1)
    lo = lane < HEAD_DIM
    kp, vp = [], []
    for h in range(N_KVH):
        yh = y[:, h * LANES:(h + 1) * LANES]
        ms = jnp.sum(jnp.where(lo, yh * yh, 0.0), axis=-1, keepdims=True) * (1.0 / HEAD_DIM)
        kp.append(jnp.where(lo, (yh * lax.rsqrt(ms + EPS)) * gain, extra))
        vp.append(jnp.where(lo, 1.0, yh))
    return jnp.concatenate(kp, axis=1), jnp.concatenate(vp, axis=1)


def _kv_proj_kernel(x_ref, g_ref, wc_ref, ws_ref, ww_ref, gs_ref, gw_ref,
                    cmp_ref, selk_ref, selv_ref, wink_ref, winv_ref, y_ref, *, seq_len):
    tm = x_ref.shape[0]
    xn = _rms(x_ref[...], g_ref[...]).astype(BF16)

    y = _dot(xn, wc_ref[...])
    n_slab = y.shape[1] // LANES
    for s in range(n_slab):
        y_ref[s] = y[:, s * LANES:(s + 1) * LANES]
    groups = tm // CMP_STRIDE
    lo = lax.broadcasted_iota(jnp.int32, (1, LANES), 1) < HEAD_DIM
    for s in range(n_slab):
        for rp in range(CMP_STRIDE // 2):
            even = y_ref[s, pl.ds(2 * rp, groups, stride=CMP_STRIDE), :]
            odd = y_ref[s, pl.ds(2 * rp + 1, groups, stride=CMP_STRIDE), :]
            cols = slice(rp * LANES, (rp + 1) * LANES)
            cmp_ref[2 * s, :, cols] = jnp.where(lo, even, pltpu.roll(odd, HEAD_DIM, 1))
            cmp_ref[2 * s + 1, :, cols] = jnp.where(lo, pltpu.roll(even, HEAD_DIM, 1), odd)

    pos = (pl.program_id(0) * tm) % seq_len + lax.broadcasted_iota(jnp.int32, (tm, 1), 0)
    lane = lax.broadcasted_iota(jnp.int32, (1, LANES), 1)
    block_bias = jnp.where(lane == HEAD_DIM + lax.shift_right_logical(pos, SEL_BLOCK_SHIFT), SEL_BIAS, 0.0)
    kp, vp = _kv_packs(_dot(xn, ws_ref[...]), gs_ref[...], block_bias)
    selk_ref[...] = kp.astype(BF16)
    selv_ref[...] = vp.astype(BF16)
    kp, vp = _kv_packs(_dot(xn, ww_ref[...]), gw_ref[...], 0.0)
    wink_ref[...] = kp.astype(BF16)
    winv_ref[...] = vp.astype(BF16)


def _kv_pack_weights(wk, wv):
    d = wk.shape[0]
    wk = wk.reshape(d, N_KVH, HEAD_DIM)
    wv = wv.reshape(d, N_KVH, HEAD_DIM)
    return jnp.concatenate([wk, wv], axis=-1).reshape(d, N_KVH * LANES)


def _k_gain_row(gain):
    return jnp.concatenate([gain.astype(F32), jnp.ones((HEAD_DIM,), F32)]).reshape(1, LANES)


def kv_proj(h2, t, g, kv_w, k_norm):
    n, d = h2.shape
    wb = kv_w.astype(BF16)
    kc, vc, ks, vs, kw, vw = jnp.split(wb, 6, axis=1)
    wcmp = jnp.concatenate([kc, vc], axis=1)
    wsel = _kv_pack_weights(ks, vs)
    wwin = _kv_pack_weights(kw, vw)
    wd = N_KVH * LANES
    tm = TOKEN_TILE
    assert t % tm == 0 and t // SEL_BLOCK <= LANES - HEAD_DIM
    tile = lambda w: pl.BlockSpec((tm, w), lambda i: (i, 0))
    packed = jax.ShapeDtypeStruct((n, wd), BF16)
    feat = CMP_STRIDE * HEAD_DIM
    groups = tm // CMP_STRIDE
    return pl.pallas_call(
        functools.partial(_kv_proj_kernel, seq_len=t),
        grid=(n // tm,),
        in_specs=[tile(d), _resident((1, d)), _resident(wcmp.shape), _resident(wsel.shape),
                  _resident(wwin.shape), _resident((1, LANES)), _resident((1, LANES))],
        out_specs=[pl.BlockSpec((2 * N_KVH, groups, feat), lambda i: (0, i, 0))] + [tile(wd)] * 4,
        out_shape=[jax.ShapeDtypeStruct((2 * N_KVH, n // CMP_STRIDE, feat), F32), packed, packed, packed, packed],
        scratch_shapes=[pltpu.VMEM((2 * N_KVH * HEAD_DIM // LANES, tm, LANES), F32)],
        compiler_params=_params("parallel"),
        name="kv_proj",
    )(h2, g.reshape(1, d), wcmp, wsel, wwin, _k_gain_row(k_norm[1]), _k_gain_row(k_norm[2]))


def _cmp_kernel(xk_ref, xv_ref, pe_ref, w1_ref, w2k_ref, w2v_ref, gk_ref, ok_ref, ov_ref):
    def compress(x, kv, w2):
        u = _dot((x + pe_ref[kv, 0]).astype(BF16), w1_ref[kv, 0])
        v = _dot((x + pe_ref[kv, 1]).astype(BF16), w1_ref[kv, 1])
        nrow = v.shape[0]
        hid = u + pltpu.roll(v, nrow - 1, 0)
        return _dot((hid * jax.nn.sigmoid(hid)).astype(BF16), w2)

    yk = compress(xk_ref[0], 0, w2k_ref[...])
    yv = compress(xv_ref[0], 1, w2v_ref[...])
    ms = jnp.sum(yk * yk, axis=-1, keepdims=True) * (1.0 / HEAD_DIM)
    ok_ref[0, 0] = ((yk * lax.rsqrt(ms + EPS)) * gk_ref[...]).astype(BF16)
    lane = lax.broadcasted_iota(jnp.int32, (1, LANES), 1)
    ov_ref[0, 0] = jnp.where(lane < HEAD_DIM, 1.0, yv).astype(BF16)


def cmp_kv(cmp_x, b, t, cmp_pe, cmp_w1, cmp_w2, k_gain):
    ng = t // CMP_STRIDE
    feat = CMP_STRIDE * HEAD_DIM
    pe = cmp_pe.astype(F32).reshape(2, 2, 1, feat)
    w1 = cmp_w1.astype(BF16).reshape(2, 2, feat, CMP_HIDDEN)
    zeros = jnp.zeros((CMP_HIDDEN, HEAD_DIM), BF16)
    w2k = jnp.concatenate([cmp_w2[0].astype(BF16), zeros], axis=1)
    w2v = jnp.concatenate([zeros, cmp_w2[1].astype(BF16)], axis=1)
    ospec = pl.BlockSpec((1, 1, ng, LANES), lambda i, h: (i, h, 0, 0))
    packed = jax.ShapeDtypeStruct((b, N_KVH, ng, LANES), BF16)
    return pl.pallas_call(
        _cmp_kernel,
        grid=(b, N_KVH),
        in_specs=[pl.BlockSpec((1, ng, feat), lambda i, h: (h, i, 0)),
                  pl.BlockSpec((1, ng, feat), lambda i, h: (N_KVH + h, i, 0)),
                  _resident(pe.shape), _resident(w1.shape), _resident(w2k.shape),
                  _resident(w2v.shape), _resident((1, LANES))],
        out_specs=[ospec, ospec],
        out_shape=[packed, packed],
        compiler_params=_params("parallel", "parallel"),
        name="cmp_kv",
    )(cmp_x, cmp_x, pe, w1, w2k, w2v, _k_gain_row(k_gain))


def _nsa_proj_kernel(x_ref, g_ref, wq_ref, wg_ref, qg_ref, q_ref, gate_ref):
    xn = _rms(x_ref[...], g_ref[...]).astype(BF16)
    q = _dot(xn, wq_ref[...])
    lane = lax.broadcasted_iota(jnp.int32, (1, LANES), 1)
    lo = lane < HEAD_DIM
    outs = []
    for cb in range(q.shape[1] // LANES):
        y = q[:, cb * LANES:(cb + 1) * LANES]
        y2 = y * y
        ms_lo = jnp.sum(jnp.where(lo, y2, 0.0), axis=-1, keepdims=True) * (1.0 / HEAD_DIM)
        ms_hi = jnp.sum(jnp.where(lo, 0.0, y2), axis=-1, keepdims=True) * (1.0 / HEAD_DIM)
        scale = jnp.where(lo, lax.rsqrt(ms_lo + EPS), lax.rsqrt(ms_hi + EPS))
        outs.append(((y * scale) * qg_ref[...]) * (HEAD_DIM ** -0.5 * LOG2E))
    q_ref[...] = jnp.concatenate(outs, axis=1).astype(BF16)
    gate_ref[...] = jax.nn.sigmoid(_dot(xn, wg_ref[...])).astype(BF16)


def nsa_proj(h2, g, w_in, q_norm):
    n, d = h2.shape
    wb = w_in.astype(BF16)
    wq = wb[:, :N_QH * HEAD_DIM]
    wg = wb[:, N_QH * HEAD_DIM:].reshape(d, 3, N_KVH, GROUP).transpose(0, 2, 1, 3).reshape(d, N_KVH * 3 * GROUP)
    wg = jnp.pad(wg, ((0, 0), (0, LANES - N_KVH * 3 * GROUP)))
    qg = jnp.tile(q_norm.astype(F32), LANES // HEAD_DIM).reshape(1, LANES)
    tm = TOKEN_TILE
    tile = lambda w: pl.BlockSpec((tm, w), lambda i: (i, 0))
    return pl.pallas_call(
        _nsa_proj_kernel,
        grid=(n // tm,),
        in_specs=[tile(d), _resident((1, d)), _resident(wq.shape), _resident(wg.shape), _resident((1, LANES))],
        out_specs=[tile(N_QH * HEAD_DIM), tile(LANES)],
        out_shape=[jax.ShapeDtypeStruct((n, N_QH * HEAD_DIM), BF16),
                   jax.ShapeDtypeStruct((n, LANES), BF16)],
        compiler_params=_params("parallel"),
        name="nsa_proj",
    )(h2, g.reshape(1, d), wq, wg, qg)


def _masked_exp2(s3, mask):
    sm = jnp.where(mask[None], s3, NEG)
    mx = jnp.max(sm, axis=-1, keepdims=True)
    return jnp.exp2(sm - mx), mx > 0.5 * NEG


def _normalise(acc, values_lo, guard_empty=False):
    lo = lax.broadcasted_iota(jnp.int32, acc.shape, 1) < HEAD_DIM
    den = pltpu.roll(acc, HEAD_DIM, 1)
    use = lo == values_lo
    if guard_empty:
        use = use & (den > 0.0)
    return acc / jnp.where(use, den, 1.0)


def _stacked_queries(q_ref, s0, tq):
    lo = lax.broadcasted_iota(jnp.int32, (tq, LANES), 1) < HEAD_DIM
    qs = []
    for g in range(GROUP):
        blk = q_ref[0, pl.ds(s0, tq), (g // 2) * LANES:(g // 2 + 1) * LANES].astype(F32)
        if g % 2 == 1:
            blk = pltpu.roll(blk, HEAD_DIM, 1)
        qs.append(jnp.where(lo, blk, 0.0))
    return jnp.concatenate(qs, axis=0)


def _dropped_blocks(imp, s0, tq, n_sel):
    lane = lax.broadcasted_iota(jnp.int32, (tq, LANES), 1)
    t_q = s0 + lax.broadcasted_iota(jnp.int32, (tq, 1), 0)
    valid = (lane * SEL_BLOCK <= t_q) & (lane < n_sel)
    cur = lax.shift_right_logical(t_q, SEL_BLOCK_SHIFT)
    forced = (lane == 0) | (lane == cur) | (lane == cur - 1)
    score = jnp.where(forced & valid, FORCE_SCORE, jnp.where(valid, imp, -1.0))
    score = jnp.where(lane < n_sel, score, -2.0)
    s_t = score.T[:HEAD_DIM]
    sub = lax.broadcasted_iota(jnp.int32, (8, tq), 0)
    groups = [s_t[8 * v:8 * v + 8] for v in range(HEAD_DIM // 8)]
    ranks = [jnp.zeros((8, tq), F32) for _ in groups]
    for i in range(n_sel):
        s_i = s_t[i:i + 1, :]
        for v, s_v in enumerate(groups):
            if 8 * v + 7 < i:
                ahead = s_i > s_v
            elif 8 * v > i:
                ahead = s_i >= s_v
            else:
                ahead = (s_i > s_v) | ((s_i == s_v) & (sub > i - 8 * v))
            ranks[v] = ranks[v] + jnp.where(ahead, 1.0, 0.0)
    rank = jnp.concatenate(ranks, axis=0)
    blk_t = lax.broadcasted_iota(jnp.int32, (HEAD_DIM, tq), 0)
    t_row = s0 + lax.broadcasted_iota(jnp.int32, (1, tq), 1)
    dropped_t = jnp.where((rank < SEL_TOPK) & (blk_t * SEL_BLOCK <= t_row), 0.0, 1.0)
    return jnp.concatenate([jnp.zeros((HEAD_DIM, tq), F32), dropped_t], axis=0).T


def _store_gated(o_ref, gate_ref, spread_ref, s0, tq, o_c, o_s, o_w):
    g_rep = _dot(gate_ref[0, pl.ds(s0, tq), :], spread_ref[pl.program_id(1)])

    def gate(branch):
        first = branch * GROUP
        return jnp.concatenate([g_rep[:, (first + g) * LANES:(first + g + 1) * LANES] for g in range(GROUP)], axis=0)

    o = (gate(0) * o_c + gate(1) * o_s + gate(2) * o_w).reshape(GROUP, tq, LANES)
    lo = lax.broadcasted_iota(jnp.int32, (tq, LANES), 1) < HEAD_DIM
    out01 = jnp.where(lo, pltpu.roll(o[0], HEAD_DIM, 1), o[1])
    out23 = jnp.where(lo, pltpu.roll(o[2], HEAD_DIM, 1), o[3])
    o_ref[0, pl.ds(s0, tq), :] = jnp.concatenate([out01, out23], axis=1).astype(BF16)


def _chunk_groups(n, per_chunk):
    def trip(i, carry):
        for u in range(4):
            per_chunk(4 * i + u)
        return carry

    lax.fori_loop(0, n // 4, trip, 0)
    done = 4 * (n // 4)

    @pl.when(n - done >= 2)
    def _():
        per_chunk(done)
        per_chunk(done + 1)

    @pl.when((n - done) % 2 == 1)
    def _():
        per_chunk(n - 1)


def _nsa_attn_bounded_kernel(q_ref, gate_ref, cmpk_ref, cmpv_ref, selk_ref, selv_ref, wink_ref, winv_ref,
                             ov_ref, wbias_ref, spread_ref, o_ref, acc_ref, *, tq, tk):
    rows = GROUP * tq
    n_sel = selk_ref.shape[1] // SEL_BLOCK

    def tile(tile_idx, carry):
        s0 = pl.multiple_of(tile_idx * tq, tq)
        q4 = _stacked_queries(q_ref, s0, tq)
        qst = q4.astype(BF16)
        t_q = s0 + lax.broadcasted_iota(jnp.int32, (tq, 1), 0)

        kc = cmpk_ref[0, 0]
        n_cmp = kc.shape[0]
        cmp_end = lax.broadcasted_iota(jnp.int32, (1, n_cmp), 1) * CMP_STRIDE + (CMP_BLOCK - 1)
        s_c = jnp.where((cmp_end <= t_q)[None], _dot_nt(qst, kc).reshape(GROUP, tq, n_cmp), NEG)
        e_c = jnp.exp2(s_c).reshape(rows, n_cmp).astype(BF16)
        both = _dot(e_c, jnp.concatenate([cmpv_ref[0, 0], ov_ref[...]], axis=1))
        o_c = _normalise(both[:, :LANES], False, guard_empty=True)
        imp4 = _normalise(both[:, LANES:], True, guard_empty=True).reshape(GROUP, tq, LANES)
        dropped = _dropped_blocks(imp4[0] + imp4[1] + imp4[2] + imp4[3], s0, tq, n_sel)
        qx = (q4 + jnp.concatenate([dropped] * GROUP, axis=0)).astype(BF16)

        wk = WINDOW + tq
        w0 = pl.multiple_of(jnp.maximum(s0 - WINDOW, 0), tq)
        s_w = _dot_nt(qst, wink_ref[0, pl.ds(w0, wk), :]).reshape(GROUP, tq, wk)
        s_w = s_w + wbias_ref[jnp.minimum(tile_idx, WINDOW // tq)][None]
        e_w = jnp.exp2(s_w).reshape(rows, wk).astype(BF16)
        o_w = _normalise(_dot(e_w, winv_ref[0, pl.ds(w0, wk), :]), False)

        last = s0 // tk
        acc_ref[...] = jnp.zeros_like(acc_ref)

        def accumulate(c, causal=False):
            base = pl.multiple_of(c * tk, tk)
            s = _dot_nt(qx, selk_ref[0, pl.ds(base, tk), :])
            if causal:
                kpos = base + lax.broadcasted_iota(jnp.int32, (1, tk), 1)
                s = jnp.where((kpos <= t_q)[None], s.reshape(GROUP, tq, tk), NEG).reshape(rows, tk)
            acc_ref[...] += _dot(jnp.exp2(s).astype(BF16), selv_ref[0, pl.ds(base, tk), :])

        _chunk_groups(last, accumulate)
        accumulate(last, causal=True)
        o_s = _normalise(acc_ref[...], False)

        _store_gated(o_ref, gate_ref, spread_ref, s0, tq, o_c, o_s, o_w)
        return carry

    lax.fori_loop(0, q_ref.shape[1] // tq, tile, 0)


def _nsa_attn_kernel(q_ref, gate_ref, cmpk_ref, cmpv_ref, selk_ref, selv_ref, wink_ref, winv_ref,
                     ov_ref, wbias_ref, spread_ref, o_ref, s_ref, mx_ref, acc_ref, *, tq, tk):
    rows = GROUP * tq

    def tile(tile_idx, carry):
        s0 = pl.multiple_of(tile_idx * tq, tq)

        lane = lax.broadcasted_iota(jnp.int32, (tq, LANES), 1)
        lo = lane < HEAD_DIM
        qs = []
        for g in range(GROUP):
            blk = q_ref[0, pl.ds(s0, tq), (g // 2) * LANES:(g // 2 + 1) * LANES].astype(F32)
            if g % 2 == 1:
                blk = pltpu.roll(blk, HEAD_DIM, 1)
            qs.append(jnp.where(lo, blk, 0.0))
        q4 = jnp.concatenate(qs, axis=0)
        qst = q4.astype(BF16)
        t_q = s0 + lax.broadcasted_iota(jnp.int32, (tq, 1), 0)

        kc = cmpk_ref[0, 0]
        n_cmp = kc.shape[0]
        cmp_end = lax.broadcasted_iota(jnp.int32, (1, n_cmp), 1) * CMP_STRIDE + (CMP_BLOCK - 1)
        e_c, ok_c = _masked_exp2(_dot_nt(qst, kc).reshape(GROUP, tq, n_cmp), cmp_end <= t_q)
        e_c = e_c.reshape(rows, n_cmp)
        ok_c = ok_c.reshape(rows, 1)
        both = _dot(e_c.astype(BF16), jnp.concatenate([cmpv_ref[0, 0], ov_ref[...]], axis=1))
        o_c = jnp.where(ok_c, _normalise(both[:, :LANES], False), 0.0)

        imp4 = jnp.where(ok_c, _normalise(both[:, LANES:], True), 0.0).reshape(GROUP, tq, LANES)
        imp = imp4[0] + imp4[1] + imp4[2] + imp4[3]

        n_sel = selk_ref.shape[1] // SEL_BLOCK
        valid = (lane * SEL_BLOCK <= t_q) & (lane < n_sel)
        cur = lax.shift_right_logical(t_q, SEL_BLOCK_SHIFT)
        forced = (lane == 0) | (lane == cur) | (lane == cur - 1)
        score = jnp.where(forced & valid, FORCE_SCORE, jnp.where(valid, imp, -1.0))
        score = jnp.where(lane < n_sel, score, -2.0)
        s_t = score.T[:HEAD_DIM]
        sub = lax.broadcasted_iota(jnp.int32, (8, tq), 0)
        groups = [s_t[8 * v:8 * v + 8] for v in range(HEAD_DIM // 8)]
        ranks = [jnp.zeros((8, tq), F32) for _ in groups]
        for i in range(n_sel):
            s_i = s_t[i:i + 1, :]
            for v, s_v in enumerate(groups):
                if 8 * v + 7 < i:
                    ahead = s_i > s_v
                elif 8 * v > i:
                    ahead = s_i >= s_v
                else:
                    ahead = (s_i > s_v) | ((s_i == s_v) & (sub > i - 8 * v))
                ranks[v] = ranks[v] + jnp.where(ahead, 1.0, 0.0)
        rank = jnp.concatenate(ranks, axis=0)
        blk_t = lax.broadcasted_iota(jnp.int32, (HEAD_DIM, tq), 0)
        t_row = s0 + lax.broadcasted_iota(jnp.int32, (1, tq), 1)
        dropped_t = jnp.where((rank < SEL_TOPK) & (blk_t * SEL_BLOCK <= t_row), 0.0, 1.0)
        dropped = jnp.concatenate([jnp.zeros((HEAD_DIM, tq), F32), dropped_t], axis=0).T
        qx = (q4 + jnp.concatenate([dropped] * GROUP, axis=0)).astype(BF16)

        wk = WINDOW + tq
        w0 = pl.multiple_of(jnp.maximum(s0 - WINDOW, 0), tq)
        s_w = _dot_nt(qst, wink_ref[0, pl.ds(w0, wk), :]).reshape(GROUP, tq, wk)
        s_w = s_w + wbias_ref[jnp.minimum(tile_idx, WINDOW // tq)][None]
        e_w = jnp.exp2(s_w - jnp.max(s_w, axis=-1, keepdims=True))
        o_w = _normalise(_dot(e_w.reshape(rows, wk).astype(BF16), winv_ref[0, pl.ds(w0, wk), :]), False)

        nblk = tk // LANES
        last = s0 // tk
        mx_ref[...] = jnp.full(mx_ref.shape, NEG, F32)

        def scores(c, causal):
            base = pl.multiple_of(c * tk, tk)
            s = _dot_nt(qx, selk_ref[0, pl.ds(base, tk), :])
            if causal:
                kpos = base + lax.broadcasted_iota(jnp.int32, (1, tk), 1)
                s = jnp.where((kpos <= t_q)[None], s.reshape(GROUP, tq, tk), NEG).reshape(rows, tk)
            s_ref[c] = s
            m = mx_ref[...]
            for j in range(nblk):
                m = jnp.maximum(m, s[:, j * LANES:(j + 1) * LANES])
            mx_ref[...] = m

        def pass1(i, carry):
            for u in range(4):
                scores(4 * i + u, causal=False)
            return carry

        lax.fori_loop(0, last // 4, pass1, 0)
        done = 4 * (last // 4)

        @pl.when(last - done >= 2)
        def _():
            scores(done, causal=False)
            scores(done + 1, causal=False)

        @pl.when((last - done) % 2 == 1)
        def _():
            scores(last - 1, causal=False)

        scores(last, causal=True)
        mx_ref[...] = jnp.broadcast_to(jnp.max(mx_ref[...], axis=-1, keepdims=True), mx_ref.shape)

        acc_ref[...] = jnp.zeros_like(acc_ref)

        def weighted(c):
            base = pl.multiple_of(c * tk, tk)
            s = s_ref[c]
            m = mx_ref[...]
            p = jnp.concatenate([jnp.exp2(s[:, j * LANES:(j + 1) * LANES] - m) for j in range(nblk)], axis=1)
            return _dot(p.astype(BF16), selv_ref[0, pl.ds(base, tk), :])

        def pass2(i, carry):
            acc_ref[...] += (weighted(4 * i) + weighted(4 * i + 1)) + (weighted(4 * i + 2) + weighted(4 * i + 3))
            return carry

        n_chunks = last + 1
        lax.fori_loop(0, n_chunks // 4, pass2, 0)
        done2 = 4 * (n_chunks // 4)

        @pl.when(n_chunks - done2 >= 2)
        def _():
            acc_ref[...] += weighted(done2) + weighted(done2 + 1)

        @pl.when((n_chunks - done2) % 2 == 1)
        def _():
            acc_ref[...] += weighted(last)

        o_s = _normalise(acc_ref[...], False)

        g_rep = _dot(gate_ref[0, pl.ds(s0, tq), :], spread_ref[pl.program_id(1)])

        def gate(branch):
            first = branch * GROUP
            return jnp.concatenate([g_rep[:, (first + g) * LANES:(first + g + 1) * LANES] for g in range(GROUP)], axis=0)

        o = (gate(0) * o_c + gate(1) * o_s + gate(2) * o_w).reshape(GROUP, tq, LANES)
        out01 = jnp.where(lo, pltpu.roll(o[0], HEAD_DIM, 1), o[1])
        out23 = jnp.where(lo, pltpu.roll(o[2], HEAD_DIM, 1), o[3])
        o_ref[0, pl.ds(s0, tq), :] = jnp.concatenate([out01, out23], axis=1).astype(BF16)
        return carry

    lax.fori_loop(0, q_ref.shape[1] // tq, tile, 0)


def _overlap_matrix(n_cmp_rows, t):
    n_sel = t // SEL_BLOCK
    c0 = np.arange(n_cmp_rows) * CMP_STRIDE
    s_0 = np.arange(n_sel) * SEL_BLOCK
    ov = (c0[:, None] < s_0[None, :] + SEL_BLOCK) & (c0[:, None] + CMP_BLOCK > s_0[None, :])
    ov[(t - CMP_BLOCK) // CMP_STRIDE + 1:] = False
    out = np.zeros((n_cmp_rows, LANES), np.float32)
    out[:, :n_sel] = ov
    out[:, HEAD_DIM:] = 1.0
    return jnp.asarray(out, BF16)


def _window_bias(tq):
    off = np.arange(WINDOW // tq + 1).reshape(-1, 1, 1) * tq
    i = np.arange(tq).reshape(1, tq, 1)
    j = np.arange(WINDOW + tq).reshape(1, 1, -1)
    visible = (j <= off + i) & (j > off + i - WINDOW)
    return jnp.asarray(np.where(visible, 0.0, NEG).astype(np.float32))


def _gate_spread():
    n_gate = 3 * GROUP
    head = np.arange(N_KVH).reshape(N_KVH, 1, 1)
    src = np.arange(LANES).reshape(1, LANES, 1)
    dst_group = np.arange(n_gate * LANES).reshape(1, 1, -1) // LANES
    return jnp.asarray((src == n_gate * head + dst_group).astype(np.float32), BF16)


def nsa_attn(q, gates, kv, score_bound):
    cmpk, cmpv, selk, selv, wink, winv = kv
    b, t, _ = q.shape
    tk = min(SEL_KEY_CHUNK, t)
    n_cmp_rows = cmpk.shape[2]
    assert t // SEL_BLOCK <= LANES - HEAD_DIM and (t // SEL_BLOCK) % 8 == 0 and t % tk == 0
    ov = _overlap_matrix(n_cmp_rows, t)
    spread = _gate_spread()
    gw = GROUP * HEAD_DIM
    kv_spec = pl.BlockSpec((1, t, LANES), lambda i, h: (i, 0, h))
    cmp_spec = pl.BlockSpec((1, 1, n_cmp_rows, LANES), lambda i, h: (i, h, 0, 0))

    def call(kern, name, tq, stored_scores):
        tq = min(tq, t - WINDOW)
        assert tk % tq == 0 and WINDOW % tq == 0
        wbias = _window_bias(tq)
        rows = GROUP * tq
        scratch = [pltpu.VMEM((rows, LANES), F32)]
        if stored_scores:
            scratch = [pltpu.VMEM((t // tk, rows, tk), F32), pltpu.VMEM((rows, LANES), F32)] + scratch
        return pl.pallas_call(
            functools.partial(kern, tq=tq, tk=tk),
            grid=(b, N_KVH),
            in_specs=[
                pl.BlockSpec((1, t, gw), lambda i, h: (i, 0, h)),
                pl.BlockSpec((1, t, LANES), lambda i, h: (i, 0, 0)),
                cmp_spec, cmp_spec, kv_spec, kv_spec, kv_spec, kv_spec,
                _resident(ov.shape), _resident(wbias.shape), _resident(spread.shape),
            ],
            out_specs=pl.BlockSpec((1, t, gw), lambda i, h: (i, 0, h)),
            out_shape=jax.ShapeDtypeStruct((b, t, N_QH * HEAD_DIM), BF16),
            scratch_shapes=scratch,
            compiler_params=_params("parallel", "parallel"),
            name=name,
        )(q, gates, cmpk, cmpv, selk, selv, wink, winv, ov, wbias, spread)

    return lax.cond(
        jnp.max(score_bound) <= MAX_SINGLE_PASS_SCORE,
        lambda: call(_nsa_attn_bounded_kernel, "nsa_attn_bounded", Q_TILE_SINGLE_PASS, False),
        lambda: call(_nsa_attn_kernel, "nsa_attn", Q_TILE, True),
    )


def mlstm_mixer(h2, b, t, g, w_in, b_if, g_head):
    qk, v, og, gates = mlstm_proj(h2, g, w_in, b_if)
    L = MLSTM_CHUNK
    hp = MLSTM_HEADS_PER_STEP
    gates_t = gates[:, :2 * M_HEADS].reshape(b, t // L, L, 2, M_HEADS // hp, hp).transpose(0, 4, 3, 5, 1, 2)
    hn = mlstm_chunk(qk.reshape(b, t, -1), v.reshape(b, t, -1), og.reshape(b, t, -1), gates_t, g_head)
    return hn.reshape(b * t, -1)


def nsa_shared_kv(h2, b, t, kv_norm, kv_w, cmp_pe, cmp_w1, cmp_w2, k_norm):
    cmp_x, *packs = kv_proj(h2, t, kv_norm, kv_w, k_norm)
    cmpk, cmpv = cmp_kv(cmp_x, b, t, cmp_pe, cmp_w1, cmp_w2, k_norm[0])
    return (cmpk, cmpv) + tuple(p.reshape(b, t, -1) for p in packs)


def nsa_mixer(h2, b, t, kv, g, w_in, q_norm, k_norm):
    q, gates = nsa_proj(h2, g, w_in, q_norm)
    q_len = jnp.max(jnp.abs(q_norm)) * (LOG2E * 1.01)
    k_len = jnp.max(jnp.abs(k_norm), axis=1) * (HEAD_DIM ** 0.5)
    o = nsa_attn(q.reshape(b, t, -1), gates.reshape(b, t, -1), kv, (q_len * k_len).astype(F32))
    return o.reshape(b * t, -1)


def kernel(x, ffn_norm, ffn_w_in, ffn_w_out, mix_norm, a_w_in, a_b_if, a_g_head, a_w_out, kv_norm, kv_w, cmp_pe, cmp_w1, cmp_w2, k_norm, b_w_in, b_q_norm, b_w_out):
    b, t, d = x.shape
    depth = ffn_norm.shape[0]
    n_a = a_w_in.shape[0]
    h = x.reshape(b * t, d)
    w_in_bf, w_out_bf = ffn_w_in, ffn_w_out
    kv = None
    for layer in range(depth):
        h = ffn(h, ffn_norm[layer, 0], w_in_bf, w_out_bf, layer, 0)
        if layer < n_a:
            mixed = mlstm_mixer(h, b, t, mix_norm[layer], a_w_in[layer], a_b_if[layer], a_g_head[layer])
            w_mix = a_w_out[layer]
        else:
            j = layer - n_a
            mixed = nsa_mixer(h, b, t, kv, mix_norm[layer], b_w_in[j], b_q_norm[j], k_norm)
            w_mix = b_w_out[j]
        h = ffn(h, ffn_norm[layer, 1], w_in_bf, w_out_bf, layer, 1, mixer=(mixed, w_mix))
        if layer == n_a - 1:
            kv = nsa_shared_kv(h, b, t, kv_norm, kv_w, cmp_pe, cmp_w1, cmp_w2, k_norm)
    return h.reshape(b, t, d)
```

```python
import functools

import jax
import jax.numpy as jnp
import numpy as np
from jax import lax
from jax.experimental import pallas as pl
from jax.experimental.pallas import tpu as pltpu

F32 = jnp.float32
BF16 = jnp.bfloat16

EPS = 1e-6
NEG = -1e30

M_HEADS = 8
M_DV = 128
M_DQK = 64

N_QH = 16
N_KVH = 4
GROUP = N_QH // N_KVH
HEAD_DIM = 64
CMP_BLOCK = 32
CMP_STRIDE = 16
CMP_HIDDEN = 256
SEL_BLOCK = 64
SEL_TOPK = 16
WINDOW = 512
FORCE_SCORE = 1e4
LOG2E = 1.4426950408889634
SEL_BIAS = -2.0 ** 100
SEL_BLOCK_SHIFT = 6
MAX_SINGLE_PASS_SCORE = 48.0

LANES = 128
VMEM_LIMIT = 56 * 1024 * 1024

TOKEN_TILE = 512
FFN_CHUNK = 256
MLSTM_CHUNK = 256
MLSTM_HEADS_PER_STEP = 4
Q_TILE = 256
Q_TILE_SINGLE_PASS = 512
WINDOW_Q_TILE = 256
SEL_KEY_CHUNK = 512

_NT = (((1,), (1,)), ((), ()))


def _params(*sem):
    return pltpu.CompilerParams(dimension_semantics=sem, vmem_limit_bytes=VMEM_LIMIT)


def _rms(x, g):
    ms = jnp.mean(x * x, axis=-1, keepdims=True)
    return (x * lax.rsqrt(ms + EPS)) * g


def _dot(a, b):
    return jnp.dot(a, b, preferred_element_type=F32)


def _dot_nt(a, b, precision=None):
    return lax.dot_general(a, b, _NT, precision=precision, preferred_element_type=F32)


def _resident(shape):
    nd = len(shape)
    return pl.BlockSpec(shape, lambda *_: (0,) * nd)


def _ffn_kernel(*refs, mixer_proj):
    if mixer_proj:
        x_ref, a_ref, wmix_ref, g_ref, win_ref, wout_ref, o_ref, xn_ref, acc_ref, x_scr = refs
        x_scr[...] = x_ref[...] + _dot(a_ref[...], wmix_ref[...])
        x_ref = x_scr
    else:
        x_ref, g_ref, win_ref, wout_ref, o_ref, xn_ref, acc_ref = refs
    f = wout_ref.shape[0]
    x = x_ref[...]
    xn_ref[...] = _rms(x, g_ref[...]).astype(BF16)
    for c in range(f // FFN_CHUNK):
        cols = slice(c * FFN_CHUNK, (c + 1) * FFN_CHUNK)
        up_cols = slice(f + c * FFN_CHUNK, f + (c + 1) * FFN_CHUNK)
        xn = xn_ref[...]
        a = _dot(xn, win_ref[:, cols].astype(BF16))
        b = _dot(xn, win_ref[:, up_cols].astype(BF16))
        hid = ((a * jax.nn.sigmoid(a)) * b).astype(BF16)
        part = _dot(hid, wout_ref[cols, :].astype(BF16))
        if c == 0:
            acc_ref[...] = part
        else:
            acc_ref[...] += part
    o_ref[...] = x_ref[...] + 0.5 * acc_ref[...]


def ffn(x2, g, w_in_all, w_out_all, layer, j, mixer=None):
    n, d = x2.shape
    f = w_out_all.shape[2]
    assert f % FFN_CHUNK == 0
    tm = TOKEN_TILE
    tile = lambda w: pl.BlockSpec((tm, w), lambda i: (i, 0))
    ins, specs, scratch = [x2], [tile(d)], [pltpu.VMEM((tm, d), BF16), pltpu.VMEM((tm, d), F32)]
    if mixer is not None:
        a2, w_mix = mixer
        ins += [a2, w_mix.astype(BF16)]
        specs += [tile(a2.shape[1]), _resident(w_mix.shape)]
        scratch.append(pltpu.VMEM((tm, d), F32))
    return pl.pallas_call(
        functools.partial(_ffn_kernel, mixer_proj=mixer is not None),
        grid=(n // tm,),
        in_specs=specs + [
            _resident((1, d)),
            pl.BlockSpec((None, None, d, 2 * f), lambda i: (layer, j, 0, 0), pipeline_mode=pl.Buffered(1)),
            pl.BlockSpec((None, None, f, d), lambda i: (layer, j, 0, 0), pipeline_mode=pl.Buffered(1)),
        ],
        out_specs=tile(d),
        out_shape=jax.ShapeDtypeStruct((n, d), F32),
        scratch_shapes=scratch,
        compiler_params=_params("parallel"),
        name="ffn",
    )(*ins, g.reshape(1, d), w_in_all, w_out_all)


def _mlstm_proj_kernel(x_ref, g_ref, wqk_ref, wv_ref, wog_ref, wg_ref, bg_ref,
                       qk_ref, v_ref, og_ref, gate_ref):
    xn = _rms(x_ref[...], g_ref[...]).astype(BF16)
    qk = _dot(xn, wqk_ref[...])
    is_q = lax.broadcasted_iota(jnp.int32, (1, qk.shape[1]), 1) < qk.shape[1] // 2
    qk_ref[...] = (qk * jnp.where(is_q, M_DQK ** -0.5, 1.0)).astype(BF16)
    v_ref[...] = _dot(xn, wv_ref[...]).astype(BF16)
    og_ref[...] = jax.nn.sigmoid(_dot(xn, wog_ref[...])).astype(BF16)
    gate_ref[...] = _dot(xn, wg_ref[...]) + bg_ref[...]


def mlstm_proj(h2, g, w_in, b_if):
    n, d = h2.shape
    qk_w, v_w = M_HEADS * M_DQK, M_HEADS * M_DV
    wb = w_in.astype(BF16)
    wqk = wb[:, :2 * qk_w]
    wv = wb[:, 2 * qk_w:2 * qk_w + v_w]
    wgate = jnp.pad(wb[:, 2 * qk_w + v_w:2 * qk_w + v_w + 2 * M_HEADS], ((0, 0), (0, LANES - 2 * M_HEADS)))
    wog = wb[:, 2 * qk_w + v_w + 2 * M_HEADS:]
    bg = jnp.pad(b_if.astype(F32).reshape(1, 2 * M_HEADS), ((0, 0), (0, LANES - 2 * M_HEADS)))
    tm = TOKEN_TILE
    tile = lambda w: pl.BlockSpec((tm, w), lambda i: (i, 0))
    return pl.pallas_call(
        _mlstm_proj_kernel,
        grid=(n // tm,),
        in_specs=[tile(d), _resident((1, d)), _resident(wqk.shape), _resident(wv.shape),
                  _resident(wog.shape), _resident(wgate.shape), _resident(bg.shape)],
        out_specs=[tile(2 * qk_w), tile(v_w), tile(d), tile(LANES)],
        out_shape=[jax.ShapeDtypeStruct((n, 2 * qk_w), BF16), jax.ShapeDtypeStruct((n, v_w), BF16),
                   jax.ShapeDtypeStruct((n, d), BF16), jax.ShapeDtypeStruct((n, LANES), F32)],
        compiler_params=_params("parallel"),
        name="mlstm_proj",
    )(h2, g.reshape(1, d), wqk, wv, wog, wgate, bg)


def _log_sigmoid(x):
    return jnp.minimum(x, 0.0) - jnp.log1p(jnp.exp(-jnp.abs(x)))


def _mlstm_chunk_kernel(q_ref, k_ref, v_ref, og_ref, gate_ref, ghead_ref, o_ref, c_ref, g_r_ref, b_r_ref, cm_r_ref):
    heads = c_ref.shape[0]
    first_head = pl.program_id(1) * heads
    L = MLSTM_CHUNK
    n_chunks = q_ref.shape[1] // L

    r_i = lax.broadcasted_iota(jnp.int32, (L, L), 0)
    c_i = lax.broadcasted_iota(jnp.int32, (L, L), 1)
    causal = c_i <= r_i
    tril = causal.astype(F32)
    lane = lax.broadcasted_iota(jnp.int32, (L, LANES), 1)
    ones_blk = jnp.ones((L, LANES), BF16)

    c_ref[...] = jnp.zeros_like(c_ref)

    t_i = lax.broadcasted_iota(jnp.int32, (n_chunks, L), 1)
    for hh in range(heads):
        b = _dot_nt(_log_sigmoid(gate_ref[0, 0, 1, hh]), tril, lax.Precision.HIGHEST)
        g = gate_ref[0, 0, 0, hh] - b
        cm = g
        shift = 1
        while shift < L:
            cm = jnp.maximum(cm, jnp.where(t_i >= shift, pltpu.roll(cm, shift, 1), -jnp.inf))
            shift *= 2
        b_r_ref[hh] = b
        g_r_ref[hh] = g
        cm_r_ref[hh] = cm

    def column(row):
        return jnp.broadcast_to(row, (LANES, L)).T

    def body(c, m_prev):
        base = pl.multiple_of(c * L, L)
        m_next = []
        for hh in range(heads):
            head = first_head + hh
            pair_cols = slice((hh // 2) * LANES, (hh // 2 + 1) * LANES)
            in_head = (lane >= (hh % 2) * M_DQK) & (lane < (hh % 2 + 1) * M_DQK)
            qh = q_ref[0, pl.ds(base, L), pair_cols]
            kh = jnp.where(in_head, k_ref[0, pl.ds(base, L), pair_cols], 0)
            v_aug = jnp.concatenate([v_ref[0, pl.ds(base, L), hh * M_DV:(hh + 1) * M_DV], ones_blk], axis=1)

            g_r = g_r_ref[hh, pl.ds(c, 1), :]
            b_r = b_r_ref[hh, pl.ds(c, 1), :]
            g_tot = b_r[:, L - 1:L]
            m_p = m_prev[hh]
            c_prev = c_ref[hh]

            a_r = g_tot + g_r
            m_loc = jnp.max(a_r, axis=1, keepdims=True)
            kw_t = (kh.astype(F32).T * jnp.exp(a_r - m_loc)).astype(BF16)
            c_loc = _dot(kw_t, v_aug)

            mm = jnp.maximum(m_p, column(cm_r_ref[hh, pl.ds(c, 1), :]))
            mm_wide = jnp.concatenate([mm] * (L // LANES), axis=1)
            w = jnp.where(causal, jnp.exp(g_r - mm_wide), 0.0) * _dot_nt(qh, kh)
            num = jnp.concatenate([jnp.exp(m_p - mm)] * 2, axis=1) * _dot(qh, c_prev.astype(BF16)) \
                + _dot(w.astype(BF16), v_aug)
            floor = jnp.exp(-(column(b_r) + mm))
            h_out = num[:, :M_DV] / jnp.maximum(jnp.abs(num[:, M_DV:]), floor)
            h_out = _rms(h_out, ghead_ref[pl.ds(head, 1), :])
            cols = slice(hh * M_DV, (hh + 1) * M_DV)
            o_ref[0, pl.ds(base, L), cols] = (h_out * og_ref[0, pl.ds(base, L), cols]).astype(BF16)

            m_new = jnp.maximum(g_tot + m_p, m_loc)
            c_ref[hh] = jnp.exp(g_tot + m_p - m_new) * c_prev + jnp.exp(m_loc - m_new) * c_loc
            m_next.append(m_new)
        return tuple(m_next)

    lax.fori_loop(0, n_chunks, body, (jnp.zeros((1, 1), F32),) * heads)


def mlstm_chunk(qk, v, og, gates_t, g_head):
    b, t, _ = v.shape
    L = MLSTM_CHUNK
    nc = t // L
    hp = MLSTM_HEADS_PER_STEP
    groups = M_HEADS // hp
    qk_cols, v_cols = hp * M_DQK, hp * M_DV
    row_scratch = pltpu.VMEM((hp, nc, L), F32)
    return pl.pallas_call(
        _mlstm_chunk_kernel,
        grid=(b, groups),
        in_specs=[
            pl.BlockSpec((1, t, qk_cols), lambda i, p: (i, 0, p)),
            pl.BlockSpec((1, t, qk_cols), lambda i, p: (i, 0, groups + p)),
            pl.BlockSpec((1, t, v_cols), lambda i, p: (i, 0, p)),
            pl.BlockSpec((1, t, v_cols), lambda i, p: (i, 0, p)),
            pl.BlockSpec((1, 1, 2, hp, nc, L), lambda i, p: (i, p, 0, 0, 0, 0)),
            _resident((M_HEADS, M_DV)),
        ],
        out_specs=pl.BlockSpec((1, t, v_cols), lambda i, p: (i, 0, p)),
        out_shape=jax.ShapeDtypeStruct((b, t, M_HEADS * M_DV), BF16),
        scratch_shapes=[pltpu.VMEM((hp, 2 * M_DQK, 2 * M_DV), F32), row_scratch, row_scratch, row_scratch],
        compiler_params=_params("parallel", "parallel"),
        name="mlstm_chunk",
    )(qk, qk, v, og, gates_t, g_head.astype(F32))


def _kv_packs(y, gain, extra):
    lane = lax.broadcasted_iota(jnp.int32, (1, LANES), 1)
    lo = lane < HEAD_DIM
    kp, vp = [], []
    for h in range(N_KVH):
        yh = y[:, h * LANES:(h + 1) * LANES]
        ms = jnp.sum(jnp.where(lo, yh * yh, 0.0), axis=-1, keepdims=True) * (1.0 / HEAD_DIM)
        kp.append(jnp.where(lo, (yh * lax.rsqrt(ms + EPS)) * gain, extra))
        vp.append(jnp.where(lo, 1.0, yh))
    return jnp.concatenate(kp, axis=1), jnp.concatenate(vp, axis=1)


def _kv_proj_kernel(x_ref, g_ref, wc_ref, ws_ref, ww_ref, gs_ref, gw_ref,
                    cmp_ref, selk_ref, selv_ref, wink_ref, winv_ref, y_ref, *, seq_len):
    tm = x_ref.shape[0]
    xn = _rms(x_ref[...], g_ref[...]).astype(BF16)

    y = _dot(xn, wc_ref[...])
    n_slab = y.shape[1] // LANES
    for s in range(n_slab):
        y_ref[s] = y[:, s * LANES:(s + 1) * LANES]
    groups = tm // CMP_STRIDE
    lo = lax.broadcasted_iota(jnp.int32, (1, LANES), 1) < HEAD_DIM
    for s in range(n_slab):
        for rp in range(CMP_STRIDE // 2):
            even = y_ref[s, pl.ds(2 * rp, groups, stride=CMP_STRIDE), :]
            odd = y_ref[s, pl.ds(2 * rp + 1, groups, stride=CMP_STRIDE), :]
            cols = slice(rp * LANES, (rp + 1) * LANES)
            cmp_ref[2 * s, :, cols] = jnp.where(lo, even, pltpu.roll(odd, HEAD_DIM, 1))
            cmp_ref[2 * s + 1, :, cols] = jnp.where(lo, pltpu.roll(even, HEAD_DIM, 1), odd)

    pos = (pl.program_id(0) * tm) % seq_len + lax.broadcasted_iota(jnp.int32, (tm, 1), 0)
    lane = lax.broadcasted_iota(jnp.int32, (1, LANES), 1)
    block_bias = jnp.where(lane == HEAD_DIM + lax.shift_right_logical(pos, SEL_BLOCK_SHIFT), SEL_BIAS, 0.0)
    kp, vp = _kv_packs(_dot(xn, ws_ref[...]), gs_ref[...], block_bias)
    selk_ref[...] = kp.astype(BF16)
    selv_ref[...] = vp.astype(BF16)
    kp, vp = _kv_packs(_dot(xn, ww_ref[...]), gw_ref[...], 0.0)
    wink_ref[...] = kp.astype(BF16)
    winv_ref[...] = vp.astype(BF16)


def _kv_pack_weights(wk, wv):
    d = wk.shape[0]
    wk = wk.reshape(d, N_KVH, HEAD_DIM)
    wv = wv.reshape(d, N_KVH, HEAD_DIM)
    return jnp.concatenate([wk, wv], axis=-1).reshape(d, N_KVH * LANES)


def _k_gain_row(gain):
    return jnp.concatenate([gain.astype(F32), jnp.ones((HEAD_DIM,), F32)]).reshape(1, LANES)


def kv_proj(h2, t, g, kv_w, k_norm):
    n, d = h2.shape
    wb = kv_w.astype(BF16)
    kc, vc, ks, vs, kw, vw = jnp.split(wb, 6, axis=1)
    wcmp = jnp.concatenate([kc, vc], axis=1)
    wsel = _kv_pack_weights(ks, vs)
    wwin = _kv_pack_weights(kw, vw)
    wd = N_KVH * LANES
    tm = TOKEN_TILE
    assert t % tm == 0 and t // SEL_BLOCK <= LANES - HEAD_DIM
    tile = lambda w: pl.BlockSpec((tm, w), lambda i: (i, 0))
    packed = jax.ShapeDtypeStruct((n, wd), BF16)
    feat = CMP_STRIDE * HEAD_DIM
    groups = tm // CMP_STRIDE
    return pl.pallas_call(
        functools.partial(_kv_proj_kernel, seq_len=t),
        grid=(n // tm,),
        in_specs=[tile(d), _resident((1, d)), _resident(wcmp.shape), _resident(wsel.shape),
                  _resident(wwin.shape), _resident((1, LANES)), _resident((1, LANES))],
        out_specs=[pl.BlockSpec((2 * N_KVH, groups, feat), lambda i: (0, i, 0))] + [tile(wd)] * 4,
        out_shape=[jax.ShapeDtypeStruct((2 * N_KVH, n // CMP_STRIDE, feat), F32), packed, packed, packed, packed],
        scratch_shapes=[pltpu.VMEM((2 * N_KVH * HEAD_DIM // LANES, tm, LANES), F32)],
        compiler_params=_params("parallel"),
        name="kv_proj",
    )(h2, g.reshape(1, d), wcmp, wsel, wwin, _k_gain_row(k_norm[1]), _k_gain_row(k_norm[2]))


def _cmp_kernel(xk_ref, xv_ref, pe_ref, w1_ref, w2k_ref, w2v_ref, gk_ref, ok_ref, ov_ref):
    def compress(x, kv, w2):
        u = _dot((x + pe_ref[kv, 0]).astype(BF16), w1_ref[kv, 0])
        v = _dot((x + pe_ref[kv, 1]).astype(BF16), w1_ref[kv, 1])
        nrow = v.shape[0]
        hid = u + pltpu.roll(v, nrow - 1, 0)
        return _dot((hid * jax.nn.sigmoid(hid)).astype(BF16), w2)

    yk = compress(xk_ref[0], 0, w2k_ref[...])
    yv = compress(xv_ref[0], 1, w2v_ref[...])
    ms = jnp.sum(yk * yk, axis=-1, keepdims=True) * (1.0 / HEAD_DIM)
    ok_ref[0, 0] = ((yk * lax.rsqrt(ms + EPS)) * gk_ref[...]).astype(BF16)
    lane = lax.broadcasted_iota(jnp.int32, (1, LANES), 1)
    ov_ref[0, 0] = jnp.where(lane < HEAD_DIM, 1.0, yv).astype(BF16)


def cmp_kv(cmp_x, b, t, cmp_pe, cmp_w1, cmp_w2, k_gain):
    ng = t // CMP_STRIDE
    feat = CMP_STRIDE * HEAD_DIM
    pe = cmp_pe.astype(F32).reshape(2, 2, 1, feat)
    w1 = cmp_w1.astype(BF16).reshape(2, 2, feat, CMP_HIDDEN)
    zeros = jnp.zeros((CMP_HIDDEN, HEAD_DIM), BF16)
    w2k = jnp.concatenate([cmp_w2[0].astype(BF16), zeros], axis=1)
    w2v = jnp.concatenate([zeros, cmp_w2[1].astype(BF16)], axis=1)
    ospec = pl.BlockSpec((1, 1, ng, LANES), lambda i, h: (i, h, 0, 0))
    packed = jax.ShapeDtypeStruct((b, N_KVH, ng, LANES), BF16)
    return pl.pallas_call(
        _cmp_kernel,
        grid=(b, N_KVH),
        in_specs=[pl.BlockSpec((1, ng, feat), lambda i, h: (h, i, 0)),
                  pl.BlockSpec((1, ng, feat), lambda i, h: (N_KVH + h, i, 0)),
                  _resident(pe.shape), _resident(w1.shape), _resident(w2k.shape),
                  _resident(w2v.shape), _resident((1, LANES))],
        out_specs=[ospec, ospec],
        out_shape=[packed, packed],
        compiler_params=_params("parallel", "parallel"),
        name="cmp_kv",
    )(cmp_x, cmp_x, pe, w1, w2k, w2v, _k_gain_row(k_gain))


def _nsa_proj_kernel(x_ref, g_ref, wq_ref, wg_ref, qg_ref, q_ref, gate_ref):
    xn = _rms(x_ref[...], g_ref[...]).astype(BF16)
    q = _dot(xn, wq_ref[...])
    lane = lax.broadcasted_iota(jnp.int32, (1, LANES), 1)
    lo = lane < HEAD_DIM
    outs = []
    for cb in range(q.shape[1] // LANES):
        y = q[:, cb * LANES:(cb + 1) * LANES]
        y2 = y * y
        ms_lo = jnp.sum(jnp.where(lo, y2, 0.0), axis=-1, keepdims=True) * (1.0 / HEAD_DIM)
        ms_hi = jnp.sum(jnp.where(lo, 0.0, y2), axis=-1, keepdims=True) * (1.0 / HEAD_DIM)
        scale = jnp.where(lo, lax.rsqrt(ms_lo + EPS), lax.rsqrt(ms_hi + EPS))
        outs.append(((y * scale) * qg_ref[...]) * (HEAD_DIM ** -0.5 * LOG2E))
    q_ref[...] = jnp.concatenate(outs, axis=1).astype(BF16)
    gate_ref[...] = jax.nn.sigmoid(_dot(xn, wg_ref[...])).astype(BF16)


def nsa_proj(h2, g, w_in, q_norm):
    n, d = h2.shape
    wb = w_in.astype(BF16)
    wq = wb[:, :N_QH * HEAD_DIM]
    wg = wb[:, N_QH * HEAD_DIM:].reshape(d, 3, N_KVH, GROUP).transpose(0, 2, 1, 3).reshape(d, N_KVH * 3 * GROUP)
    wg = jnp.pad(wg, ((0, 0), (0, LANES - N_KVH * 3 * GROUP)))
    qg = jnp.tile(q_norm.astype(F32), LANES // HEAD_DIM).reshape(1, LANES)
    tm = TOKEN_TILE
    tile = lambda w: pl.BlockSpec((tm, w), lambda i: (i, 0))
    return pl.pallas_call(
        _nsa_proj_kernel,
        grid=(n // tm,),
        in_specs=[tile(d), _resident((1, d)), _resident(wq.shape), _resident(wg.shape), _resident((1, LANES))],
        out_specs=[tile(N_QH * HEAD_DIM), tile(LANES)],
        out_shape=[jax.ShapeDtypeStruct((n, N_QH * HEAD_DIM), BF16),
                   jax.ShapeDtypeStruct((n, LANES), BF16)],
        compiler_params=_params("parallel"),
        name="nsa_proj",
    )(h2, g.reshape(1, d), wq, wg, qg)


def _masked_exp2(s3, mask):
    sm = jnp.where(mask[None], s3, NEG)
    mx = jnp.max(sm, axis=-1, keepdims=True)
    return jnp.exp2(sm - mx), mx > 0.5 * NEG


def _normalise(acc, values_lo, guard_empty=False):
    lo = lax.broadcasted_iota(jnp.int32, acc.shape, 1) < HEAD_DIM
    den = pltpu.roll(acc, HEAD_DIM, 1)
    use = lo == values_lo
    if guard_empty:
        use = use & (den > 0.0)
    return acc / jnp.where(use, den, 1.0)


def _stacked_queries(q_ref, s0, tq):
    lo = lax.broadcasted_iota(jnp.int32, (tq, LANES), 1) < HEAD_DIM
    qs = []
    for g in range(GROUP):
        blk = q_ref[0, pl.ds(s0, tq), (g // 2) * LANES:(g // 2 + 1) * LANES].astype(F32)
        if g % 2 == 1:
            blk = pltpu.roll(blk, HEAD_DIM, 1)
        qs.append(jnp.where(lo, blk, 0.0))
    return jnp.concatenate(qs, axis=0)


def _dropped_blocks(imp, s0, tq, n_sel):
    lane = lax.broadcasted_iota(jnp.int32, (tq, LANES), 1)
    t_q = s0 + lax.broadcasted_iota(jnp.int32, (tq, 1), 0)
    valid = (lane * SEL_BLOCK <= t_q) & (lane < n_sel)
    cur = lax.shift_right_logical(t_q, SEL_BLOCK_SHIFT)
    forced = (lane == 0) | (lane == cur) | (lane == cur - 1)
    score = jnp.where(forced & valid, FORCE_SCORE, jnp.where(valid, imp, -1.0))
    score = jnp.where(lane < n_sel, score, -2.0)
    s_t = score.T[:HEAD_DIM]
    sub = lax.broadcasted_iota(jnp.int32, (8, tq), 0)
    groups = [s_t[8 * v:8 * v + 8] for v in range(HEAD_DIM // 8)]
    ranks = [jnp.zeros((8, tq), F32) for _ in groups]
    for i in range(n_sel):
        s_i = s_t[i:i + 1, :]
        for v, s_v in enumerate(groups):
            if 8 * v + 7 < i:
                ahead = s_i > s_v
            elif 8 * v > i:
                ahead = s_i >= s_v
            else:
                ahead = (s_i > s_v) | ((s_i == s_v) & (sub > i - 8 * v))
            ranks[v] = ranks[v] + jnp.where(ahead, 1.0, 0.0)
    rank = jnp.concatenate(ranks, axis=0)
    blk_t = lax.broadcasted_iota(jnp.int32, (HEAD_DIM, tq), 0)
    t_row = s0 + lax.broadcasted_iota(jnp.int32, (1, tq), 1)
    dropped_t = jnp.where((rank < SEL_TOPK) & (blk_t * SEL_BLOCK <= t_row), 0.0, 1.0)
    return jnp.concatenate([jnp.zeros((HEAD_DIM, tq), F32), dropped_t], axis=0).T


def _store_gated(o_ref, gate_ref, spread_ref, s0, tq, o_c, o_s, o_w):
    g_rep = _dot(gate_ref[0, pl.ds(s0, tq), :], spread_ref[pl.program_id(1)])

    def gate(branch):
        first = branch * GROUP
        return jnp.concatenate([g_rep[:, (first + g) * LANES:(first + g + 1) * LANES] for g in range(GROUP)], axis=0)

    o = (gate(0) * o_c + gate(1) * o_s + gate(2) * o_w).reshape(GROUP, tq, LANES)
    lo = lax.broadcasted_iota(jnp.int32, (tq, LANES), 1) < HEAD_DIM
    out01 = jnp.where(lo, pltpu.roll(o[0], HEAD_DIM, 1), o[1])
    out23 = jnp.where(lo, pltpu.roll(o[2], HEAD_DIM, 1), o[3])
    o_ref[0, pl.ds(s0, tq), :] = jnp.concatenate([out01, out23], axis=1).astype(BF16)


def _chunk_groups(n, per_chunk):
    def trip(i, carry):
        for u in range(4):
            per_chunk(4 * i + u)
        return carry

    lax.fori_loop(0, n // 4, trip, 0)
    done = 4 * (n // 4)

    @pl.when(n - done >= 2)
    def _():
        per_chunk(done)
        per_chunk(done + 1)

    @pl.when((n - done) % 2 == 1)
    def _():
        per_chunk(n - 1)


def _nsa_attn_bounded_kernel(q_ref, gate_ref, cmpk_ref, cmpv_ref, selk_ref, selv_ref, wink_ref, winv_ref,
                             ov_ref, wbias_ref, spread_ref, o_ref, acc_ref, *, tq, tk):
    rows = GROUP * tq
    n_sel = selk_ref.shape[1] // SEL_BLOCK

    def tile(tile_idx, carry):
        s0 = pl.multiple_of(tile_idx * tq, tq)
        q4 = _stacked_queries(q_ref, s0, tq)
        qst = q4.astype(BF16)
        t_q = s0 + lax.broadcasted_iota(jnp.int32, (tq, 1), 0)

        kc = cmpk_ref[0, 0]
        n_cmp = kc.shape[0]
        cmp_end = lax.broadcasted_iota(jnp.int32, (1, n_cmp), 1) * CMP_STRIDE + (CMP_BLOCK - 1)
        s_c = jnp.where((cmp_end <= t_q)[None], _dot_nt(qst, kc).reshape(GROUP, tq, n_cmp), NEG)
        e_c = jnp.exp2(s_c).reshape(rows, n_cmp).astype(BF16)
        both = _dot(e_c, jnp.concatenate([cmpv_ref[0, 0], ov_ref[...]], axis=1))
        o_c = _normalise(both[:, :LANES], False, guard_empty=True)
        imp4 = _normalise(both[:, LANES:], True, guard_empty=True).reshape(GROUP, tq, LANES)
        dropped = _dropped_blocks(imp4[0] + imp4[1] + imp4[2] + imp4[3], s0, tq, n_sel)
        qx = (q4 + jnp.concatenate([dropped] * GROUP, axis=0)).astype(BF16)

        tw = wbias_ref.shape[1]
        wk = WINDOW + tw
        q3 = qst.reshape(GROUP, tq, LANES)
        o_w = []
        for sub in range(tq // tw):
            s_sub = s0 + sub * tw
            w0 = pl.multiple_of(jnp.maximum(s_sub - WINDOW, 0), tw)
            q_sub = q3[:, sub * tw:(sub + 1) * tw].reshape(GROUP * tw, LANES)
            s_w = _dot_nt(q_sub, wink_ref[0, pl.ds(w0, wk), :]).reshape(GROUP, tw, wk)
            s_w = s_w + wbias_ref[jnp.minimum(tile_idx * (tq // tw) + sub, WINDOW // tw)][None]
            e_w = jnp.exp2(s_w).reshape(GROUP * tw, wk).astype(BF16)
            o_w.append(_normalise(_dot(e_w, winv_ref[0, pl.ds(w0, wk), :]), False).reshape(GROUP, tw, LANES))
        o_w = jnp.concatenate(o_w, axis=1).reshape(rows, LANES)

        last = s0 // tk
        acc_ref[...] = jnp.zeros_like(acc_ref)

        def accumulate(c, causal=False):
            base = pl.multiple_of(c * tk, tk)
            s = _dot_nt(qx, selk_ref[0, pl.ds(base, tk), :])
            if causal:
                kpos = base + lax.broadcasted_iota(jnp.int32, (1, tk), 1)
                s = jnp.where((kpos <= t_q)[None], s.reshape(GROUP, tq, tk), NEG).reshape(rows, tk)
            acc_ref[...] += _dot(jnp.exp2(s).astype(BF16), selv_ref[0, pl.ds(base, tk), :])

        _chunk_groups(last, accumulate)
        accumulate(last, causal=True)
        o_s = _normalise(acc_ref[...], False)

        _store_gated(o_ref, gate_ref, spread_ref, s0, tq, o_c, o_s, o_w)
        return carry

    lax.fori_loop(0, q_ref.shape[1] // tq, tile, 0)


def _nsa_attn_kernel(q_ref, gate_ref, cmpk_ref, cmpv_ref, selk_ref, selv_ref, wink_ref, winv_ref,
                     ov_ref, wbias_ref, spread_ref, o_ref, s_ref, mx_ref, acc_ref, *, tq, tk):
    rows = GROUP * tq

    def tile(tile_idx, carry):
        s0 = pl.multiple_of(tile_idx * tq, tq)

        lane = lax.broadcasted_iota(jnp.int32, (tq, LANES), 1)
        lo = lane < HEAD_DIM
        qs = []
        for g in range(GROUP):
            blk = q_ref[0, pl.ds(s0, tq), (g // 2) * LANES:(g // 2 + 1) * LANES].astype(F32)
            if g % 2 == 1:
                blk = pltpu.roll(blk, HEAD_DIM, 1)
            qs.append(jnp.where(lo, blk, 0.0))
        q4 = jnp.concatenate(qs, axis=0)
        qst = q4.astype(BF16)
        t_q = s0 + lax.broadcasted_iota(jnp.int32, (tq, 1), 0)

        kc = cmpk_ref[0, 0]
        n_cmp = kc.shape[0]
        cmp_end = lax.broadcasted_iota(jnp.int32, (1, n_cmp), 1) * CMP_STRIDE + (CMP_BLOCK - 1)
        e_c, ok_c = _masked_exp2(_dot_nt(qst, kc).reshape(GROUP, tq, n_cmp), cmp_end <= t_q)
        e_c = e_c.reshape(rows, n_cmp)
        ok_c = ok_c.reshape(rows, 1)
        both = _dot(e_c.astype(BF16), jnp.concatenate([cmpv_ref[0, 0], ov_ref[...]], axis=1))
        o_c = jnp.where(ok_c, _normalise(both[:, :LANES], False), 0.0)

        imp4 = jnp.where(ok_c, _normalise(both[:, LANES:], True), 0.0).reshape(GROUP, tq, LANES)
        imp = imp4[0] + imp4[1] + imp4[2] + imp4[3]

        n_sel = selk_ref.shape[1] // SEL_BLOCK
        valid = (lane * SEL_BLOCK <= t_q) & (lane < n_sel)
        cur = lax.shift_right_logical(t_q, SEL_BLOCK_SHIFT)
        forced = (lane == 0) | (lane == cur) | (lane == cur - 1)
        score = jnp.where(forced & valid, FORCE_SCORE, jnp.where(valid, imp, -1.0))
        score = jnp.where(lane < n_sel, score, -2.0)
        s_t = score.T[:HEAD_DIM]
        sub = lax.broadcasted_iota(jnp.int32, (8, tq), 0)
        groups = [s_t[8 * v:8 * v + 8] for v in range(HEAD_DIM // 8)]
        ranks = [jnp.zeros((8, tq), F32) for _ in groups]
        for i in range(n_sel):
            s_i = s_t[i:i + 1, :]
            for v, s_v in enumerate(groups):
                if 8 * v + 7 < i:
                    ahead = s_i > s_v
                elif 8 * v > i:
                    ahead = s_i >= s_v
                else:
                    ahead = (s_i > s_v) | ((s_i == s_v) & (sub > i - 8 * v))
                ranks[v] = ranks[v] + jnp.where(ahead, 1.0, 0.0)
        rank = jnp.concatenate(ranks, axis=0)
        blk_t = lax.broadcasted_iota(jnp.int32, (HEAD_DIM, tq), 0)
        t_row = s0 + lax.broadcasted_iota(jnp.int32, (1, tq), 1)
        dropped_t = jnp.where((rank < SEL_TOPK) & (blk_t * SEL_BLOCK <= t_row), 0.0, 1.0)
        dropped = jnp.concatenate([jnp.zeros((HEAD_DIM, tq), F32), dropped_t], axis=0).T
        qx = (q4 + jnp.concatenate([dropped] * GROUP, axis=0)).astype(BF16)

        wk = WINDOW + tq
        w0 = pl.multiple_of(jnp.maximum(s0 - WINDOW, 0), tq)
        s_w = _dot_nt(qst, wink_ref[0, pl.ds(w0, wk), :]).reshape(GROUP, tq, wk)
        s_w = s_w + wbias_ref[jnp.minimum(tile_idx, WINDOW // tq)][None]
        e_w = jnp.exp2(s_w - jnp.max(s_w, axis=-1, keepdims=True))
        o_w = _normalise(_dot(e_w.reshape(rows, wk).astype(BF16), winv_ref[0, pl.ds(w0, wk), :]), False)

        nblk = tk // LANES
        last = s0 // tk
        mx_ref[...] = jnp.full(mx_ref.shape, NEG, F32)

        def scores(c, causal):
            base = pl.multiple_of(c * tk, tk)
            s = _dot_nt(qx, selk_ref[0, pl.ds(base, tk), :])
            if causal:
                kpos = base + lax.broadcasted_iota(jnp.int32, (1, tk), 1)
                s = jnp.where((kpos <= t_q)[None], s.reshape(GROUP, tq, tk), NEG).reshape(rows, tk)
            s_ref[c] = s
            m = mx_ref[...]
            for j in range(nblk):
                m = jnp.maximum(m, s[:, j * LANES:(j + 1) * LANES])
            mx_ref[...] = m

        def pass1(i, carry):
            for u in range(4):
                scores(4 * i + u, causal=False)
            return carry

        lax.fori_loop(0, last // 4, pass1, 0)
        done = 4 * (last // 4)

        @pl.when(last - done >= 2)
        def _():
            scores(done, causal=False)
            scores(done + 1, causal=False)

        @pl.when((last - done) % 2 == 1)
        def _():
            scores(last - 1, causal=False)

        scores(last, causal=True)
        mx_ref[...] = jnp.broadcast_to(jnp.max(mx_ref[...], axis=-1, keepdims=True), mx_ref.shape)

        acc_ref[...] = jnp.zeros_like(acc_ref)

        def weighted(c):
            base = pl.multiple_of(c * tk, tk)
            s = s_ref[c]
            m = mx_ref[...]
            p = jnp.concatenate([jnp.exp2(s[:, j * LANES:(j + 1) * LANES] - m) for j in range(nblk)], axis=1)
            return _dot(p.astype(BF16), selv_ref[0, pl.ds(base, tk), :])

        def pass2(i, carry):
            acc_ref[...] += (weighted(4 * i) + weighted(4 * i + 1)) + (weighted(4 * i + 2) + weighted(4 * i + 3))
            return carry

        n_chunks = last + 1
        lax.fori_loop(0, n_chunks // 4, pass2, 0)
        done2 = 4 * (n_chunks // 4)

        @pl.when(n_chunks - done2 >= 2)
        def _():
            acc_ref[...] += weighted(done2) + weighted(done2 + 1)

        @pl.when((n_chunks - done2) % 2 == 1)
        def _():
            acc_ref[...] += weighted(last)

        o_s = _normalise(acc_ref[...], False)

        g_rep = _dot(gate_ref[0, pl.ds(s0, tq), :], spread_ref[pl.program_id(1)])

        def gate(branch):
            first = branch * GROUP
            return jnp.concatenate([g_rep[:, (first + g) * LANES:(first + g + 1) * LANES] for g in range(GROUP)], axis=0)

        o = (gate(0) * o_c + gate(1) * o_s + gate(2) * o_w).reshape(GROUP, tq, LANES)
        out01 = jnp.where(lo, pltpu.roll(o[0], HEAD_DIM, 1), o[1])
        out23 = jnp.where(lo, pltpu.roll(o[2], HEAD_DIM, 1), o[3])
        o_ref[0, pl.ds(s0, tq), :] = jnp.concatenate([out01, out23], axis=1).astype(BF16)
        return carry

    lax.fori_loop(0, q_ref.shape[1] // tq, tile, 0)


def _overlap_matrix(n_cmp_rows, t):
    n_sel = t // SEL_BLOCK
    c0 = np.arange(n_cmp_rows) * CMP_STRIDE
    s_0 = np.arange(n_sel) * SEL_BLOCK
    ov = (c0[:, None] < s_0[None, :] + SEL_BLOCK) & (c0[:, None] + CMP_BLOCK > s_0[None, :])
    ov[(t - CMP_BLOCK) // CMP_STRIDE + 1:] = False
    out = np.zeros((n_cmp_rows, LANES), np.float32)
    out[:, :n_sel] = ov
    out[:, HEAD_DIM:] = 1.0
    return jnp.asarray(out, BF16)


def _window_bias(tq):
    off = np.arange(WINDOW // tq + 1).reshape(-1, 1, 1) * tq
    i = np.arange(tq).reshape(1, tq, 1)
    j = np.arange(WINDOW + tq).reshape(1, 1, -1)
    visible = (j <= off + i) & (j > off + i - WINDOW)
    return jnp.asarray(np.where(visible, 0.0, NEG).astype(np.float32))


def _gate_spread():
    n_gate = 3 * GROUP
    head = np.arange(N_KVH).reshape(N_KVH, 1, 1)
    src = np.arange(LANES).reshape(1, LANES, 1)
    dst_group = np.arange(n_gate * LANES).reshape(1, 1, -1) // LANES
    return jnp.asarray((src == n_gate * head + dst_group).astype(np.float32), BF16)


def nsa_attn(q, gates, kv, score_bound):
    cmpk, cmpv, selk, selv, wink, winv = kv
    b, t, _ = q.shape
    tk = min(SEL_KEY_CHUNK, t)
    n_cmp_rows = cmpk.shape[2]
    assert t // SEL_BLOCK <= LANES - HEAD_DIM and (t // SEL_BLOCK) % 8 == 0 and t % tk == 0
    ov = _overlap_matrix(n_cmp_rows, t)
    spread = _gate_spread()
    gw = GROUP * HEAD_DIM
    kv_spec = pl.BlockSpec((1, t, LANES), lambda i, h: (i, 0, h))
    cmp_spec = pl.BlockSpec((1, 1, n_cmp_rows, LANES), lambda i, h: (i, h, 0, 0))

    def call(kern, name, tq, stored_scores):
        tq = min(tq, t - WINDOW)
        tw = min(tq, WINDOW_Q_TILE)
        assert tk % tq == 0 and WINDOW % tw == 0 and tq % tw == 0
        wbias = _window_bias(tw)
        rows = GROUP * tq
        scratch = [pltpu.VMEM((rows, LANES), F32)]
        if stored_scores:
            scratch = [pltpu.VMEM((t // tk, rows, tk), F32), pltpu.VMEM((rows, LANES), F32)] + scratch
        return pl.pallas_call(
            functools.partial(kern, tq=tq, tk=tk),
            grid=(b, N_KVH),
            in_specs=[
                pl.BlockSpec((1, t, gw), lambda i, h: (i, 0, h)),
                pl.BlockSpec((1, t, LANES), lambda i, h: (i, 0, 0)),
                cmp_spec, cmp_spec, kv_spec, kv_spec, kv_spec, kv_spec,
                _resident(ov.shape), _resident(wbias.shape), _resident(spread.shape),
            ],
            out_specs=pl.BlockSpec((1, t, gw), lambda i, h: (i, 0, h)),
            out_shape=jax.ShapeDtypeStruct((b, t, N_QH * HEAD_DIM), BF16),
            scratch_shapes=scratch,
            compiler_params=_params("parallel", "parallel"),
            name=name,
        )(q, gates, cmpk, cmpv, selk, selv, wink, winv, ov, wbias, spread)

    return lax.cond(
        jnp.max(score_bound) <= MAX_SINGLE_PASS_SCORE,
        lambda: call(_nsa_attn_bounded_kernel, "nsa_attn_bounded", Q_TILE_SINGLE_PASS, False),
        lambda: call(_nsa_attn_kernel, "nsa_attn", Q_TILE, True),
    )


def mlstm_mixer(h2, b, t, g, w_in, b_if, g_head):
    qk, v, og, gates = mlstm_proj(h2, g, w_in, b_if)
    L = MLSTM_CHUNK
    hp = MLSTM_HEADS_PER_STEP
    gates_t = gates[:, :2 * M_HEADS].reshape(b, t // L, L, 2, M_HEADS // hp, hp).transpose(0, 4, 3, 5, 1, 2)
    hn = mlstm_chunk(qk.reshape(b, t, -1), v.reshape(b, t, -1), og.reshape(b, t, -1), gates_t, g_head)
    return hn.reshape(b * t, -1)


def nsa_shared_kv(h2, b, t, kv_norm, kv_w, cmp_pe, cmp_w1, cmp_w2, k_norm):
    cmp_x, *packs = kv_proj(h2, t, kv_norm, kv_w, k_norm)
    cmpk, cmpv = cmp_kv(cmp_x, b, t, cmp_pe, cmp_w1, cmp_w2, k_norm[0])
    return (cmpk, cmpv) + tuple(p.reshape(b, t, -1) for p in packs)


def nsa_mixer(h2, b, t, kv, g, w_in, q_norm, k_norm):
    q, gates = nsa_proj(h2, g, w_in, q_norm)
    q_len = jnp.max(jnp.abs(q_norm)) * (LOG2E * 1.01)
    k_len = jnp.max(jnp.abs(k_norm), axis=1) * (HEAD_DIM ** 0.5)
    o = nsa_attn(q.reshape(b, t, -1), gates.reshape(b, t, -1), kv, (q_len * k_len).astype(F32))
    return o.reshape(b * t, -1)


def kernel(x, ffn_norm, ffn_w_in, ffn_w_out, mix_norm, a_w_in, a_b_if, a_g_head, a_w_out, kv_norm, kv_w, cmp_pe, cmp_w1, cmp_w2, k_norm, b_w_in, b_q_norm, b_w_out):
    b, t, d = x.shape
    depth = ffn_norm.shape[0]
    n_a = a_w_in.shape[0]
    h = x.reshape(b * t, d)
    w_in_bf, w_out_bf = ffn_w_in, ffn_w_out
    kv = None
    for layer in range(depth):
        h = ffn(h, ffn_norm[layer, 0], w_in_bf, w_out_bf, layer, 0)
        if layer < n_a:
            mixed = mlstm_mixer(h, b, t, mix_norm[layer], a_w_in[layer], a_b_if[layer], a_g_head[layer])
            w_mix = a_w_out[layer]
        else:
            j = layer - n_a
            mixed = nsa_mixer(h, b, t, kv, mix_norm[layer], b_w_in[j], b_q_norm[j], k_norm)
            w_mix = b_w_out[j]
        h = ffn(h, ffn_norm[layer, 1], w_in_bf, w_out_bf, layer, 1, mixer=(mixed, w_mix))
        if layer == n_a - 1:
            kv = nsa_shared_kv(h, b, t, kv_norm, kv_w, cmp_pe, cmp_w1, cmp_w2, k_norm)
    return h.reshape(b, t, d)
```
